```python
import jax, jax.numpy as jnp
from jax import lax
import numpy as np

D_MODEL = 1024
BATCH = 16
SEQ = 256
DEPTH = 4
DEC_BATCH = 8
DEC_SEQ = 1024
PAST_LEN = 256

GRID_W = 64
EPS = 1e-6
D_FOURIER = 256
N_FOURIER_GROUPS = 4
D_FOURIER_GROUP = D_FOURIER // N_FOURIER_GROUPS
D_CONV = 256
CONV_WIDTH = 31
D_MLSTM = 512
N_HEADS_M = 4
HEAD_DIM_M = D_MLSTM // N_HEADS_M
N_DIRS = 2
CHUNK = 128
N_GATES = 2 * N_DIRS * N_HEADS_M
D_MIX = D_FOURIER + D_CONV + D_MLSTM
D_IN_PROJ = D_FOURIER + 2 * D_CONV + 4 * D_MLSTM + N_GATES
N_EXPERTS = 16
N_EXPERT_GROUPS = 4
EXPERTS_PER_GROUP = N_EXPERTS // N_EXPERT_GROUPS
TOP_K = 2
D_EXPERT = 256
N_MOD = 6

kernel_name = 'hybrid_fnet_conformer_mlstm_moe_diffusion_step'


def rmsnorm(x, g):
    xf = x.astype(jnp.float32)
    y = xf * lax.rsqrt(jnp.mean(xf * xf, axis=-1, keepdims=True) + EPS)
    return (y * g.astype(jnp.float32)).astype(x.dtype)


def grid_pos_embed(n_tokens, dtype):
    rows = n_tokens // GRID_W
    r = jnp.repeat(jnp.arange(rows, dtype=jnp.float32), GRID_W)
    col = jnp.tile(jnp.arange(GRID_W, dtype=jnp.float32), rows)
    quarter = D_MODEL // 4
    omega = 1.0 / (10000.0 ** (jnp.arange(quarter, dtype=jnp.float32) / quarter))
    def enc(p):
        a = p[:, None] * omega[None, :]
        return jnp.concatenate([jnp.sin(a), jnp.cos(a)], axis=-1)
    return jnp.concatenate([enc(r), enc(col)], axis=-1).astype(dtype)


def fourier_mix(xa, w_fnet):
    B, T, _ = xa.shape
    xf = xa.astype(jnp.float32).reshape(B, T, N_FOURIER_GROUPS, D_FOURIER_GROUP)
    y = jnp.fft.fft2(xf, axes=(1, 3), norm='ortho').real.reshape(B, T, D_FOURIER)
    return y.astype(xa.dtype) @ w_fnet


def conv_mix(xb, w_dw, b_dw, ln_g, ln_b, w_pw):
    a, gt = jnp.split(xb, 2, axis=-1)
    u = a * jax.nn.sigmoid(gt)
    pad = CONV_WIDTH // 2
    u = lax.conv_general_dilated(u, w_dw[:, None, :], window_strides=(1,), padding=((pad, pad),),
                                 dimension_numbers=('NWC', 'WIO', 'NWC'),
                                 feature_group_count=D_CONV) + b_dw
    uf = u.astype(jnp.float32)
    mu = jnp.mean(uf, axis=-1, keepdims=True)
    var = jnp.mean(jnp.square(uf - mu), axis=-1, keepdims=True)
    uf = (uf - mu) * lax.rsqrt(var + EPS) * ln_g.astype(jnp.float32) + ln_b.astype(jnp.float32)
    return jax.nn.silu(uf).astype(xb.dtype) @ w_pw


def mlstm_chunk_scan(q, k, v, li, lf, C0, n0, m0):
    B, H, T, d = q.shape
    nc = T // CHUNK
    def to_chunks(z):
        z = z.reshape((B, H, nc, CHUNK) + z.shape[3:])
        return jnp.moveaxis(z, 2, 0)
    causal = jnp.tril(jnp.ones((CHUNK, CHUNK), dtype=bool))
    def body(carry, inp):
        C, n, m = carry
        qc, kc, vc, ic, fc = inp
        b = jnp.cumsum(fc, axis=-1)
        d_log = b[..., :, None] - b[..., None, :] + ic[..., None, :]
        d_log = jnp.where(causal, d_log, -jnp.inf)
        inter_log = b + m[..., None]
        m_t = jnp.maximum(inter_log, jnp.max(d_log, axis=-1))
        dmat = jnp.exp(d_log - m_t[..., None])
        inter = jnp.exp(inter_log - m_t)
        s = jnp.einsum('bhtd,bhsd->bhts', qc, kc) * dmat
        num = inter[..., None] * jnp.einsum('bhtd,bhde->bhte', qc, C) + jnp.einsum('bhts,bhse->bhte', s, vc)
        den = inter * jnp.einsum('bhtd,bhd->bht', qc, n) + jnp.sum(s, axis=-1)
        h = num / jnp.maximum(jnp.abs(den), jnp.exp(-m_t))[..., None]
        m_new = m_t[..., -1]
        w = jnp.exp(b[..., -1:] - b + ic - m_new[..., None])
        decay = jnp.exp(b[..., -1] + m - m_new)
        kw = kc * w[..., None]
        C_new = decay[..., None, None] * C + jnp.einsum('bhsd,bhse->bhde', kw, vc)
        n_new = decay[..., None] * n + jnp.sum(kw, axis=2)
        return (C_new, n_new, m_new), h
    (C, n, m), hs = lax.scan(body, (C0, n0, m0),
                             (to_chunks(q), to_chunks(k), to_chunks(v), to_chunks(li), to_chunks(lf)))
    h = jnp.moveaxis(hs, 0, 2).reshape(B, H, T, d)
    return h, C, n, m


def mlstm_mix(xc, b_gate, g_mh, C0, n0, m0):
    B, T, _ = xc.shape
    xf = xc.astype(jnp.float32)
    q, k, v, o, g = jnp.split(xf, [D_MLSTM, 2 * D_MLSTM, 3 * D_MLSTM, 4 * D_MLSTM], axis=-1)
    def heads(z):
        return z.reshape(B, T, N_HEADS_M, HEAD_DIM_M).transpose(0, 2, 1, 3)
    q, k, v = heads(q), heads(k) * (HEAD_DIM_M ** -0.5), heads(v)
    g = (g + b_gate.astype(jnp.float32)).reshape(B, T, N_DIRS, 2, N_HEADS_M).transpose(2, 3, 0, 4, 1)
    li = g[:, 0]
    lf = jax.nn.log_sigmoid(g[:, 1])
    h_f, Cf, nf, mf = mlstm_chunk_scan(q, k, v, li[0], lf[0], C0[:, 0], n0[:, 0], m0[:, 0])
    def flip(z):
        return jnp.flip(z, axis=2)
    h_b, Cb, nb, mb = mlstm_chunk_scan(flip(q), flip(k), flip(v), flip(li[1]), flip(lf[1]),
                                       C0[:, 1], n0[:, 1], m0[:, 1])
    h = h_f + flip(h_b)
    h = h * lax.rsqrt(jnp.mean(h * h, axis=-1, keepdims=True) + EPS) * g_mh.astype(jnp.float32).reshape(N_HEADS_M, 1, HEAD_DIM_M)
    h = h.transpose(0, 2, 1, 3).reshape(B, T, D_MLSTM) * jax.nn.sigmoid(o)
    C_out = jnp.stack([Cf, Cb], axis=1)
    n_out = jnp.stack([nf, nb], axis=1)
    m_out = jnp.stack([mf, mb], axis=1)
    return h.astype(xc.dtype), C_out, n_out, m_out


def moe(h, w_router, b_router, w_eg, w_eu, w_ed):
    B, T, _ = h.shape
    logits = jnp.einsum('btd,de->bte', h.astype(jnp.float32), w_router.astype(jnp.float32))
    scores = jax.nn.sigmoid(logits)
    sel = scores + b_router.astype(jnp.float32)
    grp = sel.reshape(B, T, N_EXPERT_GROUPS, EXPERTS_PER_GROUP)
    grp_score = jnp.sum(lax.top_k(grp, TOP_K)[0], axis=-1)
    best = jnp.argmax(grp_score, axis=-1)
    in_grp = best[..., None] == (jnp.arange(N_EXPERTS) // EXPERTS_PER_GROUP)
    masked = jnp.where(in_grp, sel, -jnp.inf)
    _, idx = lax.top_k(masked, TOP_K)
    w_sel = jnp.take_along_axis(scores, idx, axis=-1)
    w_sel = w_sel / jnp.sum(w_sel, axis=-1, keepdims=True)
    combine = jnp.sum(jax.nn.one_hot(idx, N_EXPERTS, dtype=jnp.float32) * w_sel[..., None], axis=-2)
    gate = jnp.einsum('btd,edf->btef', h, w_eg)
    up = jnp.einsum('btd,edf->btef', h, w_eu)
    a = jax.nn.silu(gate) * up * combine.astype(h.dtype)[..., None]
    return jnp.einsum('btef,efd->btd', a, w_ed)


def trunk_layer(x, cvec, C0, n0, m0, lp):
    (w_ada, b_ada, g1, g2, w_in, w_fnet, w_dw, b_dw, ln_g, ln_b, w_pw,
     b_gate, g_mh, w_out, w_router, b_router, w_eg, w_eu, w_ed) = lp
    mod = (jax.nn.silu(cvec) @ w_ada + b_ada)[..., None, :]
    sh1, sc1, ga1, sh2, sc2, ga2 = jnp.split(mod, N_MOD, axis=-1)
    h = rmsnorm(x, g1) * (1.0 + sc1) + sh1
    p = h @ w_in
    xa, xb, xc = jnp.split(p, [D_FOURIER, D_FOURIER + 2 * D_CONV], axis=-1)
    out_c, C_out, n_out, m_out = mlstm_mix(xc, b_gate, g_mh, C0, n0, m0)
    mix = jnp.concatenate([fourier_mix(xa, w_fnet), conv_mix(xb, w_dw, b_dw, ln_g, ln_b, w_pw), out_c], axis=-1)
    x = x + ga1 * (mix @ w_out)
    h = rmsnorm(x, g2) * (1.0 + sc2) + sh2
    x = x + ga2 * moe(h, w_router, b_router, w_eg, w_eu, w_ed)
    return x, C_out, n_out, m_out


def setup_inputs(seed: int = 0) -> dict:
    key = jax.random.key(seed)
    ks = jax.random.split(key, 32)
    f32 = jnp.float32
    def nrm(k, shape, scale):
        return jax.random.normal(k, shape, f32) * scale
    bi = nrm(ks[20], (DEPTH, N_DIRS, 1, N_HEADS_M), 0.1)
    bf = 3.0 + nrm(ks[21], (DEPTH, N_DIRS, 1, N_HEADS_M), 0.5)
    b_gate = jnp.concatenate([bi, bf], axis=2).reshape(DEPTH, N_GATES)
    return {
        'x_prompt': nrm(ks[0], (BATCH, SEQ, D_MODEL), 1.0),
        'x_sample': nrm(ks[1], (DEC_BATCH, DEC_SEQ, D_MODEL), 1.0),
        'state_C': nrm(ks[2], (DEC_BATCH, DEPTH, N_DIRS, N_HEADS_M, HEAD_DIM_M, HEAD_DIM_M), 0.1),
        'state_n': nrm(ks[3], (DEC_BATCH, DEPTH, N_DIRS, N_HEADS_M, HEAD_DIM_M), 0.5),
        'state_m': nrm(ks[4], (DEC_BATCH, DEPTH, N_DIRS, N_HEADS_M), 0.5),
        'c': nrm(ks[5], (DEC_BATCH, D_MODEL), 1.0),
        'c_ctx': nrm(ks[6], (D_MODEL,), 1.0),
        'w_ada': nrm(ks[7], (DEPTH, D_MODEL, N_MOD * D_MODEL), 0.5 * D_MODEL ** -0.5),
        'b_ada': nrm(ks[8], (DEPTH, N_MOD * D_MODEL), 0.02),
        'norm1_g': 1.0 + nrm(ks[9], (DEPTH, D_MODEL), 0.02),
        'norm2_g': 1.0 + nrm(ks[10], (DEPTH, D_MODEL), 0.02),
        'w_in': nrm(ks[11], (DEPTH, D_MODEL, D_IN_PROJ), D_MODEL ** -0.5),
        'w_fnet': nrm(ks[12], (DEPTH, D_FOURIER, D_FOURIER), D_FOURIER ** -0.5),
        'w_dw': nrm(ks[13], (DEPTH, CONV_WIDTH, D_CONV), CONV_WIDTH ** -0.5),
        'b_dw': nrm(ks[14], (DEPTH, D_CONV), 0.02),
        'conv_ln_g': 1.0 + nrm(ks[15], (DEPTH, D_CONV), 0.02),
        'conv_ln_b': nrm(ks[16], (DEPTH, D_CONV), 0.02),
        'w_pw': nrm(ks[17], (DEPTH, D_CONV, D_CONV), D_CONV ** -0.5),
        'b_gate': b_gate,
        'g_mh': 1.0 + nrm(ks[18], (DEPTH, D_MLSTM), 0.02),
        'w_out': nrm(ks[19], (DEPTH, D_MIX, D_MODEL), D_MIX ** -0.5),
        'w_router': nrm(ks[22], (D_MODEL, N_EXPERTS), D_MODEL ** -0.5),
        'b_router': nrm(ks[23], (N_EXPERTS,), 0.01),
        'w_exp_gate': nrm(ks[24], (DEPTH, N_EXPERTS, D_MODEL, D_EXPERT), D_MODEL ** -0.5),
        'w_exp_up': nrm(ks[25], (DEPTH, N_EXPERTS, D_MODEL, D_EXPERT), D_MODEL ** -0.5),
        'w_exp_down': nrm(ks[26], (DEPTH, N_EXPERTS, D_EXPERT, D_MODEL), D_EXPERT ** -0.5),
        'final_g': 1.0 + nrm(ks[27], (D_MODEL,), 0.02),
    }


def reference(x_prompt, x_sample, state_C, state_n, state_m, c, c_ctx, w_ada, b_ada, norm1_g, norm2_g,
              w_in, w_fnet, w_dw, b_dw, conv_ln_g, conv_ln_b, w_pw, b_gate, g_mh, w_out,
              w_router, b_router, w_exp_gate, w_exp_up, w_exp_down, final_g):
    f32 = jnp.float32
    def layer_params(l):
        return (w_ada[l], b_ada[l], norm1_g[l], norm2_g[l], w_in[l], w_fnet[l], w_dw[l], b_dw[l],
                conv_ln_g[l], conv_ln_b[l], w_pw[l], b_gate[l], g_mh[l], w_out[l],
                w_router, b_router, w_exp_gate[l], w_exp_up[l], w_exp_down[l])

    bp = x_prompt.shape[0]
    C0 = jnp.zeros((bp, N_DIRS, N_HEADS_M, HEAD_DIM_M, HEAD_DIM_M), f32)
    n0 = jnp.zeros((bp, N_DIRS, N_HEADS_M, HEAD_DIM_M), f32)
    m0 = jnp.zeros((bp, N_DIRS, N_HEADS_M), f32)
    xp = x_prompt
    Cs, ns, ms = [], [], []
    for l in range(DEPTH):
        xp, Cl, nl, ml = trunk_layer(xp, c_ctx, C0, n0, m0, layer_params(l))
        Cs.append(Cl)
        ns.append(nl)
        ms.append(ml)
    y_prompt = rmsnorm(xp, final_g)
    new_state_C = jnp.stack(Cs, axis=1).astype(x_prompt.dtype)
    new_state_n = jnp.stack(ns, axis=1).astype(x_prompt.dtype)
    new_state_m = jnp.stack(ms, axis=1).astype(x_prompt.dtype)

    n_tok = x_sample.shape[1]
    xs = x_sample + grid_pos_embed(n_tok, x_sample.dtype)[None]
    for l in range(DEPTH):
        xs, _, _, _ = trunk_layer(xs, c, state_C[:, l].astype(f32), state_n[:, l].astype(f32),
                                  state_m[:, l].astype(f32), layer_params(l))
    y_sample = rmsnorm(xs, final_g)
    return (y_prompt, y_sample, new_state_C, new_state_n, new_state_m)
```

```python
import functools
import math

import numpy as np
import jax
import jax.numpy as jnp
from jax import lax
from jax.experimental import pallas as pl
from jax.experimental.pallas import tpu as pltpu

D_MODEL = 1024
DEPTH = 4
GRID_W = 64
EPS = 1e-6
D_FOURIER = 256
N_FOURIER_GROUPS = 4
D_FOURIER_GROUP = D_FOURIER // N_FOURIER_GROUPS
D_CONV = 256
CONV_WIDTH = 31
CONV_PAD = CONV_WIDTH // 2
D_MLSTM = 512
N_HEADS_M = 4
HEAD_DIM_M = D_MLSTM // N_HEADS_M
N_DIRS = 2
CHUNK = 128
N_GATES = 2 * N_DIRS * N_HEADS_M
N_UNITS = N_DIRS * N_HEADS_M
D_MIX = D_FOURIER + D_CONV + D_MLSTM
N_EXPERTS = 16
N_EXPERT_GROUPS = 4
EXPERTS_PER_GROUP = N_EXPERTS // N_EXPERT_GROUPS
D_EXPERT = 256
N_MOD = 6

ROWS = 1024
CONV_ROW_TILE = 64
PAD_LO = 16
MOD_ROWS = 16
VMEM_LIMIT = 56 * 1024 * 1024

F32 = jnp.float32
BF16 = jnp.bfloat16


def _dot(a, b):
    return jnp.dot(a, b, preferred_element_type=F32)


def _dot_nt(a, b):
    return lax.dot_general(a, b, (((1,), (1,)), ((), ())), preferred_element_type=F32)


def _dot_tn(a, b):
    return lax.dot_general(a, b, (((0,), (0,)), ((), ())), preferred_element_type=F32)


def _sigmoid(x):
    return 1.0 / (1.0 + jnp.exp(-x))


def _log_sigmoid(x):
    return jnp.minimum(x, 0.0) - jnp.log(1.0 + jnp.exp(-jnp.abs(x)))


def _split_dot(a_f32, b_bf16, nt=False):
    hi = a_f32.astype(BF16)
    lo = (a_f32 - hi.astype(F32)).astype(BF16)
    return _dot(hi, b_bf16) + _dot(lo, b_bf16)


def _split_dot_left(b_bf16, a_f32):
    hi = a_f32.astype(BF16)
    lo = (a_f32 - hi.astype(F32)).astype(BF16)
    return _dot(b_bf16, hi) + _dot(b_bf16, lo)


def _ada_body(c_ref, w_ref, b_ref, o_ref):
    cv = c_ref[...]
    s = cv * _sigmoid(cv)
    o_ref[0] = jnp.dot(s, w_ref[0], preferred_element_type=F32,
                       precision=lax.Precision.HIGHEST) + b_ref[0]


def _ada_call(cv, w_ada, b_ada):
    n_col = N_MOD * D_MODEL
    tn = D_MODEL
    return pl.pallas_call(
        _ada_body,
        grid=(DEPTH, n_col // tn),
        in_specs=[
            pl.BlockSpec((MOD_ROWS, D_MODEL), lambda l, j: (0, 0)),
            pl.BlockSpec((1, D_MODEL, tn), lambda l, j: (l, 0, j)),
            pl.BlockSpec((1, 1, tn), lambda l, j: (l, 0, j)),
        ],
        out_specs=pl.BlockSpec((1, MOD_ROWS, tn), lambda l, j: (l, 0, j)),
        out_shape=jax.ShapeDtypeStruct((DEPTH, MOD_ROWS, n_col), F32),
        compiler_params=pltpu.CompilerParams(dimension_semantics=("arbitrary", "arbitrary")),
        name="ada",
    )(cv, w_ada, b_ada.reshape(DEPTH, 1, n_col))


def _mixer_body(bb, t, has_init, emit_state, add_pos, *refs):
    rows = bb * t
    nc = t // CHUNK
    n_blk = rows // CHUNK
    refs = list(refs)
    x_ref = refs.pop(0)
    pos_ref = refs.pop(0) if add_pos else None
    (mod_ref, g1_ref, wa_ref, wb_ref, wq_ref, wk_ref, wv_ref, wo_ref, wg_ref, wgt_ref,
     bg_ref, bgt_ref, blk_ref, dc_ref, ds_ref, wfn_ref, wdw_ref, bdw_ref, lng_ref, lnb_ref,
     wpw_ref, gmh_ref) = refs[:22]
    refs = refs[22:]
    if has_init:
        c0_ref, n0_ref, m0_ref = refs[:3]
        refs = refs[3:]
    wout_ref = refs.pop(0)
    x1_ref = refs.pop(0)
    if emit_state:
        co_ref, no_ref, mo_ref = refs[:3]
        refs = refs[3:]
    (xa_s, pad_s, cact_s, q_s, k_s, v_s, so_s, gcol_s, grow_s, hbuf_s, mix_s,
     cst_s, nst_s, mst_s) = refs

    def load_x():
        xv = x_ref[...].reshape(rows, D_MODEL)
        if add_pos:
            xv = xv + pos_ref[...]
        return xv

    x = load_x()
    mod = mod_ref[0]
    sh1 = mod[:, 0:D_MODEL]
    sc1 = mod[:, D_MODEL:2 * D_MODEL]
    ms = jnp.mean(x * x, axis=-1, keepdims=True)
    h = (x * lax.rsqrt(ms + EPS) * g1_ref[...]) * (1.0 + sc1) + sh1
    hb = h.astype(BF16)

    xa_s[...] = _dot(hb, wa_ref[...]).astype(BF16)
    pb = _dot(hb, wb_ref[...])
    u = pb[:, :D_CONV] * _sigmoid(pb[:, D_CONV:])
    zpad = jnp.zeros((PAD_LO, D_CONV), F32)
    for i in range(bb):
        pad_s[i, 0:PAD_LO, :] = zpad
        pad_s[i, PAD_LO:PAD_LO + t, :] = u[i * t:(i + 1) * t]
        pad_s[i, PAD_LO + t:2 * PAD_LO + t, :] = zpad
    q_s[...] = _dot(hb, wq_ref[...]).astype(BF16)
    k_s[...] = (_dot(hb, wk_ref[...]) * (HEAD_DIM_M ** -0.5)).astype(BF16)
    v_s[...] = _dot(hb, wv_ref[...]).astype(BF16)
    so_s[...] = _sigmoid(_dot(hb, wo_ref[...]))

    gcol = _dot(hb, wg_ref[...]) + bg_ref[...]
    kind_c = (lax.broadcasted_iota(jnp.int32, gcol.shape, 1) // N_HEADS_M) % 2
    gcol_s[...] = jnp.where(kind_c == 1, _log_sigmoid(gcol), gcol)
    grow = _dot_nt(wgt_ref[...], hb) + bgt_ref[...]
    kind_r = (lax.broadcasted_iota(jnp.int32, grow.shape, 0) // N_HEADS_M) % 2
    grow = jnp.where(kind_r == 1, _log_sigmoid(grow), grow)
    for b in range(n_blk):
        grow_s[b] = grow[:, b * CHUNK:(b + 1) * CHUNK]

    for i in range(bb):
        xa_i = xa_s[i * t:(i + 1) * t, :]
        uu = _dot(xa_i, blk_ref[...])
        y = (_dot(dc_ref[...], uu[:, :D_FOURIER].astype(BF16))
             + _dot(ds_ref[...], uu[:, D_FOURIER:].astype(BF16)))
        mix_s[i * t:(i + 1) * t, 0:D_FOURIER] = _dot(y.astype(BF16), wfn_ref[...]).astype(BF16)

    wdw = wdw_ref[...]
    for i in range(bb):
        for r0 in range(0, t, CONV_ROW_TILE):
            acc = jnp.broadcast_to(bdw_ref[...], (CONV_ROW_TILE, D_CONV))
            for j in range(CONV_WIDTH):
                start = PAD_LO - CONV_PAD + r0 + j
                acc = acc + pad_s[i, start:start + CONV_ROW_TILE, :] * wdw[j:j + 1, :]
            mu = jnp.mean(acc, axis=-1, keepdims=True)
            cen = acc - mu
            var = jnp.mean(cen * cen, axis=-1, keepdims=True)
            uf = cen * lax.rsqrt(var + EPS) * lng_ref[...] + lnb_ref[...]
            act = uf * _sigmoid(uf)
            cact_s[i * t + r0:i * t + r0 + CONV_ROW_TILE, :] = act.astype(BF16)
    mix_s[:, D_FOURIER:D_FOURIER + D_CONV] = _dot(cact_s[...], wpw_ref[...]).astype(BF16)

    hbuf_s[...] = jnp.zeros(hbuf_s.shape, F32)
    r_i = lax.broadcasted_iota(jnp.int32, (CHUNK, CHUNK), 0)
    c_i = lax.broadcasted_iota(jnp.int32, (CHUNK, CHUNK), 1)
    lower = r_i >= c_i
    upper = r_i <= c_i
    tri_lo = lower.astype(BF16)
    tri_up = upper.astype(BF16)
    lane_i = lax.broadcasted_iota(jnp.int32, (1, 128), 1)
    neg_inf = jnp.float32(-jnp.inf)

    def unit(d, hh, row0, gc, gr, bcol_all, brow_all):
        j = d * N_HEADS_M + hh
        gi = d * 2 * N_HEADS_M + hh
        gf = gi + N_HEADS_M
        cols = slice(hh * HEAD_DIM_M, (hh + 1) * HEAD_DIM_M)
        qc = q_s[pl.ds(row0, CHUNK), cols]
        kc = k_s[pl.ds(row0, CHUNK), cols]
        vc = v_s[pl.ds(row0, CHUNK), cols]
        i_col = gc[:, gi:gi + 1]
        i_row = gr[gi:gi + 1, :]
        b_col = bcol_all[:, gf:gf + 1]
        b_row = brow_all[gf:gf + 1, :]
        mask = lower if d == 0 else upper
        last = CHUNK - 1 if d == 0 else 0
        d_log = jnp.where(mask, b_col - b_row + i_row, neg_inf)
        rowmax = jnp.max(d_log, axis=-1, keepdims=True)
        mrow = mst_s[0:1, :]
        m_prev = jnp.sum(jnp.where(lane_i == j, mrow, 0.0), axis=-1, keepdims=True)
        inter_log = b_col + m_prev
        m_t = jnp.maximum(inter_log, rowmax)
        dmat = jnp.exp(d_log - m_t)
        inter = jnp.exp(inter_log - m_t)
        s = _dot_nt(qc, kc) * dmat
        c_prev = cst_s[j]
        n_prev = nst_s[j:j + 1, :]
        num = inter * _dot(qc, c_prev.astype(BF16)) + _dot(s.astype(BF16), vc)
        den = (inter * jnp.sum(qc.astype(F32) * n_prev, axis=-1, keepdims=True)
               + jnp.sum(s, axis=-1, keepdims=True))
        hval = num / jnp.maximum(jnp.abs(den), jnp.exp(-m_t))
        hbuf_s[pl.ds(row0, CHUNK), cols] = hbuf_s[pl.ds(row0, CHUNK), cols] + hval
        m_new = m_t[last:last + 1, :]
        b_last = b_col[last:last + 1, :]
        w_col = jnp.exp(b_last - b_col + i_col - m_new)
        decay = jnp.exp(b_last + m_prev - m_new)
        kw = kc.astype(F32) * w_col
        cst_s[j] = decay * c_prev + _dot_tn(kw.astype(BF16), vc)
        nst_s[j:j + 1, :] = decay * n_prev + jnp.sum(kw, axis=0, keepdims=True)
        mst_s[0:1, :] = jnp.where(lane_i == j, m_new, mrow)

    def chunk_step(it, carry):
        seq = it // nc
        c = it % nc

        @pl.when(c == 0)
        def _():
            if has_init:
                cst_s[...] = c0_ref[seq]
                nst_s[...] = n0_ref[seq]
                mst_s[0:1, 0:N_UNITS] = m0_ref[seq]
            else:
                cst_s[...] = jnp.zeros(cst_s.shape, F32)
                nst_s[...] = jnp.zeros(nst_s.shape, F32)
                mst_s[...] = jnp.zeros(mst_s.shape, F32)

        for d in range(N_DIRS):
            blk = seq * nc + (c if d == 0 else nc - 1 - c)
            row0 = pl.multiple_of(blk * CHUNK, CHUNK)
            gc = gcol_s[pl.ds(row0, CHUNK), :]
            gr = grow_s[blk]
            if d == 0:
                bcol_all = _split_dot_left(tri_lo, gc)
                brow_all = _split_dot(gr, tri_up)
            else:
                bcol_all = _split_dot_left(tri_up, gc)
                brow_all = _split_dot(gr, tri_lo)
            for hh in range(N_HEADS_M):
                unit(d, hh, row0, gc, gr, bcol_all, brow_all)

        if emit_state:
            @pl.when(c == nc - 1)
            def _():
                co_ref[seq] = cst_s[...]
                no_ref[seq] = nst_s[...]
                mo_ref[seq] = mst_s[0:1, 0:N_UNITS]
        return carry

    lax.fori_loop(0, n_blk, chunk_step, 0)

    for hh in range(N_HEADS_M):
        cols = slice(hh * HEAD_DIM_M, (hh + 1) * HEAD_DIM_M)
        hv = hbuf_s[:, cols]
        r = lax.rsqrt(jnp.mean(hv * hv, axis=-1, keepdims=True) + EPS)
        oc = hv * r * gmh_ref[:, cols] * so_s[:, cols]
        mix_s[:, D_FOURIER + D_CONV + hh * HEAD_DIM_M:D_FOURIER + D_CONV + (hh + 1) * HEAD_DIM_M] = oc.astype(BF16)

    ga1 = mod_ref[0][:, 2 * D_MODEL:3 * D_MODEL]
    res = _dot(mix_s[...], wout_ref[...])
    x1_ref[...] = (load_x() + ga1 * res).reshape(bb, t, D_MODEL)


def _const_spec(shape):
    nd = len(shape)
    return pl.BlockSpec(shape, lambda g, _nd=nd: (0,) * _nd, pipeline_mode=pl.Buffered(1))


def _mixer_call(x, pos, mod, per_batch_mod, lw, state0, emit_state):
    nb, t, _ = x.shape
    bb = ROWS // t
    assert bb * t == ROWS and nb % bb == 0 and t % CHUNK == 0
    rows = ROWS
    n_blk = rows // CHUNK
    has_init = state0 is not None
    add_pos = pos is not None

    args = [x]
    in_specs = [pl.BlockSpec((bb, t, D_MODEL), lambda g: (g, 0, 0))]
    if add_pos:
        args.append(pos)
        in_specs.append(_const_spec(pos.shape))
    args.append(mod)
    if per_batch_mod:
        assert bb == 1
        in_specs.append(pl.BlockSpec((1, 1, N_MOD * D_MODEL), lambda g: (g, 0, 0)))
    else:
        in_specs.append(pl.BlockSpec((1, 1, N_MOD * D_MODEL), lambda g: (0, 0, 0)))
    consts = [lw["g1"], lw["wa"], lw["wb"], lw["wq"], lw["wk"], lw["wv"], lw["wo"], lw["wg"], lw["wgt"],
              lw["bg"], lw["bgt"], lw["blk"], lw["dc"][t], lw["ds"][t], lw["wfn"], lw["wdw"], lw["bdw"],
              lw["lng"], lw["lnb"], lw["wpw"], lw["gmh"]]
    for a in consts:
        args.append(a)
        in_specs.append(_const_spec(a.shape))
    if has_init:
        c0, n0, m0 = state0
        args += [c0, n0, m0]
        in_specs += [
            pl.BlockSpec((bb, N_UNITS, HEAD_DIM_M, HEAD_DIM_M), lambda g: (g, 0, 0, 0)),
            pl.BlockSpec((bb, N_UNITS, HEAD_DIM_M), lambda g: (g, 0, 0)),
            pl.BlockSpec((bb, 1, N_UNITS), lambda g: (g, 0, 0)),
        ]
    args.append(lw["wout"])
    in_specs.append(_const_spec(lw["wout"].shape))

    out_shape = [jax.ShapeDtypeStruct((nb, t, D_MODEL), F32)]
    out_specs = [pl.BlockSpec((bb, t, D_MODEL), lambda g: (g, 0, 0))]
    if emit_state:
        out_shape += [
            jax.ShapeDtypeStruct((nb, N_UNITS, HEAD_DIM_M, HEAD_DIM_M), F32),
            jax.ShapeDtypeStruct((nb, N_UNITS, HEAD_DIM_M), F32),
            jax.ShapeDtypeStruct((nb, 1, N_UNITS), F32),
        ]
        out_specs += [
            pl.BlockSpec((bb, N_UNITS, HEAD_DIM_M, HEAD_DIM_M), lambda g: (g, 0, 0, 0)),
            pl.BlockSpec((bb, N_UNITS, HEAD_DIM_M), lambda g: (g, 0, 0)),
            pl.BlockSpec((bb, 1, N_UNITS), lambda g: (g, 0, 0)),
        ]
    scratch = [
        pltpu.VMEM((rows, D_FOURIER), BF16),
        pltpu.VMEM((bb, t + 2 * PAD_LO, D_CONV), F32),
        pltpu.VMEM((rows, D_CONV), BF16),
        pltpu.VMEM((rows, D_MLSTM), BF16),
        pltpu.VMEM((rows, D_MLSTM), BF16),
        pltpu.VMEM((rows, D_MLSTM), BF16),
        pltpu.VMEM((rows, D_MLSTM), F32),
        pltpu.VMEM((rows, N_GATES), F32),
        pltpu.VMEM((n_blk, N_GATES, CHUNK), F32),
        pltpu.VMEM((rows, D_MLSTM), F32),
        pltpu.VMEM((rows, D_MIX), BF16),
        pltpu.VMEM((N_UNITS, HEAD_DIM_M, HEAD_DIM_M), F32),
        pltpu.VMEM((N_UNITS, HEAD_DIM_M), F32),
        pltpu.VMEM((8, 128), F32),
    ]
    body = functools.partial(_mixer_body, bb, t, has_init, emit_state, add_pos)
    return pl.pallas_call(
        body,
        grid=(nb // bb,),
        in_specs=in_specs,
        out_specs=out_specs,
        out_shape=out_shape,
        scratch_shapes=scratch,
        compiler_params=pltpu.CompilerParams(dimension_semantics=("arbitrary",),
                                             vmem_limit_bytes=VMEM_LIMIT),
        name="mixer_t%d" % t,
    )(*args)


def _rmsnorm_rows(x, g):
    ms = jnp.mean(x * x, axis=-1, keepdims=True)
    return x * lax.rsqrt(ms + EPS) * g


def _route(logits, b_router):
    scores = _sigmoid(logits)
    sel = scores + b_router
    lane = lax.broadcasted_iota(jnp.int32, sel.shape, 1)
    lane_f = lane.astype(F32)
    grp = lane // EXPERTS_PER_GROUP
    neg_inf = jnp.float32(-jnp.inf)
    big = jnp.float32(N_EXPERTS)

    def top2(v):
        m1 = jnp.max(v, axis=-1, keepdims=True)
        i1 = jnp.min(jnp.where(v == m1, lane_f, big), axis=-1, keepdims=True)
        v2 = jnp.where(lane_f == i1, neg_inf, v)
        m2 = jnp.max(v2, axis=-1, keepdims=True)
        i2 = jnp.min(jnp.where(v2 == m2, lane_f, big), axis=-1, keepdims=True)
        return m1, i1, m2, i2

    best = jnp.zeros((sel.shape[0], 1), jnp.int32)
    best_v = None
    for g in range(N_EXPERT_GROUPS):
        m1, _, m2, _ = top2(jnp.where(grp == g, sel, neg_inf))
        gs = m1 + m2
        if g == 0:
            best_v = gs
        else:
            upd = gs > best_v
            best = jnp.where(upd, g, best)
            best_v = jnp.where(upd, gs, best_v)
    _, i1, _, i2 = top2(jnp.where(grp == best, sel, neg_inf))
    hit1 = lane_f == i1
    hit2 = lane_f == i2
    w1 = jnp.sum(jnp.where(hit1, scores, 0.0), axis=-1, keepdims=True)
    w2 = jnp.sum(jnp.where(hit2, scores, 0.0), axis=-1, keepdims=True)
    tot = w1 + w2
    return jnp.where(hit1, w1 / tot, 0.0) + jnp.where(hit2, w2 / tot, 0.0)


def _moe_body(is_last, x_ref, mod_ref, g2_ref, wr_ref, br_ref, wg_ref, wu_ref, wd_ref, fg_ref,
              o_ref, h_s, comb_s, acc_s):
    e = pl.program_id(1)

    @pl.when(e == 0)
    def _():
        mod = mod_ref[0]
        sh2 = mod[:, 3 * D_MODEL:4 * D_MODEL]
        sc2 = mod[:, 4 * D_MODEL:5 * D_MODEL]
        h = _rmsnorm_rows(x_ref[...], g2_ref[...]) * (1.0 + sc2) + sh2
        h_s[...] = h.astype(BF16)
        logits = jnp.dot(h, wr_ref[...], preferred_element_type=F32, precision=lax.Precision.HIGHEST)
        comb_s[...] = _route(logits, br_ref[...])
        acc_s[...] = jnp.zeros(acc_s.shape, F32)

    hb = h_s[...]
    gate = _dot(hb, wg_ref[0])
    up = _dot(hb, wu_ref[0])
    comb = comb_s[...]
    lane = lax.broadcasted_iota(jnp.int32, comb.shape, 1)
    ce = jnp.sum(jnp.where(lane == e, comb, 0.0), axis=-1, keepdims=True)
    a = gate * _sigmoid(gate) * up * ce
    acc_s[...] = acc_s[...] + _dot(a.astype(BF16), wd_ref[0])

    @pl.when(e == N_EXPERTS - 1)
    def _():
        ga2 = mod_ref[0][:, 5 * D_MODEL:6 * D_MODEL]
        xo = x_ref[...] + ga2 * acc_s[...]
        if is_last:
            xo = _rmsnorm_rows(xo, fg_ref[...])
        o_ref[...] = xo


def _moe_call(x2d, mod, per_tile_mod, lw, is_last):
    n_tok = x2d.shape[0]
    tm = ROWS
    assert n_tok % tm == 0
    if per_tile_mod:
        mod_spec = pl.BlockSpec((1, 1, N_MOD * D_MODEL), lambda i, e: (i, 0, 0))
    else:
        mod_spec = pl.BlockSpec((1, 1, N_MOD * D_MODEL), lambda i, e: (0, 0, 0))
    return pl.pallas_call(
        functools.partial(_moe_body, is_last),
        grid=(n_tok // tm, N_EXPERTS),
        in_specs=[
            pl.BlockSpec((tm, D_MODEL), lambda i, e: (i, 0)),
            mod_spec,
            pl.BlockSpec((1, D_MODEL), lambda i, e: (0, 0)),
            pl.BlockSpec((D_MODEL, N_EXPERTS), lambda i, e: (0, 0)),
            pl.BlockSpec((1, N_EXPERTS), lambda i, e: (0, 0)),
            pl.BlockSpec((1, D_MODEL, D_EXPERT), lambda i, e: (e, 0, 0)),
            pl.BlockSpec((1, D_MODEL, D_EXPERT), lambda i, e: (e, 0, 0)),
            pl.BlockSpec((1, D_EXPERT, D_MODEL), lambda i, e: (e, 0, 0)),
            pl.BlockSpec((1, D_MODEL), lambda i, e: (0, 0)),
        ],
        out_specs=pl.BlockSpec((tm, D_MODEL), lambda i, e: (i, 0)),
        out_shape=jax.ShapeDtypeStruct((n_tok, D_MODEL), F32),
        scratch_shapes=[
            pltpu.VMEM((tm, D_MODEL), BF16),
            pltpu.VMEM((tm, N_EXPERTS), F32),
            pltpu.VMEM((tm, D_MODEL), F32),
        ],
        compiler_params=pltpu.CompilerParams(dimension_semantics=("arbitrary", "arbitrary"),
                                             vmem_limit_bytes=VMEM_LIMIT),
        name="moe",
    )(x2d, mod, lw["g2"], lw["wr"], lw["br"], lw["weg"], lw["weu"], lw["wed"], lw["fg"])


def _dft_tables(t):
    idx = np.arange(t, dtype=np.int64)
    ang = 2.0 * np.pi * ((idx[:, None] * idx[None, :]) % t).astype(np.float64) / t
    scale = 1.0 / math.sqrt(t * D_FOURIER_GROUP)
    return (np.cos(ang) * scale).astype(np.float32), (-np.sin(ang) * scale).astype(np.float32)


def _group_tables():
    idx = np.arange(D_FOURIER_GROUP, dtype=np.int64)
    ang = 2.0 * np.pi * ((idx[:, None] * idx[None, :]) % D_FOURIER_GROUP).astype(np.float64) / D_FOURIER_GROUP
    eye = np.eye(N_FOURIER_GROUPS)
    return np.concatenate([np.kron(eye, np.cos(ang)), np.kron(eye, np.sin(ang))], axis=1).astype(np.float32)


def _grid_pos_embed(n_tokens, dtype):
    rows = n_tokens // GRID_W
    r = jnp.repeat(jnp.arange(rows, dtype=jnp.float32), GRID_W)
    col = jnp.tile(jnp.arange(GRID_W, dtype=jnp.float32), rows)
    quarter = D_MODEL // 4
    omega = 1.0 / (10000.0 ** (jnp.arange(quarter, dtype=jnp.float32) / quarter))

    def enc(p):
        a = p[:, None] * omega[None, :]
        return jnp.concatenate([jnp.sin(a), jnp.cos(a)], axis=-1)

    return jnp.concatenate([enc(r), enc(col)], axis=-1).astype(dtype)


def kernel(x_prompt, x_sample, state_C, state_n, state_m, c, c_ctx, w_ada, b_ada, norm1_g, norm2_g, w_in, w_fnet, w_dw, b_dw, conv_ln_g, conv_ln_b, w_pw, b_gate, g_mh, w_out, w_router, b_router, w_exp_gate, w_exp_up, w_exp_down, final_g):
    bp, tp, _ = x_prompt.shape
    bs, ts, _ = x_sample.shape

    cv = jnp.zeros((MOD_ROWS, D_MODEL), F32).at[:bs].set(c).at[bs].set(c_ctx)
    mod_all = _ada_call(cv, w_ada, b_ada)

    dft = {t: _dft_tables(t) for t in sorted({tp, ts})}
    dc = {t: jnp.asarray(v[0]).astype(BF16) for t, v in dft.items()}
    ds = {t: jnp.asarray(v[1]).astype(BF16) for t, v in dft.items()}
    blk = jnp.asarray(_group_tables()).astype(BF16)
    pos = _grid_pos_embed(ts, x_sample.dtype)

    o_a = D_FOURIER
    o_b = o_a + 2 * D_CONV
    w_in_b = w_in.astype(BF16)
    w_out_b = w_out.astype(BF16)
    weg = w_exp_gate.astype(BF16)
    weu = w_exp_up.astype(BF16)
    wed = w_exp_down.astype(BF16)

    xp, xs = x_prompt, x_sample
    cs, ns, ms = [], [], []
    for l in range(DEPTH):
        wl = w_in_b[l]
        wg = wl[:, o_b + 4 * D_MLSTM:]
        lw = dict(
            g1=norm1_g[l].reshape(1, D_MODEL), g2=norm2_g[l].reshape(1, D_MODEL),
            wa=wl[:, :o_a], wb=wl[:, o_a:o_b],
            wq=wl[:, o_b:o_b + D_MLSTM], wk=wl[:, o_b + D_MLSTM:o_b + 2 * D_MLSTM],
            wv=wl[:, o_b + 2 * D_MLSTM:o_b + 3 * D_MLSTM], wo=wl[:, o_b + 3 * D_MLSTM:o_b + 4 * D_MLSTM],
            wg=wg, wgt=wg.T,
            bg=b_gate[l].reshape(1, N_GATES), bgt=b_gate[l].reshape(N_GATES, 1),
            blk=blk, dc=dc, ds=ds, wfn=w_fnet[l].astype(BF16),
            wdw=jnp.concatenate([w_dw[l], jnp.zeros((1, D_CONV), F32)], axis=0),
            bdw=b_dw[l].reshape(1, D_CONV), lng=conv_ln_g[l].reshape(1, D_CONV),
            lnb=conv_ln_b[l].reshape(1, D_CONV), wpw=w_pw[l].astype(BF16),
            gmh=g_mh[l].reshape(1, D_MLSTM), wout=w_out_b[l],
            wr=w_router, br=b_router.reshape(1, N_EXPERTS),
            weg=weg[l], weu=weu[l], wed=wed[l], fg=final_g.reshape(1, D_MODEL),
        )
        mod_s = mod_all[l, :bs].reshape(bs, 1, N_MOD * D_MODEL)
        mod_p = mod_all[l, bs:bs + 1].reshape(1, 1, N_MOD * D_MODEL)
        is_last = l == DEPTH - 1

        xp, c_l, n_l, m_l = _mixer_call(xp, None, mod_p, False, lw, None, True)
        cs.append(c_l)
        ns.append(n_l)
        ms.append(m_l)
        xp = _moe_call(xp.reshape(bp * tp, D_MODEL), mod_p, False, lw, is_last).reshape(bp, tp, D_MODEL)

        state0 = (state_C[:, l].reshape(bs, N_UNITS, HEAD_DIM_M, HEAD_DIM_M),
                  state_n[:, l].reshape(bs, N_UNITS, HEAD_DIM_M),
                  state_m[:, l].reshape(bs, 1, N_UNITS))
        (xs,) = _mixer_call(xs, pos if l == 0 else None, mod_s, True, lw, state0, False)
        xs = _moe_call(xs.reshape(bs * ts, D_MODEL), mod_s, True, lw, is_last).reshape(bs, ts, D_MODEL)

    new_c = jnp.stack(cs, axis=1).reshape(bp, DEPTH, N_DIRS, N_HEADS_M, HEAD_DIM_M, HEAD_DIM_M)
    new_n = jnp.stack(ns, axis=1).reshape(bp, DEPTH, N_DIRS, N_HEADS_M, HEAD_DIM_M)
    new_m = jnp.stack(ms, axis=1).reshape(bp, DEPTH, N_DIRS, N_HEADS_M)
    return (xp, xs, new_c, new_n, new_m)
```

```python
import functools
import math

import numpy as np
import jax
import jax.numpy as jnp
from jax import lax
from jax.experimental import pallas as pl
from jax.experimental.pallas import tpu as pltpu

D_MODEL = 1024
DEPTH = 4
GRID_W = 64
EPS = 1e-6
D_FOURIER = 256
N_FOURIER_GROUPS = 4
D_FOURIER_GROUP = D_FOURIER // N_FOURIER_GROUPS
D_CONV = 256
CONV_WIDTH = 31
CONV_PAD = CONV_WIDTH // 2
D_MLSTM = 512
N_HEADS_M = 4
HEAD_DIM_M = D_MLSTM // N_HEADS_M
N_DIRS = 2
CHUNK = 128
N_GATES = 2 * N_DIRS * N_HEADS_M
N_UNITS = N_DIRS * N_HEADS_M
D_MIX = D_FOURIER + D_CONV + D_MLSTM
N_EXPERTS = 16
N_EXPERT_GROUPS = 4
EXPERTS_PER_GROUP = N_EXPERTS // N_EXPERT_GROUPS
D_EXPERT = 256
N_MOD = 6

ROWS = 1024
CONV_ROW_TILE = 64
PAD_LO = 16
MOD_ROWS = 16
MOE_BLK = 128
MOE_CAP = ROWS + N_EXPERT_GROUPS * MOE_BLK
MOE_EXT = 128
VMEM_LIMIT = 56 * 1024 * 1024

F32 = jnp.float32
BF16 = jnp.bfloat16


def _dot(a, b):
    return jnp.dot(a, b, preferred_element_type=F32)


def _dot_nt(a, b):
    return lax.dot_general(a, b, (((1,), (1,)), ((), ())), preferred_element_type=F32)


def _dot_tn(a, b):
    return lax.dot_general(a, b, (((0,), (0,)), ((), ())), preferred_element_type=F32)


def _sigmoid(x):
    return 1.0 / (1.0 + jnp.exp(-x))


def _log_sigmoid(x):
    return jnp.minimum(x, 0.0) - jnp.log(1.0 + jnp.exp(-jnp.abs(x)))


def _split_dot(a_f32, b_bf16, nt=False):
    hi = a_f32.astype(BF16)
    lo = (a_f32 - hi.astype(F32)).astype(BF16)
    return _dot(hi, b_bf16) + _dot(lo, b_bf16)


def _split_dot_left(b_bf16, a_f32):
    hi = a_f32.astype(BF16)
    lo = (a_f32 - hi.astype(F32)).astype(BF16)
    return _dot(b_bf16, hi) + _dot(b_bf16, lo)


def _ada_body(c_ref, w_ref, b_ref, o_ref):
    cv = c_ref[...]
    s = cv * _sigmoid(cv)
    o_ref[0] = jnp.dot(s, w_ref[0], preferred_element_type=F32,
                       precision=lax.Precision.HIGHEST) + b_ref[0]


def _ada_call(cv, w_ada, b_ada):
    n_col = N_MOD * D_MODEL
    tn = D_MODEL
    return pl.pallas_call(
        _ada_body,
        grid=(DEPTH, n_col // tn),
        in_specs=[
            pl.BlockSpec((MOD_ROWS, D_MODEL), lambda l, j: (0, 0)),
            pl.BlockSpec((1, D_MODEL, tn), lambda l, j: (l, 0, j)),
            pl.BlockSpec((1, 1, tn), lambda l, j: (l, 0, j)),
        ],
        out_specs=pl.BlockSpec((1, MOD_ROWS, tn), lambda l, j: (l, 0, j)),
        out_shape=jax.ShapeDtypeStruct((DEPTH, MOD_ROWS, n_col), F32),
        compiler_params=pltpu.CompilerParams(dimension_semantics=("arbitrary", "arbitrary")),
        name="ada",
    )(cv, w_ada, b_ada.reshape(DEPTH, 1, n_col))


def _mixer_body(bb, t, has_init, emit_state, add_pos, *refs):
    rows = bb * t
    nc = t // CHUNK
    n_blk = rows // CHUNK
    refs = list(refs)
    x_ref = refs.pop(0)
    pos_ref = refs.pop(0) if add_pos else None
    (mod_ref, g1_ref, wa_ref, wb_ref, wq_ref, wk_ref, wv_ref, wo_ref, wg_ref, wgt_ref,
     bg_ref, bgt_ref, blk_ref, dc_ref, ds_ref, wfn_ref, wdw_ref, bdw_ref, lng_ref, lnb_ref,
     wpw_ref, gmh_ref) = refs[:22]
    refs = refs[22:]
    if has_init:
        c0_ref, n0_ref, m0_ref = refs[:3]
        refs = refs[3:]
    wout_ref = refs.pop(0)
    x1_ref = refs.pop(0)
    if emit_state:
        co_ref, no_ref, mo_ref = refs[:3]
        refs = refs[3:]
    (xa_s, pad_s, cact_s, q_s, k_s, v_s, so_s, gcol_s, grow_s, hbuf_s, mix_s,
     cst_s, nst_s, mst_s) = refs

    def load_x():
        xv = x_ref[...].reshape(rows, D_MODEL)
        if add_pos:
            xv = xv + pos_ref[...]
        return xv

    x = load_x()
    mod = mod_ref[0]
    sh1 = mod[:, 0:D_MODEL]
    sc1 = mod[:, D_MODEL:2 * D_MODEL]
    ms = jnp.mean(x * x, axis=-1, keepdims=True)
    h = (x * lax.rsqrt(ms + EPS) * g1_ref[...]) * (1.0 + sc1) + sh1
    hb = h.astype(BF16)

    xa_s[...] = _dot(hb, wa_ref[...]).astype(BF16)
    pb = _dot(hb, wb_ref[...])
    u = pb[:, :D_CONV] * _sigmoid(pb[:, D_CONV:])
    zpad = jnp.zeros((PAD_LO, D_CONV), F32)
    for i in range(bb):
        pad_s[i, 0:PAD_LO, :] = zpad
        pad_s[i, PAD_LO:PAD_LO + t, :] = u[i * t:(i + 1) * t]
        pad_s[i, PAD_LO + t:2 * PAD_LO + t, :] = zpad
    q_s[...] = _dot(hb, wq_ref[...]).astype(BF16)
    k_s[...] = (_dot(hb, wk_ref[...]) * (HEAD_DIM_M ** -0.5)).astype(BF16)
    v_s[...] = _dot(hb, wv_ref[...]).astype(BF16)
    so_s[...] = _sigmoid(_dot(hb, wo_ref[...]))

    gcol = _dot(hb, wg_ref[...]) + bg_ref[...]
    kind_c = (lax.broadcasted_iota(jnp.int32, gcol.shape, 1) // N_HEADS_M) % 2
    gcol_s[...] = jnp.where(kind_c == 1, _log_sigmoid(gcol), gcol)
    grow = _dot_nt(wgt_ref[...], hb) + bgt_ref[...]
    kind_r = (lax.broadcasted_iota(jnp.int32, grow.shape, 0) // N_HEADS_M) % 2
    grow = jnp.where(kind_r == 1, _log_sigmoid(grow), grow)
    for b in range(n_blk):
        grow_s[b] = grow[:, b * CHUNK:(b + 1) * CHUNK]

    for i in range(bb):
        xa_i = xa_s[i * t:(i + 1) * t, :]
        uu = _dot(xa_i, blk_ref[...])
        y = (_dot(dc_ref[...], uu[:, :D_FOURIER].astype(BF16))
             + _dot(ds_ref[...], uu[:, D_FOURIER:].astype(BF16)))
        mix_s[i * t:(i + 1) * t, 0:D_FOURIER] = _dot(y.astype(BF16), wfn_ref[...]).astype(BF16)

    wdw = wdw_ref[...]
    for i in range(bb):
        for r0 in range(0, t, CONV_ROW_TILE):
            acc = jnp.broadcast_to(bdw_ref[...], (CONV_ROW_TILE, D_CONV))
            for j in range(CONV_WIDTH):
                start = PAD_LO - CONV_PAD + r0 + j
                acc = acc + pad_s[i, start:start + CONV_ROW_TILE, :] * wdw[j:j + 1, :]
            mu = jnp.mean(acc, axis=-1, keepdims=True)
            cen = acc - mu
            var = jnp.mean(cen * cen, axis=-1, keepdims=True)
            uf = cen * lax.rsqrt(var + EPS) * lng_ref[...] + lnb_ref[...]
            act = uf * _sigmoid(uf)
            cact_s[i * t + r0:i * t + r0 + CONV_ROW_TILE, :] = act.astype(BF16)
    mix_s[:, D_FOURIER:D_FOURIER + D_CONV] = _dot(cact_s[...], wpw_ref[...]).astype(BF16)

    hbuf_s[...] = jnp.zeros(hbuf_s.shape, F32)
    r_i = lax.broadcasted_iota(jnp.int32, (CHUNK, CHUNK), 0)
    c_i = lax.broadcasted_iota(jnp.int32, (CHUNK, CHUNK), 1)
    lower = r_i >= c_i
    upper = r_i <= c_i
    tri_lo = lower.astype(BF16)
    tri_up = upper.astype(BF16)
    lane_i = lax.broadcasted_iota(jnp.int32, (1, 128), 1)
    neg_inf = jnp.float32(-jnp.inf)

    def unit(d, hh, row0, gc, gr, bcol_all, brow_all):
        j = d * N_HEADS_M + hh
        gi = d * 2 * N_HEADS_M + hh
        gf = gi + N_HEADS_M
        cols = slice(hh * HEAD_DIM_M, (hh + 1) * HEAD_DIM_M)
        qc = q_s[pl.ds(row0, CHUNK), cols]
        kc = k_s[pl.ds(row0, CHUNK), cols]
        vc = v_s[pl.ds(row0, CHUNK), cols]
        i_col = gc[:, gi:gi + 1]
        i_row = gr[gi:gi + 1, :]
        b_col = bcol_all[:, gf:gf + 1]
        b_row = brow_all[gf:gf + 1, :]
        mask = lower if d == 0 else upper
        last = CHUNK - 1 if d == 0 else 0
        d_log = jnp.where(mask, b_col - b_row + i_row, neg_inf)
        rowmax = jnp.max(d_log, axis=-1, keepdims=True)
        mrow = mst_s[0:1, :]
        m_prev = jnp.sum(jnp.where(lane_i == j, mrow, 0.0), axis=-1, keepdims=True)
        inter_log = b_col + m_prev
        m_t = jnp.maximum(inter_log, rowmax)
        dmat = jnp.exp(d_log - m_t)
        inter = jnp.exp(inter_log - m_t)
        s = _dot_nt(qc, kc) * dmat
        c_prev = cst_s[j]
        n_prev = nst_s[j:j + 1, :]
        num = inter * _dot(qc, c_prev.astype(BF16)) + _dot(s.astype(BF16), vc)
        den = (inter * jnp.sum(qc.astype(F32) * n_prev, axis=-1, keepdims=True)
               + jnp.sum(s, axis=-1, keepdims=True))
        hval = num / jnp.maximum(jnp.abs(den), jnp.exp(-m_t))
        hbuf_s[pl.ds(row0, CHUNK), cols] = hbuf_s[pl.ds(row0, CHUNK), cols] + hval
        m_new = m_t[last:last + 1, :]
        b_last = b_col[last:last + 1, :]
        w_col = jnp.exp(b_last - b_col + i_col - m_new)
        decay = jnp.exp(b_last + m_prev - m_new)
        kw = kc.astype(F32) * w_col
        cst_s[j] = decay * c_prev + _dot_tn(kw.astype(BF16), vc)
        nst_s[j:j + 1, :] = decay * n_prev + jnp.sum(kw, axis=0, keepdims=True)
        mst_s[0:1, :] = jnp.where(lane_i == j, m_new, mrow)

    def chunk_step(it, carry):
        seq = it // nc
        c = it % nc

        @pl.when(c == 0)
        def _():
            if has_init:
                cst_s[...] = c0_ref[seq]
                nst_s[...] = n0_ref[seq]
                mst_s[0:1, 0:N_UNITS] = m0_ref[seq]
            else:
                cst_s[...] = jnp.zeros(cst_s.shape, F32)
                nst_s[...] = jnp.zeros(nst_s.shape, F32)
                mst_s[...] = jnp.zeros(mst_s.shape, F32)

        for d in range(N_DIRS):
            blk = seq * nc + (c if d == 0 else nc - 1 - c)
            row0 = pl.multiple_of(blk * CHUNK, CHUNK)
            gc = gcol_s[pl.ds(row0, CHUNK), :]
            gr = grow_s[blk]
            if d == 0:
                bcol_all = _split_dot_left(tri_lo, gc)
                brow_all = _split_dot(gr, tri_up)
            else:
                bcol_all = _split_dot_left(tri_up, gc)
                brow_all = _split_dot(gr, tri_lo)
            for hh in range(N_HEADS_M):
                unit(d, hh, row0, gc, gr, bcol_all, brow_all)

        if emit_state:
            @pl.when(c == nc - 1)
            def _():
                co_ref[seq] = cst_s[...]
                no_ref[seq] = nst_s[...]
                mo_ref[seq] = mst_s[0:1, 0:N_UNITS]
        return carry

    lax.fori_loop(0, n_blk, chunk_step, 0)

    for hh in range(N_HEADS_M):
        cols = slice(hh * HEAD_DIM_M, (hh + 1) * HEAD_DIM_M)
        hv = hbuf_s[:, cols]
        r = lax.rsqrt(jnp.mean(hv * hv, axis=-1, keepdims=True) + EPS)
        oc = hv * r * gmh_ref[:, cols] * so_s[:, cols]
        mix_s[:, D_FOURIER + D_CONV + hh * HEAD_DIM_M:D_FOURIER + D_CONV + (hh + 1) * HEAD_DIM_M] = oc.astype(BF16)

    ga1 = mod_ref[0][:, 2 * D_MODEL:3 * D_MODEL]
    res = _dot(mix_s[...], wout_ref[...])
    x1_ref[...] = (load_x() + ga1 * res).reshape(bb, t, D_MODEL)


def _const_spec(shape):
    nd = len(shape)
    return pl.BlockSpec(shape, lambda g, _nd=nd: (0,) * _nd, pipeline_mode=pl.Buffered(1))


def _mixer_call(x, pos, mod, per_batch_mod, lw, state0, emit_state):
    nb, t, _ = x.shape
    bb = ROWS // t
    assert bb * t == ROWS and nb % bb == 0 and t % CHUNK == 0
    rows = ROWS
    n_blk = rows // CHUNK
    has_init = state0 is not None
    add_pos = pos is not None

    args = [x]
    in_specs = [pl.BlockSpec((bb, t, D_MODEL), lambda g: (g, 0, 0))]
    if add_pos:
        args.append(pos)
        in_specs.append(_const_spec(pos.shape))
    args.append(mod)
    if per_batch_mod:
        assert bb == 1
        in_specs.append(pl.BlockSpec((1, 1, N_MOD * D_MODEL), lambda g: (g, 0, 0)))
    else:
        in_specs.append(pl.BlockSpec((1, 1, N_MOD * D_MODEL), lambda g: (0, 0, 0)))
    consts = [lw["g1"], lw["wa"], lw["wb"], lw["wq"], lw["wk"], lw["wv"], lw["wo"], lw["wg"], lw["wgt"],
              lw["bg"], lw["bgt"], lw["blk"], lw["dc"][t], lw["ds"][t], lw["wfn"], lw["wdw"], lw["bdw"],
              lw["lng"], lw["lnb"], lw["wpw"], lw["gmh"]]
    for a in consts:
        args.append(a)
        in_specs.append(_const_spec(a.shape))
    if has_init:
        c0, n0, m0 = state0
        args += [c0, n0, m0]
        in_specs += [
            pl.BlockSpec((bb, N_UNITS, HEAD_DIM_M, HEAD_DIM_M), lambda g: (g, 0, 0, 0)),
            pl.BlockSpec((bb, N_UNITS, HEAD_DIM_M), lambda g: (g, 0, 0)),
            pl.BlockSpec((bb, 1, N_UNITS), lambda g: (g, 0, 0)),
        ]
    args.append(lw["wout"])
    in_specs.append(_const_spec(lw["wout"].shape))

    out_shape = [jax.ShapeDtypeStruct((nb, t, D_MODEL), F32)]
    out_specs = [pl.BlockSpec((bb, t, D_MODEL), lambda g: (g, 0, 0))]
    if emit_state:
        out_shape += [
            jax.ShapeDtypeStruct((nb, N_UNITS, HEAD_DIM_M, HEAD_DIM_M), F32),
            jax.ShapeDtypeStruct((nb, N_UNITS, HEAD_DIM_M), F32),
            jax.ShapeDtypeStruct((nb, 1, N_UNITS), F32),
        ]
        out_specs += [
            pl.BlockSpec((bb, N_UNITS, HEAD_DIM_M, HEAD_DIM_M), lambda g: (g, 0, 0, 0)),
            pl.BlockSpec((bb, N_UNITS, HEAD_DIM_M), lambda g: (g, 0, 0)),
            pl.BlockSpec((bb, 1, N_UNITS), lambda g: (g, 0, 0)),
        ]
    scratch = [
        pltpu.VMEM((rows, D_FOURIER), BF16),
        pltpu.VMEM((bb, t + 2 * PAD_LO, D_CONV), F32),
        pltpu.VMEM((rows, D_CONV), BF16),
        pltpu.VMEM((rows, D_MLSTM), BF16),
        pltpu.VMEM((rows, D_MLSTM), BF16),
        pltpu.VMEM((rows, D_MLSTM), BF16),
        pltpu.VMEM((rows, D_MLSTM), F32),
        pltpu.VMEM((rows, N_GATES), F32),
        pltpu.VMEM((n_blk, N_GATES, CHUNK), F32),
        pltpu.VMEM((rows, D_MLSTM), F32),
        pltpu.VMEM((rows, D_MIX), BF16),
        pltpu.VMEM((N_UNITS, HEAD_DIM_M, HEAD_DIM_M), F32),
        pltpu.VMEM((N_UNITS, HEAD_DIM_M), F32),
        pltpu.VMEM((8, 128), F32),
    ]
    body = functools.partial(_mixer_body, bb, t, has_init, emit_state, add_pos)
    return pl.pallas_call(
        body,
        grid=(nb // bb,),
        in_specs=in_specs,
        out_specs=out_specs,
        out_shape=out_shape,
        scratch_shapes=scratch,
        compiler_params=pltpu.CompilerParams(dimension_semantics=("arbitrary",),
                                             vmem_limit_bytes=VMEM_LIMIT),
        name="mixer_t%d" % t,
    )(*args)


def _rmsnorm_rows(x, g):
    ms = jnp.mean(x * x, axis=-1, keepdims=True)
    return x * lax.rsqrt(ms + EPS) * g


def _route_rows(logits_t, b_router_col):
    scores = _sigmoid(logits_t)
    sel = scores + b_router_col
    sel_r = [sel[e:e + 1, :] for e in range(N_EXPERTS)]
    sc_r = [scores[e:e + 1, :] for e in range(N_EXPERTS)]

    best = None
    best_v = None
    for g in range(N_EXPERT_GROUPS):
        a, b, c, d = sel_r[g * EXPERTS_PER_GROUP:(g + 1) * EXPERTS_PER_GROUP]
        hi1, lo1 = jnp.maximum(a, b), jnp.minimum(a, b)
        hi2, lo2 = jnp.maximum(c, d), jnp.minimum(c, d)
        gs = jnp.maximum(hi1, hi2) + jnp.maximum(jnp.minimum(hi1, hi2), jnp.maximum(lo1, lo2))
        if g == 0:
            best = jnp.zeros(gs.shape, jnp.int32)
            best_v = gs
        else:
            upd = gs > best_v
            best = jnp.where(upd, g, best)
            best_v = jnp.where(upd, gs, best_v)

    def pick(rows, j):
        out = rows[(N_EXPERT_GROUPS - 1) * EXPERTS_PER_GROUP + j]
        for g in range(N_EXPERT_GROUPS - 2, -1, -1):
            out = jnp.where(best == g, rows[g * EXPERTS_PER_GROUP + j], out)
        return out

    s = [pick(sel_r, j) for j in range(EXPERTS_PER_GROUP)]
    sc = [pick(sc_r, j) for j in range(EXPERTS_PER_GROUP)]
    rank = [jnp.zeros(best.shape, jnp.int32) for _ in range(EXPERTS_PER_GROUP)]
    for a in range(EXPERTS_PER_GROUP):
        for b in range(a + 1, EXPERTS_PER_GROUP):
            b_first = s[b] > s[a]
            rank[a] = rank[a] + b_first.astype(jnp.int32)
            rank[b] = rank[b] + (1 - b_first.astype(jnp.int32))
    w = [jnp.where(rank[j] < 2, sc[j], 0.0) for j in range(EXPERTS_PER_GROUP)]
    tot = w[0] + w[1] + w[2] + w[3]
    return best, [wj / tot for wj in w]


def _moe_body(is_last, x_ref, mod_ref, g2_ref, wrt_ref, brc_ref, utri_ref, wg_ref, wu_ref, wd_ref, fg_ref,
              o_ref, he_s, p_s, pt_s, ys_s, sm_s):
    gi = pl.program_id(1)
    tm = ROWS
    n_cap_blk = MOE_CAP // MOE_BLK

    @pl.when(gi == 0)
    def _():
        mod = mod_ref[0]
        sh2 = mod[:, 3 * D_MODEL:4 * D_MODEL]
        sc2 = mod[:, 4 * D_MODEL:5 * D_MODEL]
        h = _rmsnorm_rows(x_ref[...], g2_ref[...]) * (1.0 + sc2) + sh2
        he_s[:, 0:D_MODEL] = h.astype(BF16)
        logits_t = lax.dot_general(wrt_ref[...], h, (((1,), (1,)), ((), ())),
                                   preferred_element_type=F32, precision=lax.Precision.HIGHEST)
        best, cw = _route_rows(logits_t, brc_ref[...])

        row_i = lax.broadcasted_iota(jnp.int32, (16, tm), 0)
        onehot = (row_i == best).astype(F32)
        before = _dot(onehot.astype(BF16), utri_ref[...])
        dest = jnp.zeros((1, tm), F32)
        off_blk = jnp.int32(0)
        for g in range(N_EXPERT_GROUPS):
            cnt = jnp.sum(onehot[g:g + 1, :]).astype(jnp.int32)
            n_blk = lax.shift_right_logical(cnt + (MOE_BLK - 1), int(math.log2(MOE_BLK)))
            sm_s[g] = off_blk
            sm_s[N_EXPERT_GROUPS + g] = n_blk
            base = (off_blk * MOE_BLK).astype(F32)
            dest = dest + onehot[g:g + 1, :] * (before[g:g + 1, :] + base)
            off_blk = off_blk + n_blk

        cw_hi = [wj.astype(BF16).astype(F32) for wj in cw]
        cw_lo = [wj - hj for wj, hj in zip(cw, cw_hi)]
        stack = jnp.concatenate([dest] + cw_hi + cw_lo + [jnp.zeros((MOE_EXT - 9, tm), F32)], axis=0)
        cols = stack.T
        he_s[:, D_MODEL:D_MODEL + MOE_EXT] = cols.astype(BF16)
        dest_col = cols[:, 0:1]

        r_iota = lax.broadcasted_iota(jnp.int32, (MOE_BLK, 1), 0)
        for b in range(n_cap_blk):
            rows_f = (r_iota + b * MOE_BLK).astype(F32)
            p_s[b * MOE_BLK:(b + 1) * MOE_BLK, :] = jnp.where(dest == rows_f, 1.0, 0.0).astype(BF16)
        c_iota = lax.broadcasted_iota(jnp.int32, (1, MOE_CAP), 1).astype(F32)
        pt_s[...] = jnp.where(dest_col == c_iota, 1.0, 0.0).astype(BF16)
        ys_s[...] = jnp.zeros(ys_s.shape, BF16)

    first_blk = sm_s[gi]
    n_blk = sm_s[N_EXPERT_GROUPS + gi]

    def block(i, carry):
        r0 = pl.multiple_of((first_blk + i) * MOE_BLK, MOE_BLK)
        xe = _dot(p_s[pl.ds(r0, MOE_BLK), :], he_s[...])
        xs = xe[:, 0:D_MODEL].astype(BF16)
        gate = _dot(xs, wg_ref[0])
        up = _dot(xs, wu_ref[0])
        act = gate * _sigmoid(gate) * up
        parts = []
        for j in range(EXPERTS_PER_GROUP):
            cwj = (xe[:, D_MODEL + 1 + j:D_MODEL + 2 + j]
                   + xe[:, D_MODEL + 1 + EXPERTS_PER_GROUP + j:D_MODEL + 2 + EXPERTS_PER_GROUP + j])
            parts.append((act[:, j * D_EXPERT:(j + 1) * D_EXPERT] * cwj).astype(BF16))
        a = jnp.concatenate(parts, axis=1)
        ys_s[pl.ds(r0, MOE_BLK), :] = _dot(a, wd_ref[0]).astype(BF16)
        return carry

    lax.fori_loop(0, n_blk, block, 0)

    @pl.when(gi == N_EXPERT_GROUPS - 1)
    def _():
        ga2 = mod_ref[0][:, 5 * D_MODEL:6 * D_MODEL]
        xo = x_ref[...] + ga2 * _dot(pt_s[...], ys_s[...])
        if is_last:
            xo = _rmsnorm_rows(xo, fg_ref[...])
        o_ref[...] = xo


def _moe_call(x2d, mod, per_tile_mod, lw, is_last):
    n_tok = x2d.shape[0]
    tm = ROWS
    assert n_tok % tm == 0
    if per_tile_mod:
        mod_spec = pl.BlockSpec((1, 1, N_MOD * D_MODEL), lambda i, g: (i, 0, 0))
    else:
        mod_spec = pl.BlockSpec((1, 1, N_MOD * D_MODEL), lambda i, g: (0, 0, 0))
    d_grp = EXPERTS_PER_GROUP * D_EXPERT

    def const(shape):
        return pl.BlockSpec(shape, lambda i, g: (0,) * len(shape), pipeline_mode=pl.Buffered(1))

    return pl.pallas_call(
        functools.partial(_moe_body, is_last),
        grid=(n_tok // tm, N_EXPERT_GROUPS),
        in_specs=[
            pl.BlockSpec((tm, D_MODEL), lambda i, g: (i, 0)),
            mod_spec,
            const((1, D_MODEL)),
            const((N_EXPERTS, D_MODEL)),
            const((N_EXPERTS, 1)),
            const((tm, tm)),
            pl.BlockSpec((1, D_MODEL, d_grp), lambda i, g: (g, 0, 0)),
            pl.BlockSpec((1, D_MODEL, d_grp), lambda i, g: (g, 0, 0)),
            pl.BlockSpec((1, d_grp, D_MODEL), lambda i, g: (g, 0, 0)),
            const((1, D_MODEL)),
        ],
        out_specs=pl.BlockSpec((tm, D_MODEL), lambda i, g: (i, 0)),
        out_shape=jax.ShapeDtypeStruct((n_tok, D_MODEL), F32),
        scratch_shapes=[
            pltpu.VMEM((tm, D_MODEL + MOE_EXT), BF16),
            pltpu.VMEM((MOE_CAP, tm), BF16),
            pltpu.VMEM((tm, MOE_CAP), BF16),
            pltpu.VMEM((MOE_CAP, D_MODEL), BF16),
            pltpu.SMEM((2 * N_EXPERT_GROUPS,), jnp.int32),
        ],
        compiler_params=pltpu.CompilerParams(dimension_semantics=("arbitrary", "arbitrary"),
                                             vmem_limit_bytes=VMEM_LIMIT),
        name="moe",
    )(x2d, mod, lw["g2"], lw["wrt"], lw["brc"], lw["utri"], lw["weg"], lw["weu"], lw["wed"], lw["fg"])


def _dft_tables(t):
    idx = np.arange(t, dtype=np.int64)
    ang = 2.0 * np.pi * ((idx[:, None] * idx[None, :]) % t).astype(np.float64) / t
    scale = 1.0 / math.sqrt(t * D_FOURIER_GROUP)
    return (np.cos(ang) * scale).astype(np.float32), (-np.sin(ang) * scale).astype(np.float32)


def _group_tables():
    idx = np.arange(D_FOURIER_GROUP, dtype=np.int64)
    ang = 2.0 * np.pi * ((idx[:, None] * idx[None, :]) % D_FOURIER_GROUP).astype(np.float64) / D_FOURIER_GROUP
    eye = np.eye(N_FOURIER_GROUPS)
    return np.concatenate([np.kron(eye, np.cos(ang)), np.kron(eye, np.sin(ang))], axis=1).astype(np.float32)


def _grid_pos_embed(n_tokens, dtype):
    rows = n_tokens // GRID_W
    r = jnp.repeat(jnp.arange(rows, dtype=jnp.float32), GRID_W)
    col = jnp.tile(jnp.arange(GRID_W, dtype=jnp.float32), rows)
    quarter = D_MODEL // 4
    omega = 1.0 / (10000.0 ** (jnp.arange(quarter, dtype=jnp.float32) / quarter))

    def enc(p):
        a = p[:, None] * omega[None, :]
        return jnp.concatenate([jnp.sin(a), jnp.cos(a)], axis=-1)

    return jnp.concatenate([enc(r), enc(col)], axis=-1).astype(dtype)


def kernel(x_prompt, x_sample, state_C, state_n, state_m, c, c_ctx, w_ada, b_ada, norm1_g, norm2_g, w_in, w_fnet, w_dw, b_dw, conv_ln_g, conv_ln_b, w_pw, b_gate, g_mh, w_out, w_router, b_router, w_exp_gate, w_exp_up, w_exp_down, final_g):
    bp, tp, _ = x_prompt.shape
    bs, ts, _ = x_sample.shape

    cv = jnp.zeros((MOD_ROWS, D_MODEL), F32).at[:bs].set(c).at[bs].set(c_ctx)
    mod_all = _ada_call(cv, w_ada, b_ada)

    dft = {t: _dft_tables(t) for t in sorted({tp, ts})}
    dc = {t: jnp.asarray(v[0]).astype(BF16) for t, v in dft.items()}
    ds = {t: jnp.asarray(v[1]).astype(BF16) for t, v in dft.items()}
    blk = jnp.asarray(_group_tables()).astype(BF16)
    pos = _grid_pos_embed(ts, x_sample.dtype)

    o_a = D_FOURIER
    o_b = o_a + 2 * D_CONV
    w_in_b = w_in.astype(BF16)
    w_out_b = w_out.astype(BF16)
    d_grp = EXPERTS_PER_GROUP * D_EXPERT

    def by_group(w):
        w = w.reshape(DEPTH, N_EXPERT_GROUPS, EXPERTS_PER_GROUP, D_MODEL, D_EXPERT)
        return w.transpose(0, 1, 3, 2, 4).reshape(DEPTH, N_EXPERT_GROUPS, D_MODEL, d_grp).astype(BF16)

    weg = by_group(w_exp_gate)
    weu = by_group(w_exp_up)
    wed = w_exp_down.reshape(DEPTH, N_EXPERT_GROUPS, d_grp, D_MODEL).astype(BF16)
    tok = np.arange(ROWS)
    utri = jnp.asarray((tok[:, None] < tok[None, :]).astype(np.float32)).astype(BF16)

    xp, xs = x_prompt, x_sample
    cs, ns, ms = [], [], []
    for l in range(DEPTH):
        wl = w_in_b[l]
        wg = wl[:, o_b + 4 * D_MLSTM:]
        lw = dict(
            g1=norm1_g[l].reshape(1, D_MODEL), g2=norm2_g[l].reshape(1, D_MODEL),
            wa=wl[:, :o_a], wb=wl[:, o_a:o_b],
            wq=wl[:, o_b:o_b + D_MLSTM], wk=wl[:, o_b + D_MLSTM:o_b + 2 * D_MLSTM],
            wv=wl[:, o_b + 2 * D_MLSTM:o_b + 3 * D_MLSTM], wo=wl[:, o_b + 3 * D_MLSTM:o_b + 4 * D_MLSTM],
            wg=wg, wgt=wg.T,
            bg=b_gate[l].reshape(1, N_GATES), bgt=b_gate[l].reshape(N_GATES, 1),
            blk=blk, dc=dc, ds=ds, wfn=w_fnet[l].astype(BF16),
            wdw=jnp.concatenate([w_dw[l], jnp.zeros((1, D_CONV), F32)], axis=0),
            bdw=b_dw[l].reshape(1, D_CONV), lng=conv_ln_g[l].reshape(1, D_CONV),
            lnb=conv_ln_b[l].reshape(1, D_CONV), wpw=w_pw[l].astype(BF16),
            gmh=g_mh[l].reshape(1, D_MLSTM), wout=w_out_b[l],
            wrt=w_router.T, brc=b_router.reshape(N_EXPERTS, 1), utri=utri,
            weg=weg[l], weu=weu[l], wed=wed[l], fg=final_g.reshape(1, D_MODEL),
        )
        mod_s = mod_all[l, :bs].reshape(bs, 1, N_MOD * D_MODEL)
        mod_p = mod_all[l, bs:bs + 1].reshape(1, 1, N_MOD * D_MODEL)
        is_last = l == DEPTH - 1

        xp, c_l, n_l, m_l = _mixer_call(xp, None, mod_p, False, lw, None, True)
        cs.append(c_l)
        ns.append(n_l)
        ms.append(m_l)
        xp = _moe_call(xp.reshape(bp * tp, D_MODEL), mod_p, False, lw, is_last).reshape(bp, tp, D_MODEL)

        state0 = (state_C[:, l].reshape(bs, N_UNITS, HEAD_DIM_M, HEAD_DIM_M),
                  state_n[:, l].reshape(bs, N_UNITS, HEAD_DIM_M),
                  state_m[:, l].reshape(bs, 1, N_UNITS))
        (xs,) = _mixer_call(xs, pos if l == 0 else None, mod_s, True, lw, state0, False)
        xs = _moe_call(xs.reshape(bs * ts, D_MODEL), mod_s, True, lw, is_last).reshape(bs, ts, D_MODEL)

    new_c = jnp.stack(cs, axis=1).reshape(bp, DEPTH, N_DIRS, N_HEADS_M, HEAD_DIM_M, HEAD_DIM_M)
    new_n = jnp.stack(ns, axis=1).reshape(bp, DEPTH, N_DIRS, N_HEADS_M, HEAD_DIM_M)
    new_m = jnp.stack(ms, axis=1).reshape(bp, DEPTH, N_DIRS, N_HEADS_M)
    return (xp, xs, new_c, new_n, new_m)
```

```python
import functools
import math

import numpy as np
import jax
import jax.numpy as jnp
from jax import lax
from jax.experimental import pallas as pl
from jax.experimental.pallas import tpu as pltpu

D_MODEL = 1024
DEPTH = 4
GRID_W = 64
EPS = 1e-6
D_FOURIER = 256
N_FOURIER_GROUPS = 4
D_FOURIER_GROUP = D_FOURIER // N_FOURIER_GROUPS
D_CONV = 256
CONV_WIDTH = 31
CONV_PAD = CONV_WIDTH // 2
D_MLSTM = 512
N_HEADS_M = 4
HEAD_DIM_M = D_MLSTM // N_HEADS_M
N_DIRS = 2
CHUNK = 128
N_GATES = 2 * N_DIRS * N_HEADS_M
N_UNITS = N_DIRS * N_HEADS_M
D_MIX = D_FOURIER + D_CONV + D_MLSTM
D_IN_PROJ = D_FOURIER + 2 * D_CONV + 4 * D_MLSTM + N_GATES
N_EXPERTS = 16
N_EXPERT_GROUPS = 4
EXPERTS_PER_GROUP = N_EXPERTS // N_EXPERT_GROUPS
D_EXPERT = 256
N_MOD = 6

O_B = D_FOURIER
O_Q = O_B + 2 * D_CONV
O_K = O_Q + D_MLSTM
O_V = O_K + D_MLSTM
O_O = O_V + D_MLSTM
O_G = O_O + D_MLSTM

ROWS = 1024
CONV_ROW_TILE = 64
PAD_LO = 16
MOD_ROWS = 16
AUG = 16
HEAD_AUG = HEAD_DIM_M + AUG
GATE_ROWS = 8
MOE_BLK = 128
MOE_CAP = ROWS + N_EXPERT_GROUPS * MOE_BLK
MOE_EXT = 128
VMEM_LIMIT = 56 * 1024 * 1024

F32 = jnp.float32
BF16 = jnp.bfloat16


def _dot(a, b):
    return jnp.dot(a, b, preferred_element_type=F32)


def _dot_nt(a, b):
    return lax.dot_general(a, b, (((1,), (1,)), ((), ())), preferred_element_type=F32)


def _sigmoid(x):
    return 1.0 / (1.0 + jnp.exp(-x))


def _log_sigmoid(x):
    return jnp.minimum(x, 0.0) - jnp.log(1.0 + jnp.exp(-jnp.abs(x)))


def _hi_lo(a_f32):
    hi = a_f32.astype(BF16)
    return hi, (a_f32 - hi.astype(F32)).astype(BF16)


def _split_dot(a_f32, b_bf16):
    hi, lo = _hi_lo(a_f32)
    return _dot(hi, b_bf16) + _dot(lo, b_bf16)


def _split_dot_left(b_bf16, a_f32):
    hi, lo = _hi_lo(a_f32)
    return _dot(b_bf16, hi) + _dot(b_bf16, lo)


def _rmsnorm_rows(x, g):
    ms = jnp.mean(x * x, axis=-1, keepdims=True)
    return x * lax.rsqrt(ms + EPS) * g


def _ada_body(c_ref, w_ref, b_ref, o_ref):
    cv = c_ref[...]
    s = cv * _sigmoid(cv)
    o_ref[0] = jnp.dot(s, w_ref[0], preferred_element_type=F32,
                       precision=lax.Precision.HIGHEST) + b_ref[0]


def _ada_call(cv, w_ada, b_ada):
    n_col = N_MOD * D_MODEL
    tn = D_MODEL
    return pl.pallas_call(
        _ada_body,
        grid=(DEPTH, n_col // tn),
        in_specs=[
            pl.BlockSpec((MOD_ROWS, D_MODEL), lambda l, j: (0, 0)),
            pl.BlockSpec((1, D_MODEL, tn), lambda l, j: (l, 0, j)),
            pl.BlockSpec((1, 1, tn), lambda l, j: (l, 0, j)),
        ],
        out_specs=pl.BlockSpec((1, MOD_ROWS, tn), lambda l, j: (l, 0, j)),
        out_shape=jax.ShapeDtypeStruct((DEPTH, MOD_ROWS, n_col), F32),
        compiler_params=pltpu.CompilerParams(dimension_semantics=("arbitrary", "arbitrary")),
        name="ada",
    )(cv, w_ada, b_ada.reshape(DEPTH, 1, n_col))


def _scan_max_lanes(x, reverse):
    lane = lax.broadcasted_iota(jnp.int32, x.shape, 1)
    neg_inf = jnp.float32(-jnp.inf)
    sh = 1
    while sh < CHUNK:
        if reverse:
            y = pltpu.roll(x, CHUNK - sh, axis=1)
            x = jnp.maximum(x, jnp.where(lane < CHUNK - sh, y, neg_inf))
        else:
            y = pltpu.roll(x, sh, axis=1)
            x = jnp.maximum(x, jnp.where(lane >= sh, y, neg_inf))
        sh *= 2
    return x


def _mixer_body(bb, t, has_init, emit_state, add_pos, *refs):
    rows = bb * t
    nc = t // CHUNK
    n_blk = rows // CHUNK
    refs = list(refs)
    x_ref = refs.pop(0)
    pos_ref = refs.pop(0) if add_pos else None
    (mod_ref, g1_ref, win_ref, wgt_ref, bg_ref, bgt_ref, blk_ref, dc_ref, ds_ref, wfn_ref,
     wdw_ref, bdw_ref, lng_ref, lnb_ref, wpw_ref, gmh_ref) = refs[:16]
    refs = refs[16:]
    if has_init:
        c0_ref, n0_ref, m0_ref = refs[:3]
        refs = refs[3:]
    wout_ref = refs.pop(0)
    x1_ref = refs.pop(0)
    if emit_state:
        co_ref, no_ref, mo_ref = refs[:3]
        refs = refs[3:]
    (xa_s, pad_s, cact_s, qt_s, k_s, vta_s, so_s, gcol_s, grow_s, hdir_s, mix_s, cta_s, mst_s) = refs

    def load_x():
        xv = x_ref[...].reshape(rows, D_MODEL)
        if add_pos:
            xv = xv + pos_ref[...]
        return xv

    def w_in(lo, hi):
        return win_ref[0, :, lo:hi]

    x = load_x()
    mod = mod_ref[0]
    sh1 = mod[:, 0:D_MODEL]
    sc1 = mod[:, D_MODEL:2 * D_MODEL]
    h = _rmsnorm_rows(x, g1_ref[0]) * (1.0 + sc1) + sh1
    hb = h.astype(BF16)

    xa_s[...] = _dot(hb, w_in(0, O_B)).astype(BF16)
    pb = _dot(hb, w_in(O_B, O_Q))
    u = pb[:, :D_CONV] * _sigmoid(pb[:, D_CONV:])
    zpad = jnp.zeros((PAD_LO, D_CONV), F32)
    for i in range(bb):
        pad_s[i, 0:PAD_LO, :] = zpad
        pad_s[i, PAD_LO:PAD_LO + t, :] = u[i * t:(i + 1) * t]
        pad_s[i, PAD_LO + t:2 * PAD_LO + t, :] = zpad
    k_s[...] = (_dot(hb, w_in(O_K, O_V)) * (HEAD_DIM_M ** -0.5)).astype(BF16)
    so_s[...] = _sigmoid(_dot(hb, w_in(O_O, O_G)))
    qf = _dot(hb, w_in(O_Q, O_K))
    vf = _dot(hb, w_in(O_V, O_O))
    ones_rows = jnp.where(lax.broadcasted_iota(jnp.int32, (AUG, CHUNK), 0) == 0, 1.0, 0.0).astype(BF16)
    for b in range(n_blk):
        rs = slice(b * CHUNK, (b + 1) * CHUNK)
        qt_s[b] = qf[rs, :].T.astype(BF16)
        vt = vf[rs, :].T.astype(BF16)
        for hh in range(N_HEADS_M):
            vta_s[b, hh, 0:HEAD_DIM_M, :] = vt[hh * HEAD_DIM_M:(hh + 1) * HEAD_DIM_M, :]
            vta_s[b, hh, HEAD_DIM_M:HEAD_AUG, :] = ones_rows

    gcol = _dot(hb, w_in(O_G, D_IN_PROJ)) + bg_ref[0]
    kind_c = (lax.broadcasted_iota(jnp.int32, gcol.shape, 1) // N_HEADS_M) % 2
    gcol_s[...] = jnp.where(kind_c == 1, _log_sigmoid(gcol), gcol)
    grow = _dot_nt(wgt_ref[0], hb) + bgt_ref[0]
    kind_r = (lax.broadcasted_iota(jnp.int32, grow.shape, 0) // N_HEADS_M) % 2
    grow = jnp.where(kind_r == 1, _log_sigmoid(grow), grow)
    for b in range(n_blk):
        grow_s[b] = grow[:, b * CHUNK:(b + 1) * CHUNK]

    for i in range(bb):
        xa_i = xa_s[i * t:(i + 1) * t, :]
        uu = _dot(xa_i, blk_ref[...])
        y = (_dot(dc_ref[...], uu[:, :D_FOURIER].astype(BF16))
             + _dot(ds_ref[...], uu[:, D_FOURIER:].astype(BF16)))
        mix_s[i * t:(i + 1) * t, 0:D_FOURIER] = _dot(y.astype(BF16), wfn_ref[0]).astype(BF16)

    wdw = wdw_ref[0]
    for i in range(bb):
        for r0 in range(0, t, CONV_ROW_TILE):
            acc = jnp.broadcast_to(bdw_ref[0], (CONV_ROW_TILE, D_CONV))
            for j in range(CONV_WIDTH):
                start = PAD_LO - CONV_PAD + r0 + j
                acc = acc + pad_s[i, start:start + CONV_ROW_TILE, :] * wdw[j:j + 1, :]
            mu = jnp.mean(acc, axis=-1, keepdims=True)
            cen = acc - mu
            var = jnp.mean(cen * cen, axis=-1, keepdims=True)
            uf = cen * lax.rsqrt(var + EPS) * lng_ref[0] + lnb_ref[0]
            act = uf * _sigmoid(uf)
            cact_s[i * t + r0:i * t + r0 + CONV_ROW_TILE, :] = act.astype(BF16)
    mix_s[:, D_FOURIER:D_FOURIER + D_CONV] = _dot(cact_s[...], wpw_ref[0]).astype(BF16)

    r_i = lax.broadcasted_iota(jnp.int32, (CHUNK, CHUNK), 0)
    c_i = lax.broadcasted_iota(jnp.int32, (CHUNK, CHUNK), 1)
    lower = r_i >= c_i
    upper = r_i <= c_i
    tri_lo = lower.astype(BF16)
    tri_up = upper.astype(BF16)
    neg_inf = jnp.float32(-jnp.inf)

    def chunk_step(it, carry):
        seq = it // nc
        c = it % nc

        @pl.when(c == 0)
        def _():
            if has_init:
                for j in range(N_UNITS):
                    cta_s[j, 0:HEAD_DIM_M, :] = c0_ref[seq, 0, j].T
                    first_row = lax.broadcasted_iota(jnp.int32, (AUG, HEAD_DIM_M), 0) == 0
                    cta_s[j, HEAD_DIM_M:HEAD_AUG, :] = jnp.where(first_row, n0_ref[seq, 0, j:j + 1, :], 0.0)
                mst_s[...] = m0_ref[seq, 0]
            else:
                cta_s[...] = jnp.zeros(cta_s.shape, F32)
                mst_s[...] = jnp.zeros(mst_s.shape, F32)

        units = []
        for d in range(N_DIRS):
            blk = seq * nc + (c if d == 0 else nc - 1 - c)
            row0 = pl.multiple_of(blk * CHUNK, CHUNK)
            gc = gcol_s[pl.ds(row0, CHUNK), :]
            gr = grow_s[blk]
            if d == 0:
                bcol_all = _split_dot_left(tri_lo, gc)
                brow_all = _split_dot(gr, tri_up)
            else:
                bcol_all = _split_dot_left(tri_up, gc)
                brow_all = _split_dot(gr, tri_lo)
            g0 = d * 2 * N_HEADS_M
            last = CHUNK - 1 if d == 0 else 0
            i_rows = gr[g0:g0 + GATE_ROWS, :]
            b_rows = pltpu.roll(brow_all[g0:g0 + GATE_ROWS, :], N_HEADS_M, axis=0)
            a_rows = i_rows - b_rows
            m_prev = mst_s[d]
            m_t = jnp.maximum(b_rows + m_prev, b_rows + _scan_max_lanes(a_rows, d == 1))
            u_rows = b_rows - m_t
            inter_rows = jnp.exp(b_rows + m_prev - m_t)
            floor_rows = jnp.exp(-m_t)
            m_new = m_t[:, last:last + 1]
            b_last = b_rows[:, last:last + 1]
            w_rows = jnp.exp(b_last + a_rows - m_new)
            decay = jnp.exp(b_last + m_prev[:, 0:1] - m_new)
            mst_s[d] = jnp.broadcast_to(m_new, (GATE_ROWS, CHUNK))
            a_cols = gc[:, g0:g0 + N_HEADS_M] - bcol_all[:, g0 + N_HEADS_M:g0 + 2 * N_HEADS_M]
            mask_t = upper if d == 0 else lower
            for hh in range(N_HEADS_M):
                hs = slice(hh * HEAD_DIM_M, (hh + 1) * HEAD_DIM_M)
                units.append(dict(
                    d=d, hh=hh, j=d * N_HEADS_M + hh, hs=hs, blk=blk, mask=mask_t,
                    kc=k_s[pl.ds(row0, CHUNK), hs],
                    qt=qt_s[blk, hs, :],
                    vta=vta_s[blk, hh],
                    z=a_cols[:, hh:hh + 1] + u_rows[hh:hh + 1, :],
                    inter=inter_rows[hh:hh + 1, :], floor=floor_rows[hh:hh + 1, :],
                    w=w_rows[hh:hh + 1, :], decay=decay[hh:hh + 1, :]))
        for un in units:
            un["st"] = _dot(un["kc"], un["qt"])
            un["cta"] = cta_s[un["j"]]
            un["p1"] = _dot(un["cta"].astype(BF16), un["qt"])
        for un in units:
            dm = jnp.exp(jnp.where(un["mask"], un["z"], neg_inf))
            un["p2"] = _dot(un["vta"], (un["st"] * dm).astype(BF16))
            vw = (un["vta"].astype(F32) * un["w"]).astype(BF16)
            un["upd"] = _dot(vw, un["kc"])
        for un in units:
            numa = un["inter"] * un["p1"] + un["p2"]
            den = numa[HEAD_DIM_M:HEAD_DIM_M + 1, :]
            ht = numa[0:HEAD_DIM_M, :] / jnp.maximum(jnp.abs(den), un["floor"])
            hdir_s[un["d"], un["blk"], un["hs"], :] = ht
            cta_s[un["j"]] = un["decay"] * un["cta"] + un["upd"]

        if emit_state:
            @pl.when(c == nc - 1)
            def _():
                for j in range(N_UNITS):
                    co_ref[seq, j] = cta_s[j, 0:HEAD_DIM_M, :].T
                    no_ref[seq, j:j + 1, :] = cta_s[j, HEAD_DIM_M:HEAD_DIM_M + 1, :]
                mo_ref[seq] = mst_s[...]
        return carry

    lax.fori_loop(0, n_blk, chunk_step, 0)

    for b in range(n_blk):
        parts = []
        for hh in range(N_HEADS_M):
            hs = slice(hh * HEAD_DIM_M, (hh + 1) * HEAD_DIM_M)
            hv = hdir_s[0, b, hs, :] + hdir_s[1, b, hs, :]
            r = lax.rsqrt(jnp.mean(hv * hv, axis=0, keepdims=True) + EPS)
            parts.append((hv * r * gmh_ref[0, hs, :]).T)
        rs = slice(b * CHUNK, (b + 1) * CHUNK)
        oc = jnp.concatenate(parts, axis=1) * so_s[rs, :]
        mix_s[rs, D_FOURIER + D_CONV:D_MIX] = oc.astype(BF16)

    ga1 = mod_ref[0][:, 2 * D_MODEL:3 * D_MODEL]
    res = _dot(mix_s[...], wout_ref[0])
    x1_ref[...] = (load_x() + ga1 * res).reshape(bb, t, D_MODEL)


def _whole(a):
    nd = a.ndim
    return pl.BlockSpec(a.shape, lambda g: (0,) * nd, pipeline_mode=pl.Buffered(1))


def _layer_block(a, l):
    nd = a.ndim
    return pl.BlockSpec((1,) + a.shape[1:], lambda g: (l,) + (0,) * (nd - 1), pipeline_mode=pl.Buffered(1))


def _mixer_call(x, pos, mod_rows, mod_base, per_batch_mod, l, pw, state0, emit_state):
    nb, t, _ = x.shape
    bb = ROWS // t
    assert bb * t == ROWS and nb % bb == 0 and t % CHUNK == 0
    rows = ROWS
    n_blk = rows // CHUNK
    has_init = state0 is not None
    add_pos = pos is not None

    args = [x]
    in_specs = [pl.BlockSpec((bb, t, D_MODEL), lambda g: (g, 0, 0))]
    if add_pos:
        args.append(pos)
        in_specs.append(_whole(pos))
    args.append(mod_rows)
    if per_batch_mod:
        assert bb == 1
        in_specs.append(pl.BlockSpec((1, 1, N_MOD * D_MODEL), lambda g: (mod_base + g, 0, 0)))
    else:
        in_specs.append(pl.BlockSpec((1, 1, N_MOD * D_MODEL), lambda g: (mod_base, 0, 0)))
    for name in ("g1", "win", "wgt", "bg", "bgt"):
        args.append(pw[name])
        in_specs.append(_layer_block(pw[name], l))
    for a in (pw["blk"], pw["dc"][t], pw["ds"][t]):
        args.append(a)
        in_specs.append(_whole(a))
    for name in ("wfn", "wdw", "bdw", "lng", "lnb", "wpw", "gmh"):
        args.append(pw[name])
        in_specs.append(_layer_block(pw[name], l))
    if has_init:
        c0, n0, m0 = state0
        args += [c0, n0, m0]
        in_specs += [
            pl.BlockSpec((bb, 1, N_UNITS, HEAD_DIM_M, HEAD_DIM_M), lambda g: (g, l, 0, 0, 0)),
            pl.BlockSpec((bb, 1, N_UNITS, HEAD_DIM_M), lambda g: (g, l, 0, 0)),
            pl.BlockSpec((bb, 1, N_DIRS, GATE_ROWS, CHUNK), lambda g: (g, l, 0, 0, 0)),
        ]
    args.append(pw["wout"])
    in_specs.append(_layer_block(pw["wout"], l))

    out_shape = [jax.ShapeDtypeStruct((nb, t, D_MODEL), F32)]
    out_specs = [pl.BlockSpec((bb, t, D_MODEL), lambda g: (g, 0, 0))]
    if emit_state:
        out_shape += [
            jax.ShapeDtypeStruct((nb, N_UNITS, HEAD_DIM_M, HEAD_DIM_M), F32),
            jax.ShapeDtypeStruct((nb, N_UNITS, HEAD_DIM_M), F32),
            jax.ShapeDtypeStruct((nb, N_DIRS, GATE_ROWS, CHUNK), F32),
        ]
        out_specs += [
            pl.BlockSpec((bb, N_UNITS, HEAD_DIM_M, HEAD_DIM_M), lambda g: (g, 0, 0, 0)),
            pl.BlockSpec((bb, N_UNITS, HEAD_DIM_M), lambda g: (g, 0, 0)),
            pl.BlockSpec((bb, N_DIRS, GATE_ROWS, CHUNK), lambda g: (g, 0, 0, 0)),
        ]
    scratch = [
        pltpu.VMEM((rows, D_FOURIER), BF16),
        pltpu.VMEM((bb, t + 2 * PAD_LO, D_CONV), F32),
        pltpu.VMEM((rows, D_CONV), BF16),
        pltpu.VMEM((n_blk, D_MLSTM, CHUNK), BF16),
        pltpu.VMEM((rows, D_MLSTM), BF16),
        pltpu.VMEM((n_blk, N_HEADS_M, HEAD_AUG, CHUNK), BF16),
        pltpu.VMEM((rows, D_MLSTM), F32),
        pltpu.VMEM((rows, N_GATES), F32),
        pltpu.VMEM((n_blk, N_GATES, CHUNK), F32),
        pltpu.VMEM((N_DIRS, n_blk, D_MLSTM, CHUNK), F32),
        pltpu.VMEM((rows, D_MIX), BF16),
        pltpu.VMEM((N_UNITS, HEAD_AUG, HEAD_DIM_M), F32),
        pltpu.VMEM((N_DIRS, GATE_ROWS, CHUNK), F32),
    ]
    body = functools.partial(_mixer_body, bb, t, has_init, emit_state, add_pos)
    return pl.pallas_call(
        body,
        grid=(nb // bb,),
        in_specs=in_specs,
        out_specs=out_specs,
        out_shape=out_shape,
        scratch_shapes=scratch,
        compiler_params=pltpu.CompilerParams(dimension_semantics=("arbitrary",),
                                             vmem_limit_bytes=VMEM_LIMIT),
        name="mixer_t%d" % t,
    )(*args)


def _route_rows(logits_t, b_router_col):
    scores = _sigmoid(logits_t)
    sel = scores + b_router_col
    sel_r = [sel[e:e + 1, :] for e in range(N_EXPERTS)]
    sc_r = [scores[e:e + 1, :] for e in range(N_EXPERTS)]

    best = None
    best_v = None
    for g in range(N_EXPERT_GROUPS):
        a, b, c, d = sel_r[g * EXPERTS_PER_GROUP:(g + 1) * EXPERTS_PER_GROUP]
        hi1, lo1 = jnp.maximum(a, b), jnp.minimum(a, b)
        hi2, lo2 = jnp.maximum(c, d), jnp.minimum(c, d)
        gs = jnp.maximum(hi1, hi2) + jnp.maximum(jnp.minimum(hi1, hi2), jnp.maximum(lo1, lo2))
        if g == 0:
            best = jnp.zeros(gs.shape, jnp.int32)
            best_v = gs
        else:
            upd = gs > best_v
            best = jnp.where(upd, g, best)
            best_v = jnp.where(upd, gs, best_v)

    def pick(rows, j):
        out = rows[(N_EXPERT_GROUPS - 1) * EXPERTS_PER_GROUP + j]
        for g in range(N_EXPERT_GROUPS - 2, -1, -1):
            out = jnp.where(best == g, rows[g * EXPERTS_PER_GROUP + j], out)
        return out

    s = [pick(sel_r, j) for j in range(EXPERTS_PER_GROUP)]
    sc = [pick(sc_r, j) for j in range(EXPERTS_PER_GROUP)]
    rank = [jnp.zeros(best.shape, jnp.int32) for _ in range(EXPERTS_PER_GROUP)]
    for a in range(EXPERTS_PER_GROUP):
        for b in range(a + 1, EXPERTS_PER_GROUP):
            b_first = s[b] > s[a]
            rank[a] = rank[a] + b_first.astype(jnp.int32)
            rank[b] = rank[b] + (1 - b_first.astype(jnp.int32))
    w = [jnp.where(rank[j] < 2, sc[j], 0.0) for j in range(EXPERTS_PER_GROUP)]
    tot = w[0] + w[1] + w[2] + w[3]
    return best, [wj / tot for wj in w]


def _moe_body(is_last, x_ref, mod_ref, g2_ref, wrt_ref, brc_ref, utri_ref, wg_ref, wu_ref, wd_ref, fg_ref,
              o_ref, he_s, p_s, pt_s, ys_s, sm_s):
    gi = pl.program_id(1)
    tm = ROWS
    n_cap_blk = MOE_CAP // MOE_BLK

    @pl.when(gi == 0)
    def _():
        mod = mod_ref[0]
        sh2 = mod[:, 3 * D_MODEL:4 * D_MODEL]
        sc2 = mod[:, 4 * D_MODEL:5 * D_MODEL]
        h = _rmsnorm_rows(x_ref[...], g2_ref[0]) * (1.0 + sc2) + sh2
        he_s[:, 0:D_MODEL] = h.astype(BF16)
        logits_t = lax.dot_general(wrt_ref[...], h, (((1,), (1,)), ((), ())),
                                   preferred_element_type=F32, precision=lax.Precision.HIGHEST)
        best, cw = _route_rows(logits_t, brc_ref[...])

        row_i = lax.broadcasted_iota(jnp.int32, (16, tm), 0)
        onehot = (row_i == best).astype(F32)
        before = _dot(onehot.astype(BF16), utri_ref[...])
        dest = jnp.zeros((1, tm), F32)
        off_blk = jnp.int32(0)
        for g in range(N_EXPERT_GROUPS):
            cnt = jnp.sum(onehot[g:g + 1, :]).astype(jnp.int32)
            n_blk = lax.shift_right_logical(cnt + (MOE_BLK - 1), int(math.log2(MOE_BLK)))
            sm_s[g] = off_blk
            sm_s[N_EXPERT_GROUPS + g] = n_blk
            base = (off_blk * MOE_BLK).astype(F32)
            dest = dest + onehot[g:g + 1, :] * (before[g:g + 1, :] + base)
            off_blk = off_blk + n_blk

        cw_hi = [wj.astype(BF16).astype(F32) for wj in cw]
        cw_lo = [wj - hj for wj, hj in zip(cw, cw_hi)]
        stack = jnp.concatenate([dest] + cw_hi + cw_lo + [jnp.zeros((MOE_EXT - 9, tm), F32)], axis=0)
        cols = stack.T
        he_s[:, D_MODEL:D_MODEL + MOE_EXT] = cols.astype(BF16)
        dest_col = cols[:, 0:1]

        r_iota = lax.broadcasted_iota(jnp.int32, (MOE_BLK, 1), 0)
        for b in range(n_cap_blk):
            rows_f = (r_iota + b * MOE_BLK).astype(F32)
            p_s[b * MOE_BLK:(b + 1) * MOE_BLK, :] = jnp.where(dest == rows_f, 1.0, 0.0).astype(BF16)
        c_iota = lax.broadcasted_iota(jnp.int32, (1, MOE_CAP), 1).astype(F32)
        pt_s[...] = jnp.where(dest_col == c_iota, 1.0, 0.0).astype(BF16)
        ys_s[...] = jnp.zeros(ys_s.shape, BF16)

    first_blk = sm_s[gi]
    n_blk = sm_s[N_EXPERT_GROUPS + gi]

    def block(i, carry):
        r0 = pl.multiple_of((first_blk + i) * MOE_BLK, MOE_BLK)
        xe = _dot(p_s[pl.ds(r0, MOE_BLK), :], he_s[...])
        xs = xe[:, 0:D_MODEL].astype(BF16)
        y = None
        for j in range(EXPERTS_PER_GROUP):
            gate = _dot(xs, wg_ref[0, j])
            up = _dot(xs, wu_ref[0, j])
            cwj = (xe[:, D_MODEL + 1 + j:D_MODEL + 2 + j]
                   + xe[:, D_MODEL + 1 + EXPERTS_PER_GROUP + j:D_MODEL + 2 + EXPERTS_PER_GROUP + j])
            a = (gate * _sigmoid(gate) * up * cwj).astype(BF16)
            yj = _dot(a, wd_ref[0, j])
            y = yj if y is None else y + yj
        ys_s[pl.ds(r0, MOE_BLK), :] = y.astype(BF16)
        return carry

    lax.fori_loop(0, n_blk, block, 0)

    @pl.when(gi == N_EXPERT_GROUPS - 1)
    def _():
        ga2 = mod_ref[0][:, 5 * D_MODEL:6 * D_MODEL]
        xo = x_ref[...] + ga2 * _dot(pt_s[...], ys_s[...])
        if is_last:
            xo = _rmsnorm_rows(xo, fg_ref[...])
        o_ref[...] = xo


def _moe_call(x2d, mod_rows, mod_base, per_tile_mod, l, pw, is_last):
    n_tok = x2d.shape[0]
    tm = ROWS
    assert n_tok % tm == 0
    if per_tile_mod:
        mod_spec = pl.BlockSpec((1, 1, N_MOD * D_MODEL), lambda i, g: (mod_base + i, 0, 0))
    else:
        mod_spec = pl.BlockSpec((1, 1, N_MOD * D_MODEL), lambda i, g: (mod_base, 0, 0))

    def const(shape):
        return pl.BlockSpec(shape, lambda i, g: (0,) * len(shape), pipeline_mode=pl.Buffered(1))

    def grp(shape):
        return pl.BlockSpec((1, EXPERTS_PER_GROUP) + shape, lambda i, g: (l, g, 0, 0))

    return pl.pallas_call(
        functools.partial(_moe_body, is_last),
        grid=(n_tok // tm, N_EXPERT_GROUPS),
        in_specs=[
            pl.BlockSpec((tm, D_MODEL), lambda i, g: (i, 0)),
            mod_spec,
            pl.BlockSpec((1, 1, D_MODEL), lambda i, g: (l, 0, 0), pipeline_mode=pl.Buffered(1)),
            const((N_EXPERTS, D_MODEL)),
            const((N_EXPERTS, 1)),
            const((tm, tm)),
            grp((D_MODEL, D_EXPERT)),
            grp((D_MODEL, D_EXPERT)),
            grp((D_EXPERT, D_MODEL)),
            const((1, D_MODEL)),
        ],
        out_specs=pl.BlockSpec((tm, D_MODEL), lambda i, g: (i, 0)),
        out_shape=jax.ShapeDtypeStruct((n_tok, D_MODEL), F32),
        scratch_shapes=[
            pltpu.VMEM((tm, D_MODEL + MOE_EXT), BF16),
            pltpu.VMEM((MOE_CAP, tm), BF16),
            pltpu.VMEM((tm, MOE_CAP), BF16),
            pltpu.VMEM((MOE_CAP, D_MODEL), BF16),
            pltpu.SMEM((2 * N_EXPERT_GROUPS,), jnp.int32),
        ],
        compiler_params=pltpu.CompilerParams(dimension_semantics=("arbitrary", "arbitrary"),
                                             vmem_limit_bytes=VMEM_LIMIT),
        name="moe",
    )(x2d, mod_rows, pw["g2"], pw["wrt"], pw["brc"], pw["utri"], pw["weg"], pw["weu"], pw["wed"], pw["fg"])


def _dft_tables(t):
    idx = np.arange(t, dtype=np.int64)
    ang = 2.0 * np.pi * ((idx[:, None] * idx[None, :]) % t).astype(np.float64) / t
    scale = 1.0 / math.sqrt(t * D_FOURIER_GROUP)
    return (np.cos(ang) * scale).astype(np.float32), (-np.sin(ang) * scale).astype(np.float32)


def _group_tables():
    idx = np.arange(D_FOURIER_GROUP, dtype=np.int64)
    ang = 2.0 * np.pi * ((idx[:, None] * idx[None, :]) % D_FOURIER_GROUP).astype(np.float64) / D_FOURIER_GROUP
    eye = np.eye(N_FOURIER_GROUPS)
    return np.concatenate([np.kron(eye, np.cos(ang)), np.kron(eye, np.sin(ang))], axis=1).astype(np.float32)


def _grid_pos_embed(n_tokens, dtype):
    rows = n_tokens // GRID_W
    r = jnp.repeat(jnp.arange(rows, dtype=jnp.float32), GRID_W)
    col = jnp.tile(jnp.arange(GRID_W, dtype=jnp.float32), rows)
    quarter = D_MODEL // 4
    omega = 1.0 / (10000.0 ** (jnp.arange(quarter, dtype=jnp.float32) / quarter))

    def enc(p):
        a = p[:, None] * omega[None, :]
        return jnp.concatenate([jnp.sin(a), jnp.cos(a)], axis=-1)

    return jnp.concatenate([enc(r), enc(col)], axis=-1).astype(dtype)


def _prepare(seq_lens, norm1_g, norm2_g, w_in, w_fnet, w_dw, b_dw, conv_ln_g, conv_ln_b, w_pw, b_gate, g_mh,
             w_out, w_router, b_router, w_exp_gate, w_exp_up, w_exp_down, final_g):
    dft = {t: _dft_tables(t) for t in sorted(set(seq_lens))}
    w_in_b = w_in.astype(BF16)
    tok = np.arange(ROWS)
    return dict(
        g1=norm1_g.reshape(DEPTH, 1, D_MODEL), g2=norm2_g.reshape(DEPTH, 1, D_MODEL),
        win=w_in_b, wgt=jnp.swapaxes(w_in_b[:, :, O_G:], 1, 2),
        bg=b_gate.reshape(DEPTH, 1, N_GATES), bgt=b_gate.reshape(DEPTH, N_GATES, 1),
        blk=jnp.asarray(_group_tables()).astype(BF16),
        dc={t: jnp.asarray(v[0]).astype(BF16) for t, v in dft.items()},
        ds={t: jnp.asarray(v[1]).astype(BF16) for t, v in dft.items()},
        wfn=w_fnet.astype(BF16),
        wdw=jnp.concatenate([w_dw, jnp.zeros((DEPTH, 1, D_CONV), F32)], axis=1),
        bdw=b_dw.reshape(DEPTH, 1, D_CONV), lng=conv_ln_g.reshape(DEPTH, 1, D_CONV),
        lnb=conv_ln_b.reshape(DEPTH, 1, D_CONV), wpw=w_pw.astype(BF16),
        gmh=jnp.broadcast_to(g_mh.reshape(DEPTH, D_MLSTM, 1), (DEPTH, D_MLSTM, CHUNK)),
        wout=w_out.astype(BF16),
        wrt=w_router.T, brc=b_router.reshape(N_EXPERTS, 1),
        utri=jnp.asarray((tok[:, None] < tok[None, :]).astype(np.float32)).astype(BF16),
        weg=w_exp_gate.astype(BF16), weu=w_exp_up.astype(BF16), wed=w_exp_down.astype(BF16),
        fg=final_g.reshape(1, D_MODEL),
    )


def kernel(x_prompt, x_sample, state_C, state_n, state_m, c, c_ctx, w_ada, b_ada, norm1_g, norm2_g, w_in, w_fnet, w_dw, b_dw, conv_ln_g, conv_ln_b, w_pw, b_gate, g_mh, w_out, w_router, b_router, w_exp_gate, w_exp_up, w_exp_down, final_g):
    bp, tp, _ = x_prompt.shape
    bs, ts, _ = x_sample.shape

    cv = jnp.zeros((MOD_ROWS, D_MODEL), F32).at[:bs].set(c).at[bs].set(c_ctx)
    mod_all = _ada_call(cv, w_ada, b_ada)
    mod_rows = mod_all.reshape(DEPTH * MOD_ROWS, 1, N_MOD * D_MODEL)

    pw = _prepare((tp, ts), norm1_g, norm2_g, w_in, w_fnet, w_dw, b_dw, conv_ln_g, conv_ln_b, w_pw, b_gate,
                  g_mh, w_out, w_router, b_router, w_exp_gate, w_exp_up, w_exp_down, final_g)
    pos = _grid_pos_embed(ts, x_sample.dtype)

    c0 = state_C.reshape(bs, DEPTH, N_UNITS, HEAD_DIM_M, HEAD_DIM_M)
    n0 = state_n.reshape(bs, DEPTH, N_UNITS, HEAD_DIM_M)
    m0 = jnp.pad(state_m, ((0, 0), (0, 0), (0, 0), (0, GATE_ROWS - N_HEADS_M)))
    m0 = jnp.broadcast_to(m0[..., None], (bs, DEPTH, N_DIRS, GATE_ROWS, CHUNK))

    xp, xs = x_prompt, x_sample
    cs, ns, ms = [], [], []
    for l in range(DEPTH):
        is_last = l == DEPTH - 1
        base = l * MOD_ROWS
        xp, c_l, n_l, m_l = _mixer_call(xp, None, mod_rows, base + bs, False, l, pw, None, True)
        cs.append(c_l)
        ns.append(n_l)
        ms.append(m_l[:, :, :N_HEADS_M, 0])
        xp = _moe_call(xp.reshape(bp * tp, D_MODEL), mod_rows, base + bs, False, l, pw, is_last)
        xp = xp.reshape(bp, tp, D_MODEL)

        (xs,) = _mixer_call(xs, pos if l == 0 else None, mod_rows, base, True, l, pw, (c0, n0, m0), False)
        xs = _moe_call(xs.reshape(bs * ts, D_MODEL), mod_rows, base, True, l, pw, is_last)
        xs = xs.reshape(bs, ts, D_MODEL)

    new_c = jnp.stack(cs, axis=1).reshape(bp, DEPTH, N_DIRS, N_HEADS_M, HEAD_DIM_M, HEAD_DIM_M)
    new_n = jnp.stack(ns, axis=1).reshape(bp, DEPTH, N_DIRS, N_HEADS_M, HEAD_DIM_M)
    new_m = jnp.stack(ms, axis=1).reshape(bp, DEPTH, N_DIRS, N_HEADS_M)
    return (xp, xs, new_c, new_n, new_m)
```

```python
import functools
import math

import numpy as np
import jax
import jax.numpy as jnp
from jax import lax
from jax.experimental import pallas as pl
from jax.experimental.pallas import tpu as pltpu

D_MODEL = 1024
DEPTH = 4
GRID_W = 64
EPS = 1e-6
D_FOURIER = 256
N_FOURIER_GROUPS = 4
D_FOURIER_GROUP = D_FOURIER // N_FOURIER_GROUPS
D_CONV = 256
CONV_WIDTH = 31
CONV_PAD = CONV_WIDTH // 2
D_MLSTM = 512
N_HEADS_M = 4
HEAD_DIM_M = D_MLSTM // N_HEADS_M
N_DIRS = 2
CHUNK = 128
N_GATES = 2 * N_DIRS * N_HEADS_M
N_UNITS = N_DIRS * N_HEADS_M
D_MIX = D_FOURIER + D_CONV + D_MLSTM
D_IN_PROJ = D_FOURIER + 2 * D_CONV + 4 * D_MLSTM + N_GATES
N_EXPERTS = 16
N_EXPERT_GROUPS = 4
EXPERTS_PER_GROUP = N_EXPERTS // N_EXPERT_GROUPS
D_EXPERT = 256
N_MOD = 6

O_B = D_FOURIER
O_Q = O_B + 2 * D_CONV
O_K = O_Q + D_MLSTM
O_V = O_K + D_MLSTM
O_O = O_V + D_MLSTM
O_G = O_O + D_MLSTM

ROWS = 1024
CONV_ROW_TILE = 64
PAD_LO = 16
MOD_ROWS = 16
AUG = 16
HEAD_AUG = HEAD_DIM_M + AUG
GATE_ROWS = 8
MOE_BLK = 128
MOE_CAP = ROWS + N_EXPERT_GROUPS * MOE_BLK
MOE_EXT = 128
VMEM_LIMIT = 56 * 1024 * 1024

F32 = jnp.float32
BF16 = jnp.bfloat16


def _dot(a, b):
    return jnp.dot(a, b, preferred_element_type=F32)


def _dot_nt(a, b):
    return lax.dot_general(a, b, (((1,), (1,)), ((), ())), preferred_element_type=F32)


def _sigmoid(x):
    return 1.0 / (1.0 + jnp.exp(-x))


def _log_sigmoid(x):
    return jnp.minimum(x, 0.0) - jnp.log(1.0 + jnp.exp(-jnp.abs(x)))


def _hi_lo(a_f32):
    hi = a_f32.astype(BF16)
    return hi, (a_f32 - hi.astype(F32)).astype(BF16)


def _split_dot(a_f32, b_bf16):
    hi, lo = _hi_lo(a_f32)
    return _dot(hi, b_bf16) + _dot(lo, b_bf16)


def _split_dot_left(b_bf16, a_f32):
    hi, lo = _hi_lo(a_f32)
    return _dot(b_bf16, hi) + _dot(b_bf16, lo)


def _rmsnorm_rows(x, g):
    ms = jnp.mean(x * x, axis=-1, keepdims=True)
    return x * lax.rsqrt(ms + EPS) * g


def _ada_body(c_ref, w_ref, b_ref, o_ref):
    cv = c_ref[...]
    s = cv * _sigmoid(cv)
    o_ref[0] = jnp.dot(s, w_ref[0], preferred_element_type=F32,
                       precision=lax.Precision.HIGHEST) + b_ref[0]


def _ada_call(cv, w_ada, b_ada):
    n_col = N_MOD * D_MODEL
    tn = D_MODEL
    return pl.pallas_call(
        _ada_body,
        grid=(DEPTH, n_col // tn),
        in_specs=[
            pl.BlockSpec((MOD_ROWS, D_MODEL), lambda l, j: (0, 0)),
            pl.BlockSpec((1, D_MODEL, tn), lambda l, j: (l, 0, j)),
            pl.BlockSpec((1, 1, tn), lambda l, j: (l, 0, j)),
        ],
        out_specs=pl.BlockSpec((1, MOD_ROWS, tn), lambda l, j: (l, 0, j)),
        out_shape=jax.ShapeDtypeStruct((DEPTH, MOD_ROWS, n_col), F32),
        compiler_params=pltpu.CompilerParams(dimension_semantics=("arbitrary", "arbitrary")),
        name="ada",
    )(cv, w_ada, b_ada.reshape(DEPTH, 1, n_col))


def _scan_max_lanes(x, reverse):
    lane = lax.broadcasted_iota(jnp.int32, x.shape, 1)
    neg_inf = jnp.float32(-jnp.inf)
    sh = 1
    while sh < CHUNK:
        if reverse:
            y = pltpu.roll(x, CHUNK - sh, axis=1)
            x = jnp.maximum(x, jnp.where(lane < CHUNK - sh, y, neg_inf))
        else:
            y = pltpu.roll(x, sh, axis=1)
            x = jnp.maximum(x, jnp.where(lane >= sh, y, neg_inf))
        sh *= 2
    return x


def _mixer_body(bb, t, has_init, emit_state, add_pos, *refs):
    rows = bb * t
    nc = t // CHUNK
    n_blk = rows // CHUNK
    refs = list(refs)
    x_ref = refs.pop(0)
    pos_ref = refs.pop(0) if add_pos else None
    (mod_ref, g1_ref, win_ref, wgt_ref, bg_ref, bgt_ref, blk_ref, dc_ref, ds_ref, wfn_ref,
     wdw_ref, bdw_ref, lng_ref, lnb_ref, wpw_ref, gmh_ref) = refs[:16]
    refs = refs[16:]
    if has_init:
        c0_ref, n0_ref, m0_ref = refs[:3]
        refs = refs[3:]
    wout_ref = refs.pop(0)
    x1_ref = refs.pop(0)
    if emit_state:
        co_ref, no_ref, mo_ref = refs[:3]
        refs = refs[3:]
    (xa_s, pad_s, cact_s, qt_s, k_s, vta_s, so_s, rows_s, acol_s, mprev_s, ctab_s,
     mix_s, cta_s, mst_s) = refs

    def load_x():
        xv = x_ref[...].reshape(rows, D_MODEL)
        if add_pos:
            xv = xv + pos_ref[...]
        return xv

    def w_in(lo, hi):
        return win_ref[0, :, lo:hi]

    x = load_x()
    mod = mod_ref[0]
    sh1 = mod[:, 0:D_MODEL]
    sc1 = mod[:, D_MODEL:2 * D_MODEL]
    h = _rmsnorm_rows(x, g1_ref[0]) * (1.0 + sc1) + sh1
    hb = h.astype(BF16)

    pb = _dot(hb, w_in(O_B, O_Q))
    u = pb[:, :D_CONV] * _sigmoid(pb[:, D_CONV:])
    zpad = jnp.zeros((PAD_LO, D_CONV), F32)
    for i in range(bb):
        pad_s[i, 0:PAD_LO, :] = zpad
        pad_s[i, PAD_LO:PAD_LO + t, :] = u[i * t:(i + 1) * t]
        pad_s[i, PAD_LO + t:2 * PAD_LO + t, :] = zpad

    wdw = wdw_ref[0]
    conv_tiles = [(i, r0) for i in range(bb) for r0 in range(0, t, CONV_ROW_TILE)]

    def conv_some(tiles):
        for i, r0 in tiles:
            acc = jnp.broadcast_to(bdw_ref[0], (CONV_ROW_TILE, D_CONV))
            for j in range(CONV_WIDTH):
                start = PAD_LO - CONV_PAD + r0 + j
                acc = acc + pad_s[i, start:start + CONV_ROW_TILE, :] * wdw[j:j + 1, :]
            mu = jnp.mean(acc, axis=-1, keepdims=True)
            cen = acc - mu
            var = jnp.mean(cen * cen, axis=-1, keepdims=True)
            uf = cen * lax.rsqrt(var + EPS) * lng_ref[0] + lnb_ref[0]
            act = uf * _sigmoid(uf)
            cact_s[i * t + r0:i * t + r0 + CONV_ROW_TILE, :] = act.astype(BF16)

    n_tiles = len(conv_tiles)
    q4 = n_tiles // 4
    xa_s[...] = _dot(hb, w_in(0, O_B)).astype(BF16)
    k_s[...] = (_dot(hb, w_in(O_K, O_V)) * (HEAD_DIM_M ** -0.5)).astype(BF16)
    conv_some(conv_tiles[0:q4])
    so_s[...] = _sigmoid(_dot(hb, w_in(O_O, O_G))).astype(BF16)
    conv_some(conv_tiles[q4:2 * q4])
    qf = _dot(hb, w_in(O_Q, O_K))
    for b in range(n_blk):
        qt_s[b] = qf[b * CHUNK:(b + 1) * CHUNK, :].T.astype(BF16)
    conv_some(conv_tiles[2 * q4:3 * q4])
    vf = _dot(hb, w_in(O_V, O_O))
    ones_rows = jnp.where(lax.broadcasted_iota(jnp.int32, (AUG, CHUNK), 0) == 0, 1.0, 0.0).astype(BF16)
    for b in range(n_blk):
        vt = vf[b * CHUNK:(b + 1) * CHUNK, :].T.astype(BF16)
        for hh in range(N_HEADS_M):
            vta_s[b, hh, 0:HEAD_DIM_M, :] = vt[hh * HEAD_DIM_M:(hh + 1) * HEAD_DIM_M, :]
            vta_s[b, hh, HEAD_DIM_M:HEAD_AUG, :] = ones_rows
    conv_some(conv_tiles[3 * q4:])
    mix_s[:, D_FOURIER:D_FOURIER + D_CONV] = _dot(cact_s[...], wpw_ref[0]).astype(BF16)

    r_i = lax.broadcasted_iota(jnp.int32, (CHUNK, CHUNK), 0)
    c_i = lax.broadcasted_iota(jnp.int32, (CHUNK, CHUNK), 1)
    lower = r_i >= c_i
    upper = r_i <= c_i
    tri_lo = lower.astype(BF16)
    tri_up = upper.astype(BF16)
    neg_inf = jnp.float32(-jnp.inf)
    n_gr = n_blk * GATE_ROWS

    gcol = _dot(hb, w_in(O_G, D_IN_PROJ)) + bg_ref[0]
    kind_c = (lax.broadcasted_iota(jnp.int32, gcol.shape, 1) // N_HEADS_M) % 2
    gcol = jnp.where(kind_c == 1, _log_sigmoid(gcol), gcol)
    grow = _dot_nt(wgt_ref[0], hb) + bgt_ref[0]
    kind_r = (lax.broadcasted_iota(jnp.int32, grow.shape, 0) // N_HEADS_M) % 2
    grow = jnp.where(kind_r == 1, _log_sigmoid(grow), grow)
    for d in range(N_DIRS):
        g0 = d * 2 * N_HEADS_M
        ll = CHUNK - 1 if d == 0 else 0
        xg = jnp.concatenate([grow[g0:g0 + GATE_ROWS, b * CHUNK:(b + 1) * CHUNK] for b in range(n_blk)], axis=0)
        b_rows = pltpu.roll(_split_dot(xg, tri_up if d == 0 else tri_lo), n_gr - N_HEADS_M, axis=0)
        a_rows = xg - b_rows
        a_run = _scan_max_lanes(a_rows, d == 1)
        b_last = jnp.broadcast_to(b_rows[:, ll:ll + 1], (n_gr, CHUNK))
        a_max = jnp.broadcast_to(a_run[:, ll:ll + 1], (n_gr, CHUNK))
        rows_s[d, 0] = b_rows
        rows_s[d, 1] = a_run
        rows_s[d, 2] = b_last
        rows_s[d, 3] = a_max
        rows_s[d, 4] = jnp.exp(a_rows - a_max)
        for b in range(n_blk):
            gc = gcol[b * CHUNK:(b + 1) * CHUNK, :]
            bcol = _split_dot_left(tri_lo if d == 0 else tri_up, gc)
            acol_s[d, b] = gc[:, g0:g0 + N_HEADS_M] - bcol[:, g0 + N_HEADS_M:g0 + 2 * N_HEADS_M]

    for i in range(bb):
        xa_i = xa_s[i * t:(i + 1) * t, :]
        uu = _dot(xa_i, blk_ref[...])
        y = (_dot(dc_ref[...], uu[:, :D_FOURIER].astype(BF16))
             + _dot(ds_ref[...], uu[:, D_FOURIER:].astype(BF16)))
        mix_s[i * t:(i + 1) * t, 0:D_FOURIER] = _dot(y.astype(BF16), wfn_ref[0]).astype(BF16)

    def state_step(it, carry):
        seq = it // nc
        c = it % nc

        @pl.when(c == 0)
        def _():
            if has_init:
                first_row = lax.broadcasted_iota(jnp.int32, (AUG, HEAD_DIM_M), 0) == 0
                for j in range(N_UNITS):
                    cta_s[j, 0:HEAD_DIM_M, :] = c0_ref[seq, 0, j].T
                    cta_s[j, HEAD_DIM_M:HEAD_AUG, :] = jnp.where(first_row, n0_ref[seq, 0, j:j + 1, :], 0.0)
                mst_s[...] = m0_ref[seq, 0]
            else:
                cta_s[...] = jnp.zeros(cta_s.shape, F32)
                mst_s[...] = jnp.zeros(mst_s.shape, F32)

        units = []
        for d in range(N_DIRS):
            blk = seq * nc + (c if d == 0 else nc - 1 - c)
            row0 = pl.multiple_of(blk * CHUNK, CHUNK)
            gr0 = pl.multiple_of(blk * GATE_ROWS, GATE_ROWS)
            b_last = rows_s[d, 2, pl.ds(gr0, GATE_ROWS), :]
            a_max = rows_s[d, 3, pl.ds(gr0, GATE_ROWS), :]
            w_rows = rows_s[d, 4, pl.ds(gr0, GATE_ROWS), :]
            m_prev = mst_s[d]
            mprev_s[d, blk] = m_prev
            m_new = b_last + jnp.maximum(m_prev, a_max)
            mst_s[d] = m_new
            decay = jnp.exp(b_last + m_prev - m_new)
            fac = jnp.exp(a_max + b_last - m_new)
            for hh in range(N_HEADS_M):
                hs = slice(hh * HEAD_DIM_M, (hh + 1) * HEAD_DIM_M)
                units.append(dict(d=d, hh=hh, j=d * N_HEADS_M + hh, blk=blk,
                                  kc=k_s[pl.ds(row0, CHUNK), hs], vta=vta_s[blk, hh],
                                  w=w_rows[hh:hh + 1, :], decay=decay[hh:hh + 1, :], fac=fac[hh:hh + 1, :]))
        for un in units:
            vw = (un["vta"].astype(F32) * un["w"]).astype(BF16)
            un["upd"] = _dot(vw, un["kc"])
        for un in units:
            cta = cta_s[un["j"]]
            ctab_s[un["d"], un["blk"], un["hh"]] = cta.astype(BF16)
            cta_s[un["j"]] = un["decay"] * cta + un["fac"] * un["upd"]

        if emit_state:
            @pl.when(c == nc - 1)
            def _():
                for j in range(N_UNITS):
                    co_ref[seq, j] = cta_s[j, 0:HEAD_DIM_M, :].T
                    no_ref[seq, j:j + 1, :] = cta_s[j, HEAD_DIM_M:HEAD_DIM_M + 1, :]
                mo_ref[seq] = mst_s[...]
        return carry

    lax.fori_loop(0, n_blk, state_step, 0)

    def chunk_out(blk, carry):
        row0 = pl.multiple_of(blk * CHUNK, CHUNK)
        gr0 = pl.multiple_of(blk * GATE_ROWS, GATE_ROWS)
        dirs = []
        for d in range(N_DIRS):
            m_prev = mprev_s[d, blk]
            mm = jnp.maximum(m_prev, rows_s[d, 1, pl.ds(gr0, GATE_ROWS), :])
            floor = jnp.exp(-(rows_s[d, 0, pl.ds(gr0, GATE_ROWS), :] + mm))
            dirs.append(dict(mm=mm, inter=jnp.exp(m_prev - mm), floor=floor,
                             acol=acol_s[d, blk], mask=upper if d == 0 else lower))
        heads = []
        for hh in range(N_HEADS_M):
            hs = slice(hh * HEAD_DIM_M, (hh + 1) * HEAD_DIM_M)
            hd = dict(hh=hh, hs=hs, kc=k_s[pl.ds(row0, CHUNK), hs], qt=qt_s[blk, hs, :], vta=vta_s[blk, hh])
            hd["st"] = _dot(hd["kc"], hd["qt"])
            hd["p1"] = [_dot(ctab_s[d, blk, hh], hd["qt"]) for d in range(N_DIRS)]
            heads.append(hd)
        for hd in heads:
            hh = hd["hh"]
            sm = []
            for dd in dirs:
                z = dd["acol"][:, hh:hh + 1] - dd["mm"][hh:hh + 1, :]
                sm.append((hd["st"] * jnp.exp(jnp.where(dd["mask"], z, neg_inf))).astype(BF16))
            hd["p2"] = _dot(hd["vta"], jnp.concatenate(sm, axis=1))
        parts = []
        for hd in heads:
            hh = hd["hh"]
            hsum = None
            for d, dd in enumerate(dirs):
                numa = dd["inter"][hh:hh + 1, :] * hd["p1"][d] + hd["p2"][:, d * CHUNK:(d + 1) * CHUNK]
                den = numa[HEAD_DIM_M:HEAD_DIM_M + 1, :]
                ht = numa[0:HEAD_DIM_M, :] / jnp.maximum(jnp.abs(den), dd["floor"][hh:hh + 1, :])
                hsum = ht if hsum is None else hsum + ht
            r = lax.rsqrt(jnp.mean(hsum * hsum, axis=0, keepdims=True) + EPS)
            parts.append((hsum * r * gmh_ref[0, hd["hs"], :]).T)
        oc = jnp.concatenate(parts, axis=1) * so_s[pl.ds(row0, CHUNK), :].astype(F32)
        mix_s[pl.ds(row0, CHUNK), D_FOURIER + D_CONV:D_MIX] = oc.astype(BF16)
        return carry

    lax.fori_loop(0, n_blk, chunk_out, 0)

    ga1 = mod_ref[0][:, 2 * D_MODEL:3 * D_MODEL]
    res = _dot(mix_s[...], wout_ref[0])
    x1_ref[...] = (load_x() + ga1 * res).reshape(bb, t, D_MODEL)


def _whole(a):
    nd = a.ndim
    return pl.BlockSpec(a.shape, lambda g: (0,) * nd, pipeline_mode=pl.Buffered(1))


def _layer_block(a, l):
    nd = a.ndim
    return pl.BlockSpec((1,) + a.shape[1:], lambda g: (l,) + (0,) * (nd - 1), pipeline_mode=pl.Buffered(1))


def _mixer_call(x, pos, mod_rows, mod_base, per_batch_mod, l, pw, state0, emit_state):
    nb, t, _ = x.shape
    bb = ROWS // t
    assert bb * t == ROWS and nb % bb == 0 and t % CHUNK == 0
    rows = ROWS
    n_blk = rows // CHUNK
    has_init = state0 is not None
    add_pos = pos is not None

    args = [x]
    in_specs = [pl.BlockSpec((bb, t, D_MODEL), lambda g: (g, 0, 0))]
    if add_pos:
        args.append(pos)
        in_specs.append(_whole(pos))
    args.append(mod_rows)
    if per_batch_mod:
        assert bb == 1
        in_specs.append(pl.BlockSpec((1, 1, N_MOD * D_MODEL), lambda g: (mod_base + g, 0, 0)))
    else:
        in_specs.append(pl.BlockSpec((1, 1, N_MOD * D_MODEL), lambda g: (mod_base, 0, 0)))
    for name in ("g1", "win", "wgt", "bg", "bgt"):
        args.append(pw[name])
        in_specs.append(_layer_block(pw[name], l))
    for a in (pw["blk"], pw["dc"][t], pw["ds"][t]):
        args.append(a)
        in_specs.append(_whole(a))
    for name in ("wfn", "wdw", "bdw", "lng", "lnb", "wpw", "gmh"):
        args.append(pw[name])
        in_specs.append(_layer_block(pw[name], l))
    if has_init:
        c0, n0, m0 = state0
        args += [c0, n0, m0]
        in_specs += [
            pl.BlockSpec((bb, 1, N_UNITS, HEAD_DIM_M, HEAD_DIM_M), lambda g: (g, l, 0, 0, 0)),
            pl.BlockSpec((bb, 1, N_UNITS, HEAD_DIM_M), lambda g: (g, l, 0, 0)),
            pl.BlockSpec((bb, 1, N_DIRS, GATE_ROWS, CHUNK), lambda g: (g, l, 0, 0, 0)),
        ]
    args.append(pw["wout"])
    in_specs.append(_layer_block(pw["wout"], l))

    out_shape = [jax.ShapeDtypeStruct((nb, t, D_MODEL), F32)]
    out_specs = [pl.BlockSpec((bb, t, D_MODEL), lambda g: (g, 0, 0))]
    if emit_state:
        out_shape += [
            jax.ShapeDtypeStruct((nb, N_UNITS, HEAD_DIM_M, HEAD_DIM_M), F32),
            jax.ShapeDtypeStruct((nb, N_UNITS, HEAD_DIM_M), F32),
            jax.ShapeDtypeStruct((nb, N_DIRS, GATE_ROWS, CHUNK), F32),
        ]
        out_specs += [
            pl.BlockSpec((bb, N_UNITS, HEAD_DIM_M, HEAD_DIM_M), lambda g: (g, 0, 0, 0)),
            pl.BlockSpec((bb, N_UNITS, HEAD_DIM_M), lambda g: (g, 0, 0)),
            pl.BlockSpec((bb, N_DIRS, GATE_ROWS, CHUNK), lambda g: (g, 0, 0, 0)),
        ]
    scratch = [
        pltpu.VMEM((rows, D_FOURIER), BF16),
        pltpu.VMEM((bb, t + 2 * PAD_LO, D_CONV), F32),
        pltpu.VMEM((rows, D_CONV), BF16),
        pltpu.VMEM((n_blk, D_MLSTM, CHUNK), BF16),
        pltpu.VMEM((rows, D_MLSTM), BF16),
        pltpu.VMEM((n_blk, N_HEADS_M, HEAD_AUG, CHUNK), BF16),
        pltpu.VMEM((rows, D_MLSTM), BF16),
        pltpu.VMEM((N_DIRS, 5, n_blk * GATE_ROWS, CHUNK), F32),
        pltpu.VMEM((N_DIRS, n_blk, CHUNK, N_HEADS_M), F32),
        pltpu.VMEM((N_DIRS, n_blk, GATE_ROWS, CHUNK), F32),
        pltpu.VMEM((N_DIRS, n_blk, N_HEADS_M, HEAD_AUG, HEAD_DIM_M), BF16),
        pltpu.VMEM((rows, D_MIX), BF16),
        pltpu.VMEM((N_UNITS, HEAD_AUG, HEAD_DIM_M), F32),
        pltpu.VMEM((N_DIRS, GATE_ROWS, CHUNK), F32),
    ]
    body = functools.partial(_mixer_body, bb, t, has_init, emit_state, add_pos)
    return pl.pallas_call(
        body,
        grid=(nb // bb,),
        in_specs=in_specs,
        out_specs=out_specs,
        out_shape=out_shape,
        scratch_shapes=scratch,
        compiler_params=pltpu.CompilerParams(dimension_semantics=("arbitrary",),
                                             vmem_limit_bytes=VMEM_LIMIT),
        name="mixer_t%d" % t,
    )(*args)


def _route_rows(logits_t, b_router_col):
    scores = _sigmoid(logits_t)
    sel = scores + b_router_col
    sel_r = [sel[e:e + 1, :] for e in range(N_EXPERTS)]
    sc_r = [scores[e:e + 1, :] for e in range(N_EXPERTS)]

    best = None
    best_v = None
    for g in range(N_EXPERT_GROUPS):
        a, b, c, d = sel_r[g * EXPERTS_PER_GROUP:(g + 1) * EXPERTS_PER_GROUP]
        hi1, lo1 = jnp.maximum(a, b), jnp.minimum(a, b)
        hi2, lo2 = jnp.maximum(c, d), jnp.minimum(c, d)
        gs = jnp.maximum(hi1, hi2) + jnp.maximum(jnp.minimum(hi1, hi2), jnp.maximum(lo1, lo2))
        if g == 0:
            best = jnp.zeros(gs.shape, jnp.int32)
            best_v = gs
        else:
            upd = gs > best_v
            best = jnp.where(upd, g, best)
            best_v = jnp.where(upd, gs, best_v)

    def pick(rows, j):
        out = rows[(N_EXPERT_GROUPS - 1) * EXPERTS_PER_GROUP + j]
        for g in range(N_EXPERT_GROUPS - 2, -1, -1):
            out = jnp.where(best == g, rows[g * EXPERTS_PER_GROUP + j], out)
        return out

    s = [pick(sel_r, j) for j in range(EXPERTS_PER_GROUP)]
    sc = [pick(sc_r, j) for j in range(EXPERTS_PER_GROUP)]
    rank = [jnp.zeros(best.shape, jnp.int32) for _ in range(EXPERTS_PER_GROUP)]
    for a in range(EXPERTS_PER_GROUP):
        for b in range(a + 1, EXPERTS_PER_GROUP):
            b_first = s[b] > s[a]
            rank[a] = rank[a] + b_first.astype(jnp.int32)
            rank[b] = rank[b] + (1 - b_first.astype(jnp.int32))
    w = [jnp.where(rank[j] < 2, sc[j], 0.0) for j in range(EXPERTS_PER_GROUP)]
    tot = w[0] + w[1] + w[2] + w[3]
    return best, [wj / tot for wj in w]


def _moe_body(is_last, x_ref, mod_ref, g2_ref, wrt_ref, brc_ref, utri_ref, wg_ref, wu_ref, wd_ref, fg_ref,
              o_ref, he_s, p_s, pt_s, ys_s, sm_s):
    gi = pl.program_id(1)
    tm = ROWS
    n_cap_blk = MOE_CAP // MOE_BLK

    @pl.when(gi == 0)
    def _():
        mod = mod_ref[0]
        sh2 = mod[:, 3 * D_MODEL:4 * D_MODEL]
        sc2 = mod[:, 4 * D_MODEL:5 * D_MODEL]
        h = _rmsnorm_rows(x_ref[...], g2_ref[0]) * (1.0 + sc2) + sh2
        h_hi, h_lo = _hi_lo(h)
        he_s[:, 0:D_MODEL] = h_hi
        w_hi, w_lo = _hi_lo(wrt_ref[...])
        logits_t = _dot_nt(w_hi, h_hi) + _dot_nt(w_hi, h_lo) + _dot_nt(w_lo, h_hi)
        best, cw = _route_rows(logits_t, brc_ref[...])

        row_i = lax.broadcasted_iota(jnp.int32, (16, tm), 0)
        onehot = (row_i == best).astype(F32)
        before = _dot(onehot.astype(BF16), utri_ref[...])
        dest = jnp.zeros((1, tm), F32)
        off_blk = jnp.int32(0)
        for g in range(N_EXPERT_GROUPS):
            cnt = jnp.sum(onehot[g:g + 1, :]).astype(jnp.int32)
            n_blk = lax.shift_right_logical(cnt + (MOE_BLK - 1), int(math.log2(MOE_BLK)))
            sm_s[g] = off_blk
            sm_s[N_EXPERT_GROUPS + g] = n_blk
            base = (off_blk * MOE_BLK).astype(F32)
            dest = dest + onehot[g:g + 1, :] * (before[g:g + 1, :] + base)
            off_blk = off_blk + n_blk

        cw_hi = [wj.astype(BF16).astype(F32) for wj in cw]
        cw_lo = [wj - hj for wj, hj in zip(cw, cw_hi)]
        stack = jnp.concatenate([dest] + cw_hi + cw_lo + [jnp.zeros((MOE_EXT - 9, tm), F32)], axis=0)
        cols = stack.T
        he_s[:, D_MODEL:D_MODEL + MOE_EXT] = cols.astype(BF16)
        dest_col = cols[:, 0:1]

        r_iota = lax.broadcasted_iota(jnp.int32, (MOE_BLK, 1), 0)
        for b in range(n_cap_blk):
            rows_f = (r_iota + b * MOE_BLK).astype(F32)
            p_s[b * MOE_BLK:(b + 1) * MOE_BLK, :] = jnp.where(dest == rows_f, 1.0, 0.0).astype(BF16)
        c_iota = lax.broadcasted_iota(jnp.int32, (1, MOE_CAP), 1).astype(F32)
        pt_s[...] = jnp.where(dest_col == c_iota, 1.0, 0.0).astype(BF16)
        ys_s[...] = jnp.zeros(ys_s.shape, BF16)

    first_blk = sm_s[gi]
    n_blk = sm_s[N_EXPERT_GROUPS + gi]

    def block(i, carry):
        r0 = pl.multiple_of((first_blk + i) * MOE_BLK, MOE_BLK)
        xe = _dot(p_s[pl.ds(r0, MOE_BLK), :], he_s[...])
        xs = xe[:, 0:D_MODEL].astype(BF16)
        acts = []
        for j in range(EXPERTS_PER_GROUP):
            gate = _dot(xs, wg_ref[0, j])
            up = _dot(xs, wu_ref[0, j])
            cwj = (xe[:, D_MODEL + 1 + j:D_MODEL + 2 + j]
                   + xe[:, D_MODEL + 1 + EXPERTS_PER_GROUP + j:D_MODEL + 2 + EXPERTS_PER_GROUP + j])
            acts.append((gate * _sigmoid(gate) * up * cwj).astype(BF16))
        ys_s[pl.ds(r0, MOE_BLK), :] = _dot(jnp.concatenate(acts, axis=1), wd_ref[0, 0]).astype(BF16)
        return carry

    lax.fori_loop(0, n_blk, block, 0)

    @pl.when(gi == N_EXPERT_GROUPS - 1)
    def _():
        ga2 = mod_ref[0][:, 5 * D_MODEL:6 * D_MODEL]
        xo = x_ref[...] + ga2 * _dot(pt_s[...], ys_s[...])
        if is_last:
            xo = _rmsnorm_rows(xo, fg_ref[...])
        o_ref[...] = xo


def _moe_call(x2d, mod_rows, mod_base, per_tile_mod, l, pw, is_last):
    n_tok = x2d.shape[0]
    tm = ROWS
    assert n_tok % tm == 0
    if per_tile_mod:
        mod_spec = pl.BlockSpec((1, 1, N_MOD * D_MODEL), lambda i, g: (mod_base + i, 0, 0))
    else:
        mod_spec = pl.BlockSpec((1, 1, N_MOD * D_MODEL), lambda i, g: (mod_base, 0, 0))

    def const(shape):
        return pl.BlockSpec(shape, lambda i, g: (0,) * len(shape), pipeline_mode=pl.Buffered(1))

    def grp(shape):
        return pl.BlockSpec((1, EXPERTS_PER_GROUP) + shape, lambda i, g: (l, g, 0, 0))

    return pl.pallas_call(
        functools.partial(_moe_body, is_last),
        grid=(n_tok // tm, N_EXPERT_GROUPS),
        in_specs=[
            pl.BlockSpec((tm, D_MODEL), lambda i, g: (i, 0)),
            mod_spec,
            pl.BlockSpec((1, 1, D_MODEL), lambda i, g: (l, 0, 0), pipeline_mode=pl.Buffered(1)),
            const((N_EXPERTS, D_MODEL)),
            const((N_EXPERTS, 1)),
            const((tm, tm)),
            grp((D_MODEL, D_EXPERT)),
            grp((D_MODEL, D_EXPERT)),
            pl.BlockSpec((1, 1, EXPERTS_PER_GROUP * D_EXPERT, D_MODEL), lambda i, g: (l, g, 0, 0)),
            const((1, D_MODEL)),
        ],
        out_specs=pl.BlockSpec((tm, D_MODEL), lambda i, g: (i, 0)),
        out_shape=jax.ShapeDtypeStruct((n_tok, D_MODEL), F32),
        scratch_shapes=[
            pltpu.VMEM((tm, D_MODEL + MOE_EXT), BF16),
            pltpu.VMEM((MOE_CAP, tm), BF16),
            pltpu.VMEM((tm, MOE_CAP), BF16),
            pltpu.VMEM((MOE_CAP, D_MODEL), BF16),
            pltpu.SMEM((2 * N_EXPERT_GROUPS,), jnp.int32),
        ],
        compiler_params=pltpu.CompilerParams(dimension_semantics=("arbitrary", "arbitrary"),
                                             vmem_limit_bytes=VMEM_LIMIT),
        name="moe",
    )(x2d, mod_rows, pw["g2"], pw["wrt"], pw["brc"], pw["utri"], pw["weg"], pw["weu"], pw["wed"], pw["fg"])


def _dft_tables(t):
    idx = np.arange(t, dtype=np.int64)
    ang = 2.0 * np.pi * ((idx[:, None] * idx[None, :]) % t).astype(np.float64) / t
    scale = 1.0 / math.sqrt(t * D_FOURIER_GROUP)
    return (np.cos(ang) * scale).astype(np.float32), (-np.sin(ang) * scale).astype(np.float32)


def _group_tables():
    idx = np.arange(D_FOURIER_GROUP, dtype=np.int64)
    ang = 2.0 * np.pi * ((idx[:, None] * idx[None, :]) % D_FOURIER_GROUP).astype(np.float64) / D_FOURIER_GROUP
    eye = np.eye(N_FOURIER_GROUPS)
    return np.concatenate([np.kron(eye, np.cos(ang)), np.kron(eye, np.sin(ang))], axis=1).astype(np.float32)


def _grid_pos_embed(n_tokens, dtype):
    rows = n_tokens // GRID_W
    r = jnp.repeat(jnp.arange(rows, dtype=jnp.float32), GRID_W)
    col = jnp.tile(jnp.arange(GRID_W, dtype=jnp.float32), rows)
    quarter = D_MODEL // 4
    omega = 1.0 / (10000.0 ** (jnp.arange(quarter, dtype=jnp.float32) / quarter))

    def enc(p):
        a = p[:, None] * omega[None, :]
        return jnp.concatenate([jnp.sin(a), jnp.cos(a)], axis=-1)

    return jnp.concatenate([enc(r), enc(col)], axis=-1).astype(dtype)


def _prepare(seq_lens, norm1_g, norm2_g, w_in, w_fnet, w_dw, b_dw, conv_ln_g, conv_ln_b, w_pw, b_gate, g_mh,
             w_out, w_router, b_router, w_exp_gate, w_exp_up, w_exp_down, final_g):
    dft = {t: _dft_tables(t) for t in sorted(set(seq_lens))}
    w_in_b = w_in.astype(BF16)
    tok = np.arange(ROWS)
    return dict(
        g1=norm1_g.reshape(DEPTH, 1, D_MODEL), g2=norm2_g.reshape(DEPTH, 1, D_MODEL),
        win=w_in_b, wgt=jnp.swapaxes(w_in_b[:, :, O_G:], 1, 2),
        bg=b_gate.reshape(DEPTH, 1, N_GATES), bgt=b_gate.reshape(DEPTH, N_GATES, 1),
        blk=jnp.asarray(_group_tables()).astype(BF16),
        dc={t: jnp.asarray(v[0]).astype(BF16) for t, v in dft.items()},
        ds={t: jnp.asarray(v[1]).astype(BF16) for t, v in dft.items()},
        wfn=w_fnet.astype(BF16),
        wdw=jnp.concatenate([w_dw, jnp.zeros((DEPTH, 1, D_CONV), F32)], axis=1),
        bdw=b_dw.reshape(DEPTH, 1, D_CONV), lng=conv_ln_g.reshape(DEPTH, 1, D_CONV),
        lnb=conv_ln_b.reshape(DEPTH, 1, D_CONV), wpw=w_pw.astype(BF16),
        gmh=jnp.broadcast_to(g_mh.reshape(DEPTH, D_MLSTM, 1), (DEPTH, D_MLSTM, CHUNK)),
        wout=w_out.astype(BF16),
        wrt=w_router.T, brc=b_router.reshape(N_EXPERTS, 1),
        utri=jnp.asarray((tok[:, None] < tok[None, :]).astype(np.float32)).astype(BF16),
        weg=w_exp_gate.astype(BF16), weu=w_exp_up.astype(BF16),
        wed=w_exp_down.reshape(DEPTH, N_EXPERT_GROUPS, EXPERTS_PER_GROUP * D_EXPERT, D_MODEL).astype(BF16),
        fg=final_g.reshape(1, D_MODEL),
    )


def kernel(x_prompt, x_sample, state_C, state_n, state_m, c, c_ctx, w_ada, b_ada, norm1_g, norm2_g, w_in, w_fnet, w_dw, b_dw, conv_ln_g, conv_ln_b, w_pw, b_gate, g_mh, w_out, w_router, b_router, w_exp_gate, w_exp_up, w_exp_down, final_g):
    bp, tp, _ = x_prompt.shape
    bs, ts, _ = x_sample.shape

    cv = jnp.zeros((MOD_ROWS, D_MODEL), F32).at[:bs].set(c).at[bs].set(c_ctx)
    mod_all = _ada_call(cv, w_ada, b_ada)
    mod_rows = mod_all.reshape(DEPTH * MOD_ROWS, 1, N_MOD * D_MODEL)

    pw = _prepare((tp, ts), norm1_g, norm2_g, w_in, w_fnet, w_dw, b_dw, conv_ln_g, conv_ln_b, w_pw, b_gate,
                  g_mh, w_out, w_router, b_router, w_exp_gate, w_exp_up, w_exp_down, final_g)
    pos = _grid_pos_embed(ts, x_sample.dtype)

    c0 = state_C.reshape(bs, DEPTH, N_UNITS, HEAD_DIM_M, HEAD_DIM_M)
    n0 = state_n.reshape(bs, DEPTH, N_UNITS, HEAD_DIM_M)
    m0 = jnp.pad(state_m, ((0, 0), (0, 0), (0, 0), (0, GATE_ROWS - N_HEADS_M)))
    m0 = jnp.broadcast_to(m0[..., None], (bs, DEPTH, N_DIRS, GATE_ROWS, CHUNK))

    xp, xs = x_prompt, x_sample
    cs, ns, ms = [], [], []
    for l in range(DEPTH):
        is_last = l == DEPTH - 1
        base = l * MOD_ROWS
        xp, c_l, n_l, m_l = _mixer_call(xp, None, mod_rows, base + bs, False, l, pw, None, True)
        cs.append(c_l)
        ns.append(n_l)
        ms.append(m_l[:, :, :N_HEADS_M, 0])
        xp = _moe_call(xp.reshape(bp * tp, D_MODEL), mod_rows, base + bs, False, l, pw, is_last)
        xp = xp.reshape(bp, tp, D_MODEL)

        (xs,) = _mixer_call(xs, pos if l == 0 else None, mod_rows, base, True, l, pw, (c0, n0, m0), False)
        xs = _moe_call(xs.reshape(bs * ts, D_MODEL), mod_rows, base, True, l, pw, is_last)
        xs = xs.reshape(bs, ts, D_MODEL)

    new_c = jnp.stack(cs, axis=1).reshape(bp, DEPTH, N_DIRS, N_HEADS_M, HEAD_DIM_M, HEAD_DIM_M)
    new_n = jnp.stack(ns, axis=1).reshape(bp, DEPTH, N_DIRS, N_HEADS_M, HEAD_DIM_M)
    new_m = jnp.stack(ms, axis=1).reshape(bp, DEPTH, N_DIRS, N_HEADS_M)
    return (xp, xs, new_c, new_n, new_m)
```

```python
import functools
import math

import numpy as np
import jax
import jax.numpy as jnp
from jax import lax
from jax.experimental import pallas as pl
from jax.experimental.pallas import tpu as pltpu

D_MODEL = 1024
DEPTH = 4
GRID_W = 64
EPS = 1e-6
D_FOURIER = 256
N_FOURIER_GROUPS = 4
D_FOURIER_GROUP = D_FOURIER // N_FOURIER_GROUPS
D_CONV = 256
CONV_WIDTH = 31
CONV_PAD = CONV_WIDTH // 2
D_MLSTM = 512
N_HEADS_M = 4
HEAD_DIM_M = D_MLSTM // N_HEADS_M
N_DIRS = 2
CHUNK = 128
N_GATES = 2 * N_DIRS * N_HEADS_M
N_UNITS = N_DIRS * N_HEADS_M
D_MIX = D_FOURIER + D_CONV + D_MLSTM
D_IN_PROJ = D_FOURIER + 2 * D_CONV + 4 * D_MLSTM + N_GATES
N_EXPERTS = 16
N_EXPERT_GROUPS = 4
EXPERTS_PER_GROUP = N_EXPERTS // N_EXPERT_GROUPS
D_EXPERT = 256
N_MOD = 6

O_B = D_FOURIER
O_Q = O_B + 2 * D_CONV
O_K = O_Q + D_MLSTM
O_V = O_K + D_MLSTM
O_O = O_V + D_MLSTM
O_G = O_O + D_MLSTM

ROWS = 1024
CONV_ROW_TILE = 64
PAD_LO = 16
MOD_ROWS = 16
SUBLANES = 8
SHIFT_ROWS = 24
AUG = 16
HEAD_AUG = HEAD_DIM_M + AUG
GATE_ROWS = 8
MOE_BLK = 128
MOE_CAP = ROWS + N_EXPERT_GROUPS * MOE_BLK
MOE_EXT = 128
VMEM_LIMIT = 56 * 1024 * 1024

F32 = jnp.float32
BF16 = jnp.bfloat16


def _dot(a, b):
    return jnp.dot(a, b, preferred_element_type=F32)


def _dot_nt(a, b):
    return lax.dot_general(a, b, (((1,), (1,)), ((), ())), preferred_element_type=F32)


def _sigmoid(x):
    return 1.0 / (1.0 + jnp.exp(-x))


def _log_sigmoid(x):
    return jnp.minimum(x, 0.0) - jnp.log(1.0 + jnp.exp(-jnp.abs(x)))


def _hi_lo(a_f32):
    hi = a_f32.astype(BF16)
    return hi, (a_f32 - hi.astype(F32)).astype(BF16)


def _split_dot(a_f32, b_bf16):
    hi, lo = _hi_lo(a_f32)
    return _dot(hi, b_bf16) + _dot(lo, b_bf16)


def _split_dot_left(b_bf16, a_f32):
    hi, lo = _hi_lo(a_f32)
    return _dot(b_bf16, hi) + _dot(b_bf16, lo)


def _rmsnorm_rows(x, g):
    ms = jnp.mean(x * x, axis=-1, keepdims=True)
    return x * lax.rsqrt(ms + EPS) * g


def _ada_body(c_ref, w_ref, b_ref, o_ref):
    cv = c_ref[...]
    s = cv * _sigmoid(cv)
    o_ref[0] = jnp.dot(s, w_ref[0], preferred_element_type=F32,
                       precision=lax.Precision.HIGHEST) + b_ref[0]


def _ada_call(cv, w_ada, b_ada):
    n_col = N_MOD * D_MODEL
    tn = D_MODEL
    return pl.pallas_call(
        _ada_body,
        grid=(DEPTH, n_col // tn),
        in_specs=[
            pl.BlockSpec((MOD_ROWS, D_MODEL), lambda l, j: (0, 0)),
            pl.BlockSpec((1, D_MODEL, tn), lambda l, j: (l, 0, j)),
            pl.BlockSpec((1, 1, tn), lambda l, j: (l, 0, j)),
        ],
        out_specs=pl.BlockSpec((1, MOD_ROWS, tn), lambda l, j: (l, 0, j)),
        out_shape=jax.ShapeDtypeStruct((DEPTH, MOD_ROWS, n_col), F32),
        compiler_params=pltpu.CompilerParams(dimension_semantics=("arbitrary", "arbitrary")),
        name="ada",
    )(cv, w_ada, b_ada.reshape(DEPTH, 1, n_col))


def _scan_max_lanes(x, reverse):
    lane = lax.broadcasted_iota(jnp.int32, x.shape, 1)
    neg_inf = jnp.float32(-jnp.inf)
    sh = 1
    while sh < CHUNK:
        if reverse:
            y = pltpu.roll(x, CHUNK - sh, axis=1)
            x = jnp.maximum(x, jnp.where(lane < CHUNK - sh, y, neg_inf))
        else:
            y = pltpu.roll(x, sh, axis=1)
            x = jnp.maximum(x, jnp.where(lane >= sh, y, neg_inf))
        sh *= 2
    return x


def _mixer_body(bb, t, has_init, emit_state, add_pos, *refs):
    rows = bb * t
    nc = t // CHUNK
    n_blk = rows // CHUNK
    refs = list(refs)
    x_ref = refs.pop(0)
    pos_ref = refs.pop(0) if add_pos else None
    (mod_ref, g1_ref, win_ref, wgt_ref, bg_ref, bgt_ref, blk_ref, dc_ref, ds_ref, wfn_ref,
     wdw_ref, bdw_ref, lng_ref, lnb_ref, wpw_ref, gmh_ref) = refs[:16]
    refs = refs[16:]
    if has_init:
        c0_ref, n0_ref, m0_ref = refs[:3]
        refs = refs[3:]
    wout_ref = refs.pop(0)
    x1_ref = refs.pop(0)
    if emit_state:
        co_ref, no_ref, mo_ref = refs[:3]
        refs = refs[3:]
    (xa_s, pad_s, shf_s, cact_s, qt_s, k_s, vta_s, so_s, rows_s, acol_s, mprev_s, ctab_s,
     mix_s, cta_s, mst_s) = refs

    def load_x():
        xv = x_ref[...].reshape(rows, D_MODEL)
        if add_pos:
            xv = xv + pos_ref[...]
        return xv

    def w_in(lo, hi):
        return win_ref[0, :, lo:hi]

    x = load_x()
    mod = mod_ref[0]
    sh1 = mod[:, 0:D_MODEL]
    sc1 = mod[:, D_MODEL:2 * D_MODEL]
    h = _rmsnorm_rows(x, g1_ref[0]) * (1.0 + sc1) + sh1
    hb = h.astype(BF16)

    pb = _dot(hb, w_in(O_B, O_Q))
    u = pb[:, :D_CONV] * _sigmoid(pb[:, D_CONV:])
    zpad = jnp.zeros((PAD_LO, D_CONV), F32)
    for i in range(bb):
        pad_s[i, 0:PAD_LO, :] = zpad
        pad_s[i, PAD_LO:PAD_LO + t, :] = u[i * t:(i + 1) * t]
        pad_s[i, PAD_LO + t:2 * PAD_LO + t, :] = zpad

    for i in range(bb):
        for s8 in range(SUBLANES):
            shf_s[s8, i] = pad_s[i, s8:s8 + t + SHIFT_ROWS, :]

    def conv_tile(seq, r0):
        acc = jnp.broadcast_to(bdw_ref[0], (CONV_ROW_TILE, D_CONV))
        for j in range(CONV_WIDTH):
            q = j + PAD_LO - CONV_PAD
            win = shf_s[q % SUBLANES, seq, pl.ds(pl.multiple_of(r0 + (q // SUBLANES) * SUBLANES, SUBLANES),
                                                 CONV_ROW_TILE), :]
            acc = acc + win * wdw_ref[0, j:j + 1, :]
        mu = jnp.mean(acc, axis=-1, keepdims=True)
        cen = acc - mu
        var = jnp.mean(cen * cen, axis=-1, keepdims=True)
        uf = cen * lax.rsqrt(var + EPS) * lng_ref[0] + lnb_ref[0]
        return (uf * _sigmoid(uf)).astype(BF16)

    xa_s[...] = _dot(hb, w_in(0, O_B)).astype(BF16)
    k_s[...] = (_dot(hb, w_in(O_K, O_V)) * (HEAD_DIM_M ** -0.5)).astype(BF16)
    so_s[...] = _sigmoid(_dot(hb, w_in(O_O, O_G))).astype(BF16)
    qf = _dot(hb, w_in(O_Q, O_K))
    for b in range(n_blk):
        qt_s[b] = qf[b * CHUNK:(b + 1) * CHUNK, :].T.astype(BF16)
    vf = _dot(hb, w_in(O_V, O_O))
    ones_rows = jnp.where(lax.broadcasted_iota(jnp.int32, (AUG, CHUNK), 0) == 0, 1.0, 0.0).astype(BF16)
    for b in range(n_blk):
        vt = vf[b * CHUNK:(b + 1) * CHUNK, :].T.astype(BF16)
        for hh in range(N_HEADS_M):
            vta_s[b, hh, 0:HEAD_DIM_M, :] = vt[hh * HEAD_DIM_M:(hh + 1) * HEAD_DIM_M, :]
            vta_s[b, hh, HEAD_DIM_M:HEAD_AUG, :] = ones_rows

    r_i = lax.broadcasted_iota(jnp.int32, (CHUNK, CHUNK), 0)
    c_i = lax.broadcasted_iota(jnp.int32, (CHUNK, CHUNK), 1)
    lower = r_i >= c_i
    upper = r_i <= c_i
    tri_lo = lower.astype(BF16)
    tri_up = upper.astype(BF16)
    neg_inf = jnp.float32(-jnp.inf)
    n_gr = n_blk * GATE_ROWS

    gcol = _dot(hb, w_in(O_G, D_IN_PROJ)) + bg_ref[0]
    kind_c = (lax.broadcasted_iota(jnp.int32, gcol.shape, 1) // N_HEADS_M) % 2
    gcol = jnp.where(kind_c == 1, _log_sigmoid(gcol), gcol)
    grow = _dot_nt(wgt_ref[0], hb) + bgt_ref[0]
    kind_r = (lax.broadcasted_iota(jnp.int32, grow.shape, 0) // N_HEADS_M) % 2
    grow = jnp.where(kind_r == 1, _log_sigmoid(grow), grow)
    for d in range(N_DIRS):
        g0 = d * 2 * N_HEADS_M
        ll = CHUNK - 1 if d == 0 else 0
        xg = jnp.concatenate([grow[g0:g0 + GATE_ROWS, b * CHUNK:(b + 1) * CHUNK] for b in range(n_blk)], axis=0)
        b_rows = pltpu.roll(_split_dot(xg, tri_up if d == 0 else tri_lo), n_gr - N_HEADS_M, axis=0)
        a_rows = xg - b_rows
        a_run = _scan_max_lanes(a_rows, d == 1)
        b_last = jnp.broadcast_to(b_rows[:, ll:ll + 1], (n_gr, CHUNK))
        a_max = jnp.broadcast_to(a_run[:, ll:ll + 1], (n_gr, CHUNK))
        rows_s[d, 0] = b_rows
        rows_s[d, 1] = a_run
        rows_s[d, 2] = b_last
        rows_s[d, 3] = a_max
        rows_s[d, 4] = jnp.exp(a_rows - a_max)
        for b in range(n_blk):
            gc = gcol[b * CHUNK:(b + 1) * CHUNK, :]
            bcol = _split_dot_left(tri_lo if d == 0 else tri_up, gc)
            acol_s[d, b] = gc[:, g0:g0 + N_HEADS_M] - bcol[:, g0 + N_HEADS_M:g0 + 2 * N_HEADS_M]

    for i in range(bb):
        xa_i = xa_s[i * t:(i + 1) * t, :]
        uu = _dot(xa_i, blk_ref[...])
        y = (_dot(dc_ref[...], uu[:, :D_FOURIER].astype(BF16))
             + _dot(ds_ref[...], uu[:, D_FOURIER:].astype(BF16)))
        mix_s[i * t:(i + 1) * t, 0:D_FOURIER] = _dot(y.astype(BF16), wfn_ref[0]).astype(BF16)

    def state_step(it, carry):
        seq = it // nc
        c = it % nc

        @pl.when(c == 0)
        def _():
            if has_init:
                first_row = lax.broadcasted_iota(jnp.int32, (AUG, HEAD_DIM_M), 0) == 0
                for j in range(N_UNITS):
                    cta_s[j, 0:HEAD_DIM_M, :] = c0_ref[seq, 0, j].T
                    cta_s[j, HEAD_DIM_M:HEAD_AUG, :] = jnp.where(first_row, n0_ref[seq, 0, j:j + 1, :], 0.0)
                mst_s[...] = m0_ref[seq, 0]
            else:
                cta_s[...] = jnp.zeros(cta_s.shape, F32)
                mst_s[...] = jnp.zeros(mst_s.shape, F32)

        units = []
        for d in range(N_DIRS):
            blk = seq * nc + (c if d == 0 else nc - 1 - c)
            row0 = pl.multiple_of(blk * CHUNK, CHUNK)
            gr0 = pl.multiple_of(blk * GATE_ROWS, GATE_ROWS)
            b_last = rows_s[d, 2, pl.ds(gr0, GATE_ROWS), :]
            a_max = rows_s[d, 3, pl.ds(gr0, GATE_ROWS), :]
            w_rows = rows_s[d, 4, pl.ds(gr0, GATE_ROWS), :]
            m_prev = mst_s[d]
            mprev_s[d, blk] = m_prev
            m_new = b_last + jnp.maximum(m_prev, a_max)
            mst_s[d] = m_new
            decay = jnp.exp(b_last + m_prev - m_new)
            fac = jnp.exp(a_max + b_last - m_new)
            for hh in range(N_HEADS_M):
                hs = slice(hh * HEAD_DIM_M, (hh + 1) * HEAD_DIM_M)
                units.append(dict(d=d, hh=hh, j=d * N_HEADS_M + hh, blk=blk,
                                  kc=k_s[pl.ds(row0, CHUNK), hs], vta=vta_s[blk, hh],
                                  w=w_rows[hh:hh + 1, :], decay=decay[hh:hh + 1, :], fac=fac[hh:hh + 1, :]))
        for un in units:
            vw = (un["vta"].astype(F32) * un["w"]).astype(BF16)
            un["upd"] = _dot(vw, un["kc"])
        cact_s[pl.ds(pl.multiple_of(it * CHUNK, CHUNK), CHUNK // 2), :] = conv_tile(seq, c * CHUNK)
        for un in units:
            cta = cta_s[un["j"]]
            ctab_s[un["d"], un["blk"], un["hh"]] = cta.astype(BF16)
            cta_s[un["j"]] = un["decay"] * cta + un["fac"] * un["upd"]

        if emit_state:
            @pl.when(c == nc - 1)
            def _():
                for j in range(N_UNITS):
                    co_ref[seq, j] = cta_s[j, 0:HEAD_DIM_M, :].T
                    no_ref[seq, j:j + 1, :] = cta_s[j, HEAD_DIM_M:HEAD_DIM_M + 1, :]
                mo_ref[seq] = mst_s[...]
        return carry

    lax.fori_loop(0, n_blk, state_step, 0)

    def chunk_out(blk, carry):
        row0 = pl.multiple_of(blk * CHUNK, CHUNK)
        gr0 = pl.multiple_of(blk * GATE_ROWS, GATE_ROWS)
        dirs = []
        for d in range(N_DIRS):
            m_prev = mprev_s[d, blk]
            mm = jnp.maximum(m_prev, rows_s[d, 1, pl.ds(gr0, GATE_ROWS), :])
            floor = jnp.exp(-(rows_s[d, 0, pl.ds(gr0, GATE_ROWS), :] + mm))
            dirs.append(dict(mm=mm, inter=jnp.exp(m_prev - mm), floor=floor,
                             acol=acol_s[d, blk], mask=upper if d == 0 else lower))
        heads = []
        for hh in range(N_HEADS_M):
            hs = slice(hh * HEAD_DIM_M, (hh + 1) * HEAD_DIM_M)
            hd = dict(hh=hh, hs=hs, kc=k_s[pl.ds(row0, CHUNK), hs], qt=qt_s[blk, hs, :], vta=vta_s[blk, hh])
            hd["st"] = _dot(hd["kc"], hd["qt"])
            hd["p1"] = [_dot(ctab_s[d, blk, hh], hd["qt"]) for d in range(N_DIRS)]
            heads.append(hd)
        for hd in heads:
            hh = hd["hh"]
            sm = []
            for dd in dirs:
                z = dd["acol"][:, hh:hh + 1] - dd["mm"][hh:hh + 1, :]
                sm.append((hd["st"] * jnp.exp(jnp.where(dd["mask"], z, neg_inf))).astype(BF16))
            hd["p2"] = _dot(hd["vta"], jnp.concatenate(sm, axis=1))
        half = CHUNK // 2
        cact_s[pl.ds(row0 + half, half), :] = conv_tile(blk // nc, (blk % nc) * CHUNK + half)
        parts = []
        for hd in heads:
            hh = hd["hh"]
            hsum = None
            for d, dd in enumerate(dirs):
                numa = dd["inter"][hh:hh + 1, :] * hd["p1"][d] + hd["p2"][:, d * CHUNK:(d + 1) * CHUNK]
                den = numa[HEAD_DIM_M:HEAD_DIM_M + 1, :]
                ht = numa[0:HEAD_DIM_M, :] / jnp.maximum(jnp.abs(den), dd["floor"][hh:hh + 1, :])
                hsum = ht if hsum is None else hsum + ht
            r = lax.rsqrt(jnp.mean(hsum * hsum, axis=0, keepdims=True) + EPS)
            parts.append((hsum * r * gmh_ref[0, hd["hs"], :]).T)
        oc = jnp.concatenate(parts, axis=1) * so_s[pl.ds(row0, CHUNK), :].astype(F32)
        mix_s[pl.ds(row0, CHUNK), D_FOURIER + D_CONV:D_MIX] = oc.astype(BF16)
        return carry

    lax.fori_loop(0, n_blk, chunk_out, 0)
    mix_s[:, D_FOURIER:D_FOURIER + D_CONV] = _dot(cact_s[...], wpw_ref[0]).astype(BF16)

    ga1 = mod_ref[0][:, 2 * D_MODEL:3 * D_MODEL]
    res = _dot(mix_s[...], wout_ref[0])
    x1_ref[...] = (load_x() + ga1 * res).reshape(bb, t, D_MODEL)


def _whole(a):
    nd = a.ndim
    return pl.BlockSpec(a.shape, lambda g: (0,) * nd, pipeline_mode=pl.Buffered(1))


def _layer_block(a, l):
    nd = a.ndim
    return pl.BlockSpec((1,) + a.shape[1:], lambda g: (l,) + (0,) * (nd - 1), pipeline_mode=pl.Buffered(1))


def _mixer_call(x, pos, mod_rows, mod_base, per_batch_mod, l, pw, state0, emit_state):
    nb, t, _ = x.shape
    bb = ROWS // t
    assert bb * t == ROWS and nb % bb == 0 and t % CHUNK == 0
    rows = ROWS
    n_blk = rows // CHUNK
    has_init = state0 is not None
    add_pos = pos is not None

    args = [x]
    in_specs = [pl.BlockSpec((bb, t, D_MODEL), lambda g: (g, 0, 0))]
    if add_pos:
        args.append(pos)
        in_specs.append(_whole(pos))
    args.append(mod_rows)
    if per_batch_mod:
        assert bb == 1
        in_specs.append(pl.BlockSpec((1, 1, N_MOD * D_MODEL), lambda g: (mod_base + g, 0, 0)))
    else:
        in_specs.append(pl.BlockSpec((1, 1, N_MOD * D_MODEL), lambda g: (mod_base, 0, 0)))
    for name in ("g1", "win", "wgt", "bg", "bgt"):
        args.append(pw[name])
        in_specs.append(_layer_block(pw[name], l))
    for a in (pw["blk"], pw["dc"][t], pw["ds"][t]):
        args.append(a)
        in_specs.append(_whole(a))
    for name in ("wfn", "wdw", "bdw", "lng", "lnb", "wpw", "gmh"):
        args.append(pw[name])
        in_specs.append(_layer_block(pw[name], l))
    if has_init:
        c0, n0, m0 = state0
        args += [c0, n0, m0]
        in_specs += [
            pl.BlockSpec((bb, 1, N_UNITS, HEAD_DIM_M, HEAD_DIM_M), lambda g: (g, l, 0, 0, 0)),
            pl.BlockSpec((bb, 1, N_UNITS, HEAD_DIM_M), lambda g: (g, l, 0, 0)),
            pl.BlockSpec((bb, 1, N_DIRS, GATE_ROWS, CHUNK), lambda g: (g, l, 0, 0, 0)),
        ]
    args.append(pw["wout"])
    in_specs.append(_layer_block(pw["wout"], l))

    out_shape = [jax.ShapeDtypeStruct((nb, t, D_MODEL), F32)]
    out_specs = [pl.BlockSpec((bb, t, D_MODEL), lambda g: (g, 0, 0), pipeline_mode=pl.Buffered(1))]
    if emit_state:
        out_shape += [
            jax.ShapeDtypeStruct((nb, N_UNITS, HEAD_DIM_M, HEAD_DIM_M), F32),
            jax.ShapeDtypeStruct((nb, N_UNITS, HEAD_DIM_M), F32),
            jax.ShapeDtypeStruct((nb, N_DIRS, GATE_ROWS, CHUNK), F32),
        ]
        out_specs += [
            pl.BlockSpec((bb, N_UNITS, HEAD_DIM_M, HEAD_DIM_M), lambda g: (g, 0, 0, 0)),
            pl.BlockSpec((bb, N_UNITS, HEAD_DIM_M), lambda g: (g, 0, 0)),
            pl.BlockSpec((bb, N_DIRS, GATE_ROWS, CHUNK), lambda g: (g, 0, 0, 0)),
        ]
    scratch = [
        pltpu.VMEM((rows, D_FOURIER), BF16),
        pltpu.VMEM((bb, t + 2 * PAD_LO, D_CONV), F32),
        pltpu.VMEM((SUBLANES, bb, t + SHIFT_ROWS, D_CONV), F32),
        pltpu.VMEM((rows, D_CONV), BF16),
        pltpu.VMEM((n_blk, D_MLSTM, CHUNK), BF16),
        pltpu.VMEM((rows, D_MLSTM), BF16),
        pltpu.VMEM((n_blk, N_HEADS_M, HEAD_AUG, CHUNK), BF16),
        pltpu.VMEM((rows, D_MLSTM), BF16),
        pltpu.VMEM((N_DIRS, 5, n_blk * GATE_ROWS, CHUNK), F32),
        pltpu.VMEM((N_DIRS, n_blk, CHUNK, N_HEADS_M), F32),
        pltpu.VMEM((N_DIRS, n_blk, GATE_ROWS, CHUNK), F32),
        pltpu.VMEM((N_DIRS, n_blk, N_HEADS_M, HEAD_AUG, HEAD_DIM_M), BF16),
        pltpu.VMEM((rows, D_MIX), BF16),
        pltpu.VMEM((N_UNITS, HEAD_AUG, HEAD_DIM_M), F32),
        pltpu.VMEM((N_DIRS, GATE_ROWS, CHUNK), F32),
    ]
    body = functools.partial(_mixer_body, bb, t, has_init, emit_state, add_pos)
    return pl.pallas_call(
        body,
        grid=(nb // bb,),
        in_specs=in_specs,
        out_specs=out_specs,
        out_shape=out_shape,
        scratch_shapes=scratch,
        compiler_params=pltpu.CompilerParams(dimension_semantics=("arbitrary",),
                                             vmem_limit_bytes=VMEM_LIMIT),
        name="mixer_t%d" % t,
    )(*args)


def _route_rows(logits_t, b_router_col):
    scores = _sigmoid(logits_t)
    sel = scores + b_router_col
    sel_r = [sel[e:e + 1, :] for e in range(N_EXPERTS)]
    sc_r = [scores[e:e + 1, :] for e in range(N_EXPERTS)]

    best = None
    best_v = None
    for g in range(N_EXPERT_GROUPS):
        a, b, c, d = sel_r[g * EXPERTS_PER_GROUP:(g + 1) * EXPERTS_PER_GROUP]
        hi1, lo1 = jnp.maximum(a, b), jnp.minimum(a, b)
        hi2, lo2 = jnp.maximum(c, d), jnp.minimum(c, d)
        gs = jnp.maximum(hi1, hi2) + jnp.maximum(jnp.minimum(hi1, hi2), jnp.maximum(lo1, lo2))
        if g == 0:
            best = jnp.zeros(gs.shape, jnp.int32)
            best_v = gs
        else:
            upd = gs > best_v
            best = jnp.where(upd, g, best)
            best_v = jnp.where(upd, gs, best_v)

    def pick(rows, j):
        out = rows[(N_EXPERT_GROUPS - 1) * EXPERTS_PER_GROUP + j]
        for g in range(N_EXPERT_GROUPS - 2, -1, -1):
            out = jnp.where(best == g, rows[g * EXPERTS_PER_GROUP + j], out)
        return out

    s = [pick(sel_r, j) for j in range(EXPERTS_PER_GROUP)]
    sc = [pick(sc_r, j) for j in range(EXPERTS_PER_GROUP)]
    rank = [jnp.zeros(best.shape, jnp.int32) for _ in range(EXPERTS_PER_GROUP)]
    for a in range(EXPERTS_PER_GROUP):
        for b in range(a + 1, EXPERTS_PER_GROUP):
            b_first = s[b] > s[a]
            rank[a] = rank[a] + b_first.astype(jnp.int32)
            rank[b] = rank[b] + (1 - b_first.astype(jnp.int32))
    w = [jnp.where(rank[j] < 2, sc[j], 0.0) for j in range(EXPERTS_PER_GROUP)]
    tot = w[0] + w[1] + w[2] + w[3]
    return best, [wj / tot for wj in w]


def _moe_body(is_last, x_ref, mod_ref, g2_ref, wr2_ref, brc_ref, utri_ref, wg_ref, wu_ref, wd_ref, fg_ref,
              o_ref, he_s, p_s, pt_s, ys_s, sm_s):
    gi = pl.program_id(1)
    tm = ROWS
    n_cap_blk = MOE_CAP // MOE_BLK

    @pl.when(gi == 0)
    def _():
        mod = mod_ref[0]
        sh2 = mod[:, 3 * D_MODEL:4 * D_MODEL]
        sc2 = mod[:, 4 * D_MODEL:5 * D_MODEL]
        h = _rmsnorm_rows(x_ref[...], g2_ref[0]) * (1.0 + sc2) + sh2
        h_hi, h_lo = _hi_lo(h)
        he_s[:, 0:D_MODEL] = h_hi
        lg = _dot(jnp.concatenate([h_hi, h_lo], axis=1), wr2_ref[...])
        lg = lg[:, 0:N_EXPERTS] + lg[:, N_EXPERTS:2 * N_EXPERTS]
        logits_t = jnp.concatenate([lg, jnp.zeros((tm, MOE_EXT - N_EXPERTS), F32)], axis=1).T[0:N_EXPERTS, :]
        best, cw = _route_rows(logits_t, brc_ref[...])

        row_i = lax.broadcasted_iota(jnp.int32, (16, tm), 0)
        onehot = (row_i == best).astype(F32)
        before = _dot(onehot.astype(BF16), utri_ref[...])
        dest = jnp.zeros((1, tm), F32)
        off_blk = jnp.int32(0)
        for g in range(N_EXPERT_GROUPS):
            cnt = jnp.sum(onehot[g:g + 1, :]).astype(jnp.int32)
            n_blk = lax.shift_right_logical(cnt + (MOE_BLK - 1), int(math.log2(MOE_BLK)))
            sm_s[g] = off_blk
            sm_s[N_EXPERT_GROUPS + g] = n_blk
            base = (off_blk * MOE_BLK).astype(F32)
            dest = dest + onehot[g:g + 1, :] * (before[g:g + 1, :] + base)
            off_blk = off_blk + n_blk

        cw_hi = [wj.astype(BF16).astype(F32) for wj in cw]
        cw_lo = [wj - hj for wj, hj in zip(cw, cw_hi)]
        stack = jnp.concatenate([dest] + cw_hi + cw_lo + [jnp.zeros((MOE_EXT - 9, tm), F32)], axis=0)
        he_s[:, D_MODEL:D_MODEL + MOE_EXT] = stack.T.astype(BF16)
        dest_lanes = jnp.broadcast_to(dest, (MOE_BLK, tm)).T

        r_iota = lax.broadcasted_iota(jnp.int32, (MOE_BLK, 1), 0)
        for b in range(n_cap_blk):
            rows_f = (r_iota + b * MOE_BLK).astype(F32)
            p_s[b * MOE_BLK:(b + 1) * MOE_BLK, :] = jnp.where(dest == rows_f, 1.0, 0.0).astype(BF16)
        c_iota = lax.broadcasted_iota(jnp.int32, (1, MOE_BLK), 1)
        for b in range(n_cap_blk):
            cols_f = (c_iota + b * MOE_BLK).astype(F32)
            pt_s[:, b * MOE_BLK:(b + 1) * MOE_BLK] = jnp.where(dest_lanes == cols_f, 1.0, 0.0).astype(BF16)
        ys_s[...] = jnp.zeros(ys_s.shape, BF16)

    first_blk = sm_s[gi]
    n_blk = sm_s[N_EXPERT_GROUPS + gi]

    def block(i, carry):
        r0 = pl.multiple_of((first_blk + i) * MOE_BLK, MOE_BLK)
        xe = _dot(p_s[pl.ds(r0, MOE_BLK), :], he_s[...])
        xs = xe[:, 0:D_MODEL].astype(BF16)
        acts = []
        for j in range(EXPERTS_PER_GROUP):
            gate = _dot(xs, wg_ref[0, j])
            up = _dot(xs, wu_ref[0, j])
            cwj = (xe[:, D_MODEL + 1 + j:D_MODEL + 2 + j]
                   + xe[:, D_MODEL + 1 + EXPERTS_PER_GROUP + j:D_MODEL + 2 + EXPERTS_PER_GROUP + j])
            acts.append((gate * _sigmoid(gate) * up * cwj).astype(BF16))
        ys_s[pl.ds(r0, MOE_BLK), :] = _dot(jnp.concatenate(acts, axis=1), wd_ref[0, 0]).astype(BF16)
        return carry

    lax.fori_loop(0, n_blk, block, 0)

    @pl.when(gi == N_EXPERT_GROUPS - 1)
    def _():
        ga2 = mod_ref[0][:, 5 * D_MODEL:6 * D_MODEL]
        xo = x_ref[...] + ga2 * _dot(pt_s[...], ys_s[...])
        if is_last:
            xo = _rmsnorm_rows(xo, fg_ref[...])
        o_ref[...] = xo


def _moe_call(x2d, mod_rows, mod_base, per_tile_mod, l, pw, is_last):
    n_tok = x2d.shape[0]
    tm = ROWS
    assert n_tok % tm == 0
    if per_tile_mod:
        mod_spec = pl.BlockSpec((1, 1, N_MOD * D_MODEL), lambda i, g: (mod_base + i, 0, 0))
    else:
        mod_spec = pl.BlockSpec((1, 1, N_MOD * D_MODEL), lambda i, g: (mod_base, 0, 0))

    def const(shape):
        return pl.BlockSpec(shape, lambda i, g: (0,) * len(shape), pipeline_mode=pl.Buffered(1))

    def grp(shape):
        return pl.BlockSpec((1, EXPERTS_PER_GROUP) + shape, lambda i, g: (l, g, 0, 0))

    return pl.pallas_call(
        functools.partial(_moe_body, is_last),
        grid=(n_tok // tm, N_EXPERT_GROUPS),
        in_specs=[
            pl.BlockSpec((tm, D_MODEL), lambda i, g: (i, 0)),
            mod_spec,
            pl.BlockSpec((1, 1, D_MODEL), lambda i, g: (l, 0, 0), pipeline_mode=pl.Buffered(1)),
            const((2 * D_MODEL, 2 * N_EXPERTS)),
            const((N_EXPERTS, 1)),
            const((tm, tm)),
            grp((D_MODEL, D_EXPERT)),
            grp((D_MODEL, D_EXPERT)),
            pl.BlockSpec((1, 1, EXPERTS_PER_GROUP * D_EXPERT, D_MODEL), lambda i, g: (l, g, 0, 0)),
            const((1, D_MODEL)),
        ],
        out_specs=pl.BlockSpec((tm, D_MODEL), lambda i, g: (i, 0)),
        out_shape=jax.ShapeDtypeStruct((n_tok, D_MODEL), F32),
        scratch_shapes=[
            pltpu.VMEM((tm, D_MODEL + MOE_EXT), BF16),
            pltpu.VMEM((MOE_CAP, tm), BF16),
            pltpu.VMEM((tm, MOE_CAP), BF16),
            pltpu.VMEM((MOE_CAP, D_MODEL), BF16),
            pltpu.SMEM((2 * N_EXPERT_GROUPS,), jnp.int32),
        ],
        compiler_params=pltpu.CompilerParams(dimension_semantics=("arbitrary", "arbitrary"),
                                             vmem_limit_bytes=VMEM_LIMIT),
        name="moe",
    )(x2d, mod_rows, pw["g2"], pw["wr2"], pw["brc"], pw["utri"], pw["weg"], pw["weu"], pw["wed"], pw["fg"])


def _dft_tables(t):
    idx = np.arange(t, dtype=np.int64)
    ang = 2.0 * np.pi * ((idx[:, None] * idx[None, :]) % t).astype(np.float64) / t
    scale = 1.0 / math.sqrt(t * D_FOURIER_GROUP)
    return (np.cos(ang) * scale).astype(np.float32), (-np.sin(ang) * scale).astype(np.float32)


def _group_tables():
    idx = np.arange(D_FOURIER_GROUP, dtype=np.int64)
    ang = 2.0 * np.pi * ((idx[:, None] * idx[None, :]) % D_FOURIER_GROUP).astype(np.float64) / D_FOURIER_GROUP
    eye = np.eye(N_FOURIER_GROUPS)
    return np.concatenate([np.kron(eye, np.cos(ang)), np.kron(eye, np.sin(ang))], axis=1).astype(np.float32)


def _grid_pos_embed(n_tokens, dtype):
    rows = n_tokens // GRID_W
    r = jnp.repeat(jnp.arange(rows, dtype=jnp.float32), GRID_W)
    col = jnp.tile(jnp.arange(GRID_W, dtype=jnp.float32), rows)
    quarter = D_MODEL // 4
    omega = 1.0 / (10000.0 ** (jnp.arange(quarter, dtype=jnp.float32) / quarter))

    def enc(p):
        a = p[:, None] * omega[None, :]
        return jnp.concatenate([jnp.sin(a), jnp.cos(a)], axis=-1)

    return jnp.concatenate([enc(r), enc(col)], axis=-1).astype(dtype)


def _router_hi_lo(w_router):
    w_hi = w_router.astype(BF16)
    w_lo = (w_router - w_hi.astype(F32)).astype(BF16)
    return jnp.concatenate([jnp.concatenate([w_hi, w_lo], axis=1),
                            jnp.concatenate([w_hi, jnp.zeros_like(w_lo)], axis=1)], axis=0)


def _prepare(seq_lens, norm1_g, norm2_g, w_in, w_fnet, w_dw, b_dw, conv_ln_g, conv_ln_b, w_pw, b_gate, g_mh,
             w_out, w_router, b_router, w_exp_gate, w_exp_up, w_exp_down, final_g):
    dft = {t: _dft_tables(t) for t in sorted(set(seq_lens))}
    w_in_b = w_in.astype(BF16)
    tok = np.arange(ROWS)
    return dict(
        g1=norm1_g.reshape(DEPTH, 1, D_MODEL), g2=norm2_g.reshape(DEPTH, 1, D_MODEL),
        win=w_in_b, wgt=jnp.swapaxes(w_in_b[:, :, O_G:], 1, 2),
        bg=b_gate.reshape(DEPTH, 1, N_GATES), bgt=b_gate.reshape(DEPTH, N_GATES, 1),
        blk=jnp.asarray(_group_tables()).astype(BF16),
        dc={t: jnp.asarray(v[0]).astype(BF16) for t, v in dft.items()},
        ds={t: jnp.asarray(v[1]).astype(BF16) for t, v in dft.items()},
        wfn=w_fnet.astype(BF16),
        wdw=jnp.concatenate([w_dw, jnp.zeros((DEPTH, 1, D_CONV), F32)], axis=1),
        bdw=b_dw.reshape(DEPTH, 1, D_CONV), lng=conv_ln_g.reshape(DEPTH, 1, D_CONV),
        lnb=conv_ln_b.reshape(DEPTH, 1, D_CONV), wpw=w_pw.astype(BF16),
        gmh=jnp.broadcast_to(g_mh.reshape(DEPTH, D_MLSTM, 1), (DEPTH, D_MLSTM, CHUNK)),
        wout=w_out.astype(BF16),
        wr2=_router_hi_lo(w_router), brc=b_router.reshape(N_EXPERTS, 1),
        utri=jnp.asarray((tok[:, None] < tok[None, :]).astype(np.float32)).astype(BF16),
        weg=w_exp_gate.astype(BF16), weu=w_exp_up.astype(BF16),
        wed=w_exp_down.reshape(DEPTH, N_EXPERT_GROUPS, EXPERTS_PER_GROUP * D_EXPERT, D_MODEL).astype(BF16),
        fg=final_g.reshape(1, D_MODEL),
    )


def kernel(x_prompt, x_sample, state_C, state_n, state_m, c, c_ctx, w_ada, b_ada, norm1_g, norm2_g, w_in, w_fnet, w_dw, b_dw, conv_ln_g, conv_ln_b, w_pw, b_gate, g_mh, w_out, w_router, b_router, w_exp_gate, w_exp_up, w_exp_down, final_g):
    bp, tp, _ = x_prompt.shape
    bs, ts, _ = x_sample.shape

    cv = jnp.zeros((MOD_ROWS, D_MODEL), F32).at[:bs].set(c).at[bs].set(c_ctx)
    mod_all = _ada_call(cv, w_ada, b_ada)
    mod_rows = mod_all.reshape(DEPTH * MOD_ROWS, 1, N_MOD * D_MODEL)

    pw = _prepare((tp, ts), norm1_g, norm2_g, w_in, w_fnet, w_dw, b_dw, conv_ln_g, conv_ln_b, w_pw, b_gate,
                  g_mh, w_out, w_router, b_router, w_exp_gate, w_exp_up, w_exp_down, final_g)
    pos = _grid_pos_embed(ts, x_sample.dtype)

    c0 = state_C.reshape(bs, DEPTH, N_UNITS, HEAD_DIM_M, HEAD_DIM_M)
    n0 = state_n.reshape(bs, DEPTH, N_UNITS, HEAD_DIM_M)
    m0 = jnp.pad(state_m, ((0, 0), (0, 0), (0, 0), (0, GATE_ROWS - N_HEADS_M)))
    m0 = jnp.broadcast_to(m0[..., None], (bs, DEPTH, N_DIRS, GATE_ROWS, CHUNK))

    xp, xs = x_prompt, x_sample
    cs, ns, ms = [], [], []
    for l in range(DEPTH):
        is_last = l == DEPTH - 1
        base = l * MOD_ROWS
        xp, c_l, n_l, m_l = _mixer_call(xp, None, mod_rows, base + bs, False, l, pw, None, True)
        cs.append(c_l)
        ns.append(n_l)
        ms.append(m_l[:, :, :N_HEADS_M, 0])
        xp = _moe_call(xp.reshape(bp * tp, D_MODEL), mod_rows, base + bs, False, l, pw, is_last)
        xp = xp.reshape(bp, tp, D_MODEL)

        (xs,) = _mixer_call(xs, pos if l == 0 else None, mod_rows, base, True, l, pw, (c0, n0, m0), False)
        xs = _moe_call(xs.reshape(bs * ts, D_MODEL), mod_rows, base, True, l, pw, is_last)
        xs = xs.reshape(bs, ts, D_MODEL)

    new_c = jnp.stack(cs, axis=1).reshape(bp, DEPTH, N_DIRS, N_HEADS_M, HEAD_DIM_M, HEAD_DIM_M)
    new_n = jnp.stack(ns, axis=1).reshape(bp, DEPTH, N_DIRS, N_HEADS_M, HEAD_DIM_M)
    new_m = jnp.stack(ms, axis=1).reshape(bp, DEPTH, N_DIRS, N_HEADS_M)
    return (xp, xs, new_c, new_n, new_m)
```

```python
import functools
import math

import numpy as np
import jax
import jax.numpy as jnp
from jax import lax
from jax.experimental import pallas as pl
from jax.experimental.pallas import tpu as pltpu

D_MODEL = 1024
DEPTH = 4
GRID_W = 64
EPS = 1e-6
D_FOURIER = 256
N_FOURIER_GROUPS = 4
D_FOURIER_GROUP = D_FOURIER // N_FOURIER_GROUPS
D_CONV = 256
CONV_WIDTH = 31
CONV_PAD = CONV_WIDTH // 2
D_MLSTM = 512
N_HEADS_M = 4
HEAD_DIM_M = D_MLSTM // N_HEADS_M
N_DIRS = 2
CHUNK = 128
N_GATES = 2 * N_DIRS * N_HEADS_M
N_UNITS = N_DIRS * N_HEADS_M
D_MIX = D_FOURIER + D_CONV + D_MLSTM
D_IN_PROJ = D_FOURIER + 2 * D_CONV + 4 * D_MLSTM + N_GATES
N_EXPERTS = 16
N_EXPERT_GROUPS = 4
EXPERTS_PER_GROUP = N_EXPERTS // N_EXPERT_GROUPS
D_EXPERT = 256
N_MOD = 6

O_B = D_FOURIER
O_Q = O_B + 2 * D_CONV
O_K = O_Q + D_MLSTM
O_V = O_K + D_MLSTM
O_O = O_V + D_MLSTM
O_G = O_O + D_MLSTM

ROWS = 1024
CONV_ROW_TILE = 64
PAD_LO = 16
MOD_ROWS = 16
SUBLANES = 8
SHIFT_ROWS = 24
AUG = 16
HEAD_AUG = HEAD_DIM_M + AUG
GATE_ROWS = 8
MOE_BLK = 128
MOE_CAP = ROWS + N_EXPERT_GROUPS * MOE_BLK
MOE_EXT = 128
VMEM_LIMIT = 56 * 1024 * 1024

F32 = jnp.float32
BF16 = jnp.bfloat16


def _dot(a, b):
    return jnp.dot(a, b, preferred_element_type=F32)


def _sigmoid(x):
    return 1.0 / (1.0 + jnp.exp(-x))


def _log_sigmoid(x):
    return jnp.minimum(x, 0.0) - jnp.log(1.0 + jnp.exp(-jnp.abs(x)))


def _hi_lo(a_f32):
    hi = a_f32.astype(BF16)
    return hi, (a_f32 - hi.astype(F32)).astype(BF16)


def _split_dot(a_f32, b_bf16):
    hi, lo = _hi_lo(a_f32)
    return _dot(hi, b_bf16) + _dot(lo, b_bf16)


def _split_dot_left(b_bf16, a_f32):
    hi, lo = _hi_lo(a_f32)
    return _dot(b_bf16, hi) + _dot(b_bf16, lo)


def _rmsnorm_rows(x, g):
    ms = jnp.mean(x * x, axis=-1, keepdims=True)
    return x * lax.rsqrt(ms + EPS) * g


def _ada_body(c_ref, w_ref, b_ref, o_ref):
    cv = c_ref[...]
    s_hi, s_lo = _hi_lo(cv * _sigmoid(cv))
    w_hi, w_lo = _hi_lo(w_ref[0])
    both = _dot(jnp.concatenate([s_hi, s_lo], axis=0), w_hi)
    o_ref[0] = both[0:MOD_ROWS] + both[MOD_ROWS:2 * MOD_ROWS] + _dot(s_hi, w_lo) + b_ref[0]


def _ada_call(cv, w_ada, b_ada):
    n_col = N_MOD * D_MODEL
    tn = D_MODEL
    return pl.pallas_call(
        _ada_body,
        grid=(DEPTH, n_col // tn),
        in_specs=[
            pl.BlockSpec((MOD_ROWS, D_MODEL), lambda l, j: (0, 0)),
            pl.BlockSpec((1, D_MODEL, tn), lambda l, j: (l, 0, j)),
            pl.BlockSpec((1, 1, tn), lambda l, j: (l, 0, j)),
        ],
        out_specs=pl.BlockSpec((1, MOD_ROWS, tn), lambda l, j: (l, 0, j)),
        out_shape=jax.ShapeDtypeStruct((DEPTH, MOD_ROWS, n_col), F32),
        compiler_params=pltpu.CompilerParams(dimension_semantics=("arbitrary", "arbitrary")),
        name="ada",
    )(cv, w_ada, b_ada.reshape(DEPTH, 1, n_col))


def _scan_max_lanes(x, reverse):
    lane = lax.broadcasted_iota(jnp.int32, x.shape, 1)
    neg_inf = jnp.float32(-jnp.inf)
    sh = 1
    while sh < CHUNK:
        if reverse:
            y = pltpu.roll(x, CHUNK - sh, axis=1)
            x = jnp.maximum(x, jnp.where(lane < CHUNK - sh, y, neg_inf))
        else:
            y = pltpu.roll(x, sh, axis=1)
            x = jnp.maximum(x, jnp.where(lane >= sh, y, neg_inf))
        sh *= 2
    return x


def _mixer_body(bb, t, has_init, emit_state, add_pos, *refs):
    rows = bb * t
    nc = t // CHUNK
    n_blk = rows // CHUNK
    refs = list(refs)
    x_ref = refs.pop(0)
    if add_pos:
        posr_ref, posc_ref = refs[:2]
        refs = refs[2:]
    (mod_ref, g1_ref, win_ref, bg_ref, blk_ref, dc_ref, ds_ref, wfn_ref,
     wdw_ref, bdw_ref, lng_ref, lnb_ref, wpw_ref, gmh_ref) = refs[:14]
    refs = refs[14:]
    if has_init:
        c0_ref, n0_ref, m0_ref = refs[:3]
        refs = refs[3:]
    wout_ref = refs.pop(0)
    x1_ref = refs.pop(0)
    if emit_state:
        co_ref, no_ref, mo_ref = refs[:3]
        refs = refs[3:]
    (xa_s, pad_s, shf_s, cact_s, qt_s, k_s, vta_s, so_s, rows_s, acol_s, mprev_s, ctab_s,
     mix_s, cta_s, mst_s) = refs

    if add_pos:
        half_d = D_MODEL // 2
        for g in range(t // GRID_W):
            rs = slice(g * GRID_W, (g + 1) * GRID_W)
            emb = jnp.concatenate([jnp.broadcast_to(posr_ref[g:g + 1, :], (GRID_W, half_d)), posc_ref[...]], axis=1)
            x1_ref[0, rs, :] = x_ref[0, rs, :] + emb

    def load_x():
        src = x1_ref if add_pos else x_ref
        return src[...].reshape(rows, D_MODEL)

    def w_in(lo, hi):
        return win_ref[0, :, lo:hi]

    x = load_x()
    mod = mod_ref[0]
    sh1 = mod[:, 0:D_MODEL]
    sc1 = mod[:, D_MODEL:2 * D_MODEL]
    h = _rmsnorm_rows(x, g1_ref[0]) * (1.0 + sc1) + sh1
    hb = h.astype(BF16)

    pb = _dot(hb, w_in(O_B, O_Q))
    u = pb[:, :D_CONV] * _sigmoid(pb[:, D_CONV:])
    zpad = jnp.zeros((PAD_LO, D_CONV), F32)
    for i in range(bb):
        pad_s[i, 0:PAD_LO, :] = zpad
        pad_s[i, PAD_LO:PAD_LO + t, :] = u[i * t:(i + 1) * t]
        pad_s[i, PAD_LO + t:2 * PAD_LO + t, :] = zpad

    for i in range(bb):
        for s8 in range(1, SUBLANES):
            shf_s[s8 - 1, i] = pad_s[i, s8:s8 + t + SHIFT_ROWS, :]

    def conv_tile(seq, r0):
        acc = jnp.broadcast_to(bdw_ref[0], (CONV_ROW_TILE, D_CONV))
        for j in range(CONV_WIDTH):
            q = j + PAD_LO - CONV_PAD
            win_rows = pl.ds(pl.multiple_of(r0 + (q // SUBLANES) * SUBLANES, SUBLANES), CONV_ROW_TILE)
            win = pad_s[seq, win_rows, :] if q % SUBLANES == 0 else shf_s[q % SUBLANES - 1, seq, win_rows, :]
            acc = acc + win * wdw_ref[0, j:j + 1, :]
        mu = jnp.mean(acc, axis=-1, keepdims=True)
        cen = acc - mu
        var = jnp.mean(cen * cen, axis=-1, keepdims=True)
        uf = cen * lax.rsqrt(var + EPS) * lng_ref[0] + lnb_ref[0]
        return (uf * _sigmoid(uf)).astype(BF16)

    xa_s[...] = _dot(hb, w_in(0, O_B)).astype(BF16)
    k_s[...] = (_dot(hb, w_in(O_K, O_V)) * (HEAD_DIM_M ** -0.5)).astype(BF16)
    og = _dot(hb, w_in(O_O, D_IN_PROJ))
    so_s[...] = _sigmoid(og[:, 0:D_MLSTM]).astype(BF16)
    qf = _dot(hb, w_in(O_Q, O_K))
    for b in range(n_blk):
        qt_s[b] = qf[b * CHUNK:(b + 1) * CHUNK, :].T.astype(BF16)
    vf = _dot(hb, w_in(O_V, O_O))
    ones_rows = jnp.where(lax.broadcasted_iota(jnp.int32, (AUG, CHUNK), 0) == 0, 1.0, 0.0).astype(BF16)
    for b in range(n_blk):
        vt = vf[b * CHUNK:(b + 1) * CHUNK, :].T.astype(BF16)
        for hh in range(N_HEADS_M):
            vta_s[b, hh, 0:HEAD_DIM_M, :] = vt[hh * HEAD_DIM_M:(hh + 1) * HEAD_DIM_M, :]
            vta_s[b, hh, HEAD_DIM_M:HEAD_AUG, :] = ones_rows

    r_i = lax.broadcasted_iota(jnp.int32, (CHUNK, CHUNK), 0)
    c_i = lax.broadcasted_iota(jnp.int32, (CHUNK, CHUNK), 1)
    lower = r_i >= c_i
    upper = r_i <= c_i
    tri_lo = lower.astype(BF16)
    tri_up = upper.astype(BF16)
    neg_inf = jnp.float32(-jnp.inf)
    n_gr = n_blk * GATE_ROWS

    gcol = og[:, D_MLSTM:D_MLSTM + N_GATES] + bg_ref[0]
    kind_c = (lax.broadcasted_iota(jnp.int32, gcol.shape, 1) // N_HEADS_M) % 2
    gcol = jnp.where(kind_c == 1, _log_sigmoid(gcol), gcol)
    grow = jnp.concatenate([gcol, jnp.zeros((rows, CHUNK - N_GATES), F32)], axis=1).T[0:N_GATES, :]
    for d in range(N_DIRS):
        g0 = d * 2 * N_HEADS_M
        ll = CHUNK - 1 if d == 0 else 0
        xg = jnp.concatenate([grow[g0:g0 + GATE_ROWS, b * CHUNK:(b + 1) * CHUNK] for b in range(n_blk)], axis=0)
        b_rows = pltpu.roll(_split_dot(xg, tri_up if d == 0 else tri_lo), n_gr - N_HEADS_M, axis=0)
        a_rows = xg - b_rows
        a_run = _scan_max_lanes(a_rows, d == 1)
        b_last = jnp.broadcast_to(b_rows[:, ll:ll + 1], (n_gr, CHUNK))
        a_max = jnp.broadcast_to(a_run[:, ll:ll + 1], (n_gr, CHUNK))
        rows_s[d, 0] = b_rows
        rows_s[d, 1] = a_run
        rows_s[d, 2] = b_last
        rows_s[d, 3] = a_max
        rows_s[d, 4] = jnp.exp(a_rows - a_max)
        for b in range(n_blk):
            gc = gcol[b * CHUNK:(b + 1) * CHUNK, :]
            bcol = _split_dot_left(tri_lo if d == 0 else tri_up, gc)
            acol_s[d, b] = gc[:, g0:g0 + N_HEADS_M] - bcol[:, g0 + N_HEADS_M:g0 + 2 * N_HEADS_M]

    for i in range(bb):
        xa_i = xa_s[i * t:(i + 1) * t, :]
        uu = _dot(xa_i, blk_ref[...])
        y = (_dot(dc_ref[...], uu[:, :D_FOURIER].astype(BF16))
             + _dot(ds_ref[...], uu[:, D_FOURIER:].astype(BF16)))
        mix_s[i * t:(i + 1) * t, 0:D_FOURIER] = _dot(y.astype(BF16), wfn_ref[0]).astype(BF16)

    def state_step(it, carry):
        seq = it // nc
        c = it % nc

        @pl.when(c == 0)
        def _():
            if has_init:
                first_row = lax.broadcasted_iota(jnp.int32, (AUG, HEAD_DIM_M), 0) == 0
                for j in range(N_UNITS):
                    cta_s[j, 0:HEAD_DIM_M, :] = c0_ref[seq, 0, j].T
                    cta_s[j, HEAD_DIM_M:HEAD_AUG, :] = jnp.where(first_row, n0_ref[seq, 0, j:j + 1, :], 0.0)
                mst_s[...] = m0_ref[seq, 0]
            else:
                cta_s[...] = jnp.zeros(cta_s.shape, F32)
                mst_s[...] = jnp.zeros(mst_s.shape, F32)

        units = []
        for d in range(N_DIRS):
            blk = seq * nc + (c if d == 0 else nc - 1 - c)
            row0 = pl.multiple_of(blk * CHUNK, CHUNK)
            gr0 = pl.multiple_of(blk * GATE_ROWS, GATE_ROWS)
            b_last = rows_s[d, 2, pl.ds(gr0, GATE_ROWS), :]
            a_max = rows_s[d, 3, pl.ds(gr0, GATE_ROWS), :]
            w_rows = rows_s[d, 4, pl.ds(gr0, GATE_ROWS), :]
            m_prev = mst_s[d]
            mprev_s[d, blk] = m_prev
            m_new = b_last + jnp.maximum(m_prev, a_max)
            mst_s[d] = m_new
            decay = jnp.exp(b_last + m_prev - m_new)
            fac = jnp.exp(a_max + b_last - m_new)
            for hh in range(N_HEADS_M):
                hs = slice(hh * HEAD_DIM_M, (hh + 1) * HEAD_DIM_M)
                units.append(dict(d=d, hh=hh, j=d * N_HEADS_M + hh, blk=blk,
                                  kc=k_s[pl.ds(row0, CHUNK), hs], vta=vta_s[blk, hh],
                                  w=w_rows[hh:hh + 1, :], decay=decay[hh:hh + 1, :], fac=fac[hh:hh + 1, :]))
        for un in units:
            vw = (un["vta"].astype(F32) * un["w"]).astype(BF16)
            un["upd"] = _dot(vw, un["kc"])
        cact_s[pl.ds(pl.multiple_of(it * CHUNK, CHUNK), CHUNK // 2), :] = conv_tile(seq, c * CHUNK)
        for un in units:
            cta = cta_s[un["j"]]
            ctab_s[un["d"], un["blk"], un["hh"]] = cta.astype(BF16)
            cta_s[un["j"]] = un["decay"] * cta + un["fac"] * un["upd"]

        if emit_state:
            @pl.when(c == nc - 1)
            def _():
                for j in range(N_UNITS):
                    co_ref[seq, j] = cta_s[j, 0:HEAD_DIM_M, :].T
                    no_ref[seq, j:j + 1, :] = cta_s[j, HEAD_DIM_M:HEAD_DIM_M + 1, :]
                mo_ref[seq] = mst_s[...]
        return carry

    lax.fori_loop(0, n_blk, state_step, 0)

    def chunk_out(pair, carry):
        heads = []
        for blk in (2 * pair, 2 * pair + 1):
            row0 = pl.multiple_of(blk * CHUNK, CHUNK)
            gr0 = pl.multiple_of(blk * GATE_ROWS, GATE_ROWS)
            dirs = []
            for d in range(N_DIRS):
                m_prev = mprev_s[d, blk]
                mm = jnp.maximum(m_prev, rows_s[d, 1, pl.ds(gr0, GATE_ROWS), :])
                floor = jnp.exp(-(rows_s[d, 0, pl.ds(gr0, GATE_ROWS), :] + mm))
                dirs.append(dict(mm=mm, inter=jnp.exp(m_prev - mm), floor=floor,
                                 acol=acol_s[d, blk], mask=upper if d == 0 else lower))
            for hh in range(N_HEADS_M):
                hs = slice(hh * HEAD_DIM_M, (hh + 1) * HEAD_DIM_M)
                hd = dict(hh=hh, hs=hs, blk=blk, row0=row0, dirs=dirs,
                          kc=k_s[pl.ds(row0, CHUNK), hs], qt=qt_s[blk, hs, :], vta=vta_s[blk, hh])
                hd["st"] = _dot(hd["kc"], hd["qt"])
                hd["p1"] = [_dot(ctab_s[d, blk, hh], hd["qt"]) for d in range(N_DIRS)]
                heads.append(hd)
        for hd in heads:
            hh = hd["hh"]
            sm = []
            for dd in hd["dirs"]:
                z = dd["acol"][:, hh:hh + 1] - dd["mm"][hh:hh + 1, :]
                sm.append((hd["st"] * jnp.exp(jnp.where(dd["mask"], z, neg_inf))).astype(BF16))
            hd["p2"] = _dot(hd["vta"], jnp.concatenate(sm, axis=1))
        half = CHUNK // 2
        for blk in (2 * pair, 2 * pair + 1):
            row0 = pl.multiple_of(blk * CHUNK, CHUNK)
            cact_s[pl.ds(row0 + half, half), :] = conv_tile(blk // nc, (blk % nc) * CHUNK + half)
        for b2 in range(2):
            parts = []
            for hd in heads[b2 * N_HEADS_M:(b2 + 1) * N_HEADS_M]:
                hh = hd["hh"]
                hsum = None
                for d, dd in enumerate(hd["dirs"]):
                    numa = dd["inter"][hh:hh + 1, :] * hd["p1"][d] + hd["p2"][:, d * CHUNK:(d + 1) * CHUNK]
                    den = numa[HEAD_DIM_M:HEAD_DIM_M + 1, :]
                    ht = numa[0:HEAD_DIM_M, :] / jnp.maximum(jnp.abs(den), dd["floor"][hh:hh + 1, :])
                    hsum = ht if hsum is None else hsum + ht
                r = lax.rsqrt(jnp.mean(hsum * hsum, axis=0, keepdims=True) + EPS)
                parts.append((hsum * r * gmh_ref[0, hd["hs"], :]).T)
            row0 = heads[b2 * N_HEADS_M]["row0"]
            oc = jnp.concatenate(parts, axis=1) * so_s[pl.ds(row0, CHUNK), :].astype(F32)
            mix_s[pl.ds(row0, CHUNK), D_FOURIER + D_CONV:D_MIX] = oc.astype(BF16)
        return carry

    lax.fori_loop(0, n_blk // 2, chunk_out, 0)
    mix_s[:, D_FOURIER:D_FOURIER + D_CONV] = _dot(cact_s[...], wpw_ref[0]).astype(BF16)

    ga1 = mod_ref[0][:, 2 * D_MODEL:3 * D_MODEL]
    res = _dot(mix_s[...], wout_ref[0])
    x1_ref[...] = (load_x() + ga1 * res).reshape(bb, t, D_MODEL)


def _whole(a):
    nd = a.ndim
    return pl.BlockSpec(a.shape, lambda g: (0,) * nd, pipeline_mode=pl.Buffered(1))


def _layer_block(a, l):
    nd = a.ndim
    return pl.BlockSpec((1,) + a.shape[1:], lambda g: (l,) + (0,) * (nd - 1), pipeline_mode=pl.Buffered(1))


def _mixer_call(x, pos, mod_rows, mod_base, per_batch_mod, l, pw, state0, emit_state):
    nb, t, _ = x.shape
    bb = ROWS // t
    assert bb * t == ROWS and nb % bb == 0 and t % CHUNK == 0
    rows = ROWS
    n_blk = rows // CHUNK
    has_init = state0 is not None
    add_pos = pos is not None

    args = [x]
    in_specs = [pl.BlockSpec((bb, t, D_MODEL), lambda g: (g, 0, 0))]
    if add_pos:
        assert bb == 1
        args += list(pos)
        in_specs += [_whole(a) for a in pos]
    args.append(mod_rows)
    if per_batch_mod:
        assert bb == 1
        in_specs.append(pl.BlockSpec((1, 1, N_MOD * D_MODEL), lambda g: (mod_base + g, 0, 0)))
    else:
        in_specs.append(pl.BlockSpec((1, 1, N_MOD * D_MODEL), lambda g: (mod_base, 0, 0)))
    for name in ("g1", "win", "bg"):
        args.append(pw[name])
        in_specs.append(_layer_block(pw[name], l))
    for a in (pw["blk"], pw["dc"][t], pw["ds"][t]):
        args.append(a)
        in_specs.append(_whole(a))
    for name in ("wfn", "wdw", "bdw", "lng", "lnb", "wpw", "gmh"):
        args.append(pw[name])
        in_specs.append(_layer_block(pw[name], l))
    if has_init:
        c0, n0, m0 = state0
        args += [c0, n0, m0]
        in_specs += [
            pl.BlockSpec((bb, 1, N_UNITS, HEAD_DIM_M, HEAD_DIM_M), lambda g: (g, l, 0, 0, 0)),
            pl.BlockSpec((bb, 1, N_UNITS, HEAD_DIM_M), lambda g: (g, l, 0, 0)),
            pl.BlockSpec((bb, 1, N_DIRS, GATE_ROWS, CHUNK), lambda g: (g, l, 0, 0, 0)),
        ]
    args.append(pw["wout"])
    in_specs.append(_layer_block(pw["wout"], l))

    out_shape = [jax.ShapeDtypeStruct((nb, t, D_MODEL), F32)]
    out_specs = [pl.BlockSpec((bb, t, D_MODEL), lambda g: (g, 0, 0), pipeline_mode=pl.Buffered(1))]
    if emit_state:
        out_shape += [
            jax.ShapeDtypeStruct((nb, N_UNITS, HEAD_DIM_M, HEAD_DIM_M), F32),
            jax.ShapeDtypeStruct((nb, N_UNITS, HEAD_DIM_M), F32),
            jax.ShapeDtypeStruct((nb, N_DIRS, GATE_ROWS, CHUNK), F32),
        ]
        out_specs += [
            pl.BlockSpec((bb, N_UNITS, HEAD_DIM_M, HEAD_DIM_M), lambda g: (g, 0, 0, 0)),
            pl.BlockSpec((bb, N_UNITS, HEAD_DIM_M), lambda g: (g, 0, 0)),
            pl.BlockSpec((bb, N_DIRS, GATE_ROWS, CHUNK), lambda g: (g, 0, 0, 0)),
        ]
    scratch = [
        pltpu.VMEM((rows, D_FOURIER), BF16),
        pltpu.VMEM((bb, t + 2 * PAD_LO, D_CONV), F32),
        pltpu.VMEM((SUBLANES - 1, bb, t + SHIFT_ROWS, D_CONV), F32),
        pltpu.VMEM((rows, D_CONV), BF16),
        pltpu.VMEM((n_blk, D_MLSTM, CHUNK), BF16),
        pltpu.VMEM((rows, D_MLSTM), BF16),
        pltpu.VMEM((n_blk, N_HEADS_M, HEAD_AUG, CHUNK), BF16),
        pltpu.VMEM((rows, D_MLSTM), BF16),
        pltpu.VMEM((N_DIRS, 5, n_blk * GATE_ROWS, CHUNK), F32),
        pltpu.VMEM((N_DIRS, n_blk, CHUNK, N_HEADS_M), F32),
        pltpu.VMEM((N_DIRS, n_blk, GATE_ROWS, CHUNK), F32),
        pltpu.VMEM((N_DIRS, n_blk, N_HEADS_M, HEAD_AUG, HEAD_DIM_M), BF16),
        pltpu.VMEM((rows, D_MIX), BF16),
        pltpu.VMEM((N_UNITS, HEAD_AUG, HEAD_DIM_M), F32),
        pltpu.VMEM((N_DIRS, GATE_ROWS, CHUNK), F32),
    ]
    body = functools.partial(_mixer_body, bb, t, has_init, emit_state, add_pos)
    return pl.pallas_call(
        body,
        grid=(nb // bb,),
        in_specs=in_specs,
        out_specs=out_specs,
        out_shape=out_shape,
        scratch_shapes=scratch,
        compiler_params=pltpu.CompilerParams(dimension_semantics=("arbitrary",),
                                             vmem_limit_bytes=VMEM_LIMIT),
        name="mixer_t%d" % t,
    )(*args)


def _route_rows(logits_t, b_router_col):
    scores = _sigmoid(logits_t)
    sel = scores + b_router_col
    sel_r = [sel[e:e + 1, :] for e in range(N_EXPERTS)]
    sc_r = [scores[e:e + 1, :] for e in range(N_EXPERTS)]

    best = None
    best_v = None
    for g in range(N_EXPERT_GROUPS):
        a, b, c, d = sel_r[g * EXPERTS_PER_GROUP:(g + 1) * EXPERTS_PER_GROUP]
        hi1, lo1 = jnp.maximum(a, b), jnp.minimum(a, b)
        hi2, lo2 = jnp.maximum(c, d), jnp.minimum(c, d)
        gs = jnp.maximum(hi1, hi2) + jnp.maximum(jnp.minimum(hi1, hi2), jnp.maximum(lo1, lo2))
        if g == 0:
            best = jnp.zeros(gs.shape, jnp.int32)
            best_v = gs
        else:
            upd = gs > best_v
            best = jnp.where(upd, g, best)
            best_v = jnp.where(upd, gs, best_v)

    def pick(rows, j):
        out = rows[(N_EXPERT_GROUPS - 1) * EXPERTS_PER_GROUP + j]
        for g in range(N_EXPERT_GROUPS - 2, -1, -1):
            out = jnp.where(best == g, rows[g * EXPERTS_PER_GROUP + j], out)
        return out

    s = [pick(sel_r, j) for j in range(EXPERTS_PER_GROUP)]
    sc = [pick(sc_r, j) for j in range(EXPERTS_PER_GROUP)]
    rank = [jnp.zeros(best.shape, jnp.int32) for _ in range(EXPERTS_PER_GROUP)]
    for a in range(EXPERTS_PER_GROUP):
        for b in range(a + 1, EXPERTS_PER_GROUP):
            b_first = s[b] > s[a]
            rank[a] = rank[a] + b_first.astype(jnp.int32)
            rank[b] = rank[b] + (1 - b_first.astype(jnp.int32))
    w = [jnp.where(rank[j] < 2, sc[j], 0.0) for j in range(EXPERTS_PER_GROUP)]
    tot = w[0] + w[1] + w[2] + w[3]
    return best, [wj / tot for wj in w]


def _moe_body(is_last, x_ref, mod_ref, g2_ref, wr2_ref, brc_ref, utri_ref, wg_ref, wu_ref, wd_ref, fg_ref,
              o_ref, he_s, p_s, pt_s, ys_s, sm_s):
    gi = pl.program_id(1)
    tm = ROWS
    n_cap_blk = MOE_CAP // MOE_BLK

    @pl.when(gi == 0)
    def _():
        mod = mod_ref[0]
        sh2 = mod[:, 3 * D_MODEL:4 * D_MODEL]
        sc2 = mod[:, 4 * D_MODEL:5 * D_MODEL]
        h = _rmsnorm_rows(x_ref[...], g2_ref[0]) * (1.0 + sc2) + sh2
        h_hi, h_lo = _hi_lo(h)
        he_s[:, 0:D_MODEL] = h_hi
        lg = _dot(jnp.concatenate([h_hi, h_lo], axis=1), wr2_ref[...])
        lg = lg[:, 0:N_EXPERTS] + lg[:, N_EXPERTS:2 * N_EXPERTS]
        logits_t = jnp.concatenate([lg, jnp.zeros((tm, MOE_EXT - N_EXPERTS), F32)], axis=1).T[0:N_EXPERTS, :]
        best, cw = _route_rows(logits_t, brc_ref[...])

        row_i = lax.broadcasted_iota(jnp.int32, (16, tm), 0)
        onehot = (row_i == best).astype(F32)
        before = _dot(onehot.astype(BF16), utri_ref[...])
        dest = jnp.zeros((1, tm), F32)
        off_blk = jnp.int32(0)
        for g in range(N_EXPERT_GROUPS):
            cnt = jnp.sum(onehot[g:g + 1, :]).astype(jnp.int32)
            n_blk = lax.shift_right_logical(cnt + (MOE_BLK - 1), int(math.log2(MOE_BLK)))
            sm_s[g] = off_blk
            sm_s[N_EXPERT_GROUPS + g] = n_blk
            base = (off_blk * MOE_BLK).astype(F32)
            dest = dest + onehot[g:g + 1, :] * (before[g:g + 1, :] + base)
            off_blk = off_blk + n_blk

        cw_hi = [wj.astype(BF16).astype(F32) for wj in cw]
        cw_lo = [wj - hj for wj, hj in zip(cw, cw_hi)]
        stack = jnp.concatenate([dest] + cw_hi + cw_lo + [jnp.zeros((MOE_EXT - 9, tm), F32)], axis=0)
        he_s[:, D_MODEL:D_MODEL + MOE_EXT] = stack.T.astype(BF16)
        dest_lanes = jnp.broadcast_to(dest, (MOE_BLK, tm)).T

        r_iota = lax.broadcasted_iota(jnp.int32, (MOE_BLK, 1), 0)
        for b in range(n_cap_blk):
            rows_f = (r_iota + b * MOE_BLK).astype(F32)
            p_s[b * MOE_BLK:(b + 1) * MOE_BLK, :] = jnp.where(dest == rows_f, 1.0, 0.0).astype(BF16)
        c_iota = lax.broadcasted_iota(jnp.int32, (1, MOE_BLK), 1)
        for b in range(n_cap_blk):
            cols_f = (c_iota + b * MOE_BLK).astype(F32)
            pt_s[:, b * MOE_BLK:(b + 1) * MOE_BLK] = jnp.where(dest_lanes == cols_f, 1.0, 0.0).astype(BF16)
        ys_s[...] = jnp.zeros(ys_s.shape, BF16)

    first_blk = sm_s[gi]
    n_blk = sm_s[N_EXPERT_GROUPS + gi]

    def block(i, carry):
        r0 = pl.multiple_of((first_blk + i) * MOE_BLK, MOE_BLK)
        xe = _dot(p_s[pl.ds(r0, MOE_BLK), :], he_s[...])
        xs = xe[:, 0:D_MODEL].astype(BF16)
        acts = []
        for j in range(EXPERTS_PER_GROUP):
            gate = _dot(xs, wg_ref[0, j])
            up = _dot(xs, wu_ref[0, j])
            cwj = (xe[:, D_MODEL + 1 + j:D_MODEL + 2 + j]
                   + xe[:, D_MODEL + 1 + EXPERTS_PER_GROUP + j:D_MODEL + 2 + EXPERTS_PER_GROUP + j])
            acts.append((gate * _sigmoid(gate) * up * cwj).astype(BF16))
        ys_s[pl.ds(r0, MOE_BLK), :] = _dot(jnp.concatenate(acts, axis=1), wd_ref[0, 0]).astype(BF16)
        return carry

    lax.fori_loop(0, n_blk, block, 0)

    @pl.when(gi == N_EXPERT_GROUPS - 1)
    def _():
        ga2 = mod_ref[0][:, 5 * D_MODEL:6 * D_MODEL]
        xo = x_ref[...] + ga2 * _dot(pt_s[...], ys_s[...])
        if is_last:
            xo = _rmsnorm_rows(xo, fg_ref[...])
        o_ref[...] = xo


def _moe_call(x2d, mod_rows, mod_base, per_tile_mod, l, pw, is_last):
    n_tok = x2d.shape[0]
    tm = ROWS
    assert n_tok % tm == 0
    if per_tile_mod:
        mod_spec = pl.BlockSpec((1, 1, N_MOD * D_MODEL), lambda i, g: (mod_base + i, 0, 0))
    else:
        mod_spec = pl.BlockSpec((1, 1, N_MOD * D_MODEL), lambda i, g: (mod_base, 0, 0))

    def const(shape):
        return pl.BlockSpec(shape, lambda i, g: (0,) * len(shape), pipeline_mode=pl.Buffered(1))

    def grp(shape):
        return pl.BlockSpec((1, EXPERTS_PER_GROUP) + shape, lambda i, g: (l, g, 0, 0))

    return pl.pallas_call(
        functools.partial(_moe_body, is_last),
        grid=(n_tok // tm, N_EXPERT_GROUPS),
        in_specs=[
            pl.BlockSpec((tm, D_MODEL), lambda i, g: (i, 0)),
            mod_spec,
            pl.BlockSpec((1, 1, D_MODEL), lambda i, g: (l, 0, 0), pipeline_mode=pl.Buffered(1)),
            const((2 * D_MODEL, 2 * N_EXPERTS)),
            const((N_EXPERTS, 1)),
            const((tm, tm)),
            grp((D_MODEL, D_EXPERT)),
            grp((D_MODEL, D_EXPERT)),
            pl.BlockSpec((1, 1, EXPERTS_PER_GROUP * D_EXPERT, D_MODEL), lambda i, g: (l, g, 0, 0)),
            const((1, D_MODEL)),
        ],
        out_specs=pl.BlockSpec((tm, D_MODEL), lambda i, g: (i, 0)),
        out_shape=jax.ShapeDtypeStruct((n_tok, D_MODEL), F32),
        scratch_shapes=[
            pltpu.VMEM((tm, D_MODEL + MOE_EXT), BF16),
            pltpu.VMEM((MOE_CAP, tm), BF16),
            pltpu.VMEM((tm, MOE_CAP), BF16),
            pltpu.VMEM((MOE_CAP, D_MODEL), BF16),
            pltpu.SMEM((2 * N_EXPERT_GROUPS,), jnp.int32),
        ],
        compiler_params=pltpu.CompilerParams(dimension_semantics=("arbitrary", "arbitrary"),
                                             vmem_limit_bytes=VMEM_LIMIT),
        name="moe",
    )(x2d, mod_rows, pw["g2"], pw["wr2"], pw["brc"], pw["utri"], pw["weg"], pw["weu"], pw["wed"], pw["fg"])


def _dft_tables(t):
    idx = np.arange(t, dtype=np.int64)
    ang = 2.0 * np.pi * ((idx[:, None] * idx[None, :]) % t).astype(np.float64) / t
    scale = 1.0 / math.sqrt(t * D_FOURIER_GROUP)
    return (np.cos(ang) * scale).astype(np.float32), (-np.sin(ang) * scale).astype(np.float32)


def _group_tables():
    idx = np.arange(D_FOURIER_GROUP, dtype=np.int64)
    ang = 2.0 * np.pi * ((idx[:, None] * idx[None, :]) % D_FOURIER_GROUP).astype(np.float64) / D_FOURIER_GROUP
    eye = np.eye(N_FOURIER_GROUPS)
    return np.concatenate([np.kron(eye, np.cos(ang)), np.kron(eye, np.sin(ang))], axis=1).astype(np.float32)


def _grid_pos_tables(n_tokens, dtype):
    quarter = D_MODEL // 4
    omega = 1.0 / (10000.0 ** (jnp.arange(quarter, dtype=jnp.float32) / quarter))

    def enc(p):
        a = p[:, None] * omega[None, :]
        return jnp.concatenate([jnp.sin(a), jnp.cos(a)], axis=-1).astype(dtype)

    return (enc(jnp.arange(n_tokens // GRID_W, dtype=jnp.float32)), enc(jnp.arange(GRID_W, dtype=jnp.float32)))


def _router_hi_lo(w_router):
    w_hi = w_router.astype(BF16)
    w_lo = (w_router - w_hi.astype(F32)).astype(BF16)
    return jnp.concatenate([jnp.concatenate([w_hi, w_lo], axis=1),
                            jnp.concatenate([w_hi, jnp.zeros_like(w_lo)], axis=1)], axis=0)


def _prepare(seq_lens, norm1_g, norm2_g, w_in, w_fnet, w_dw, b_dw, conv_ln_g, conv_ln_b, w_pw, b_gate, g_mh,
             w_out, w_router, b_router, w_exp_gate, w_exp_up, w_exp_down, final_g):
    dft = {t: _dft_tables(t) for t in sorted(set(seq_lens))}
    w_in_b = w_in.astype(BF16)
    tok = np.arange(ROWS)
    return dict(
        g1=norm1_g.reshape(DEPTH, 1, D_MODEL), g2=norm2_g.reshape(DEPTH, 1, D_MODEL),
        win=w_in_b, bg=b_gate.reshape(DEPTH, 1, N_GATES),
        blk=jnp.asarray(_group_tables()).astype(BF16),
        dc={t: jnp.asarray(v[0]).astype(BF16) for t, v in dft.items()},
        ds={t: jnp.asarray(v[1]).astype(BF16) for t, v in dft.items()},
        wfn=w_fnet.astype(BF16),
        wdw=jnp.concatenate([w_dw, jnp.zeros((DEPTH, 1, D_CONV), F32)], axis=1),
        bdw=b_dw.reshape(DEPTH, 1, D_CONV), lng=conv_ln_g.reshape(DEPTH, 1, D_CONV),
        lnb=conv_ln_b.reshape(DEPTH, 1, D_CONV), wpw=w_pw.astype(BF16),
        gmh=jnp.broadcast_to(g_mh.reshape(DEPTH, D_MLSTM, 1), (DEPTH, D_MLSTM, CHUNK)),
        wout=w_out.astype(BF16),
        wr2=_router_hi_lo(w_router), brc=b_router.reshape(N_EXPERTS, 1),
        utri=jnp.asarray((tok[:, None] < tok[None, :]).astype(np.float32)).astype(BF16),
        weg=w_exp_gate.astype(BF16), weu=w_exp_up.astype(BF16),
        wed=w_exp_down.reshape(DEPTH, N_EXPERT_GROUPS, EXPERTS_PER_GROUP * D_EXPERT, D_MODEL).astype(BF16),
        fg=final_g.reshape(1, D_MODEL),
    )


def kernel(x_prompt, x_sample, state_C, state_n, state_m, c, c_ctx, w_ada, b_ada, norm1_g, norm2_g, w_in, w_fnet, w_dw, b_dw, conv_ln_g, conv_ln_b, w_pw, b_gate, g_mh, w_out, w_router, b_router, w_exp_gate, w_exp_up, w_exp_down, final_g):
    bp, tp, _ = x_prompt.shape
    bs, ts, _ = x_sample.shape

    cv = jnp.zeros((MOD_ROWS, D_MODEL), F32).at[:bs].set(c).at[bs].set(c_ctx)
    mod_all = _ada_call(cv, w_ada, b_ada)
    mod_rows = mod_all.reshape(DEPTH * MOD_ROWS, 1, N_MOD * D_MODEL)

    pw = _prepare((tp, ts), norm1_g, norm2_g, w_in, w_fnet, w_dw, b_dw, conv_ln_g, conv_ln_b, w_pw, b_gate,
                  g_mh, w_out, w_router, b_router, w_exp_gate, w_exp_up, w_exp_down, final_g)
    pos = _grid_pos_tables(ts, x_sample.dtype)

    c0 = state_C.reshape(bs, DEPTH, N_UNITS, HEAD_DIM_M, HEAD_DIM_M)
    n0 = state_n.reshape(bs, DEPTH, N_UNITS, HEAD_DIM_M)
    m0 = jnp.pad(state_m, ((0, 0), (0, 0), (0, 0), (0, GATE_ROWS - N_HEADS_M)))
    m0 = jnp.broadcast_to(m0[..., None], (bs, DEPTH, N_DIRS, GATE_ROWS, CHUNK))

    xp, xs = x_prompt, x_sample
    cs, ns, ms = [], [], []
    for l in range(DEPTH):
        is_last = l == DEPTH - 1
        base = l * MOD_ROWS
        xp, c_l, n_l, m_l = _mixer_call(xp, None, mod_rows, base + bs, False, l, pw, None, True)
        cs.append(c_l)
        ns.append(n_l)
        ms.append(m_l[:, :, :N_HEADS_M, 0])
        xp = _moe_call(xp.reshape(bp * tp, D_MODEL), mod_rows, base + bs, False, l, pw, is_last)
        xp = xp.reshape(bp, tp, D_MODEL)

        (xs,) = _mixer_call(xs, pos if l == 0 else None, mod_rows, base, True, l, pw, (c0, n0, m0), False)
        xs = _moe_call(xs.reshape(bs * ts, D_MODEL), mod_rows, base, True, l, pw, is_last)
        xs = xs.reshape(bs, ts, D_MODEL)

    new_c = jnp.stack(cs, axis=1).reshape(bp, DEPTH, N_DIRS, N_HEADS_M, HEAD_DIM_M, HEAD_DIM_M)
    new_n = jnp.stack(ns, axis=1).reshape(bp, DEPTH, N_DIRS, N_HEADS_M, HEAD_DIM_M)
    new_m = jnp.stack(ms, axis=1).reshape(bp, DEPTH, N_DIRS, N_HEADS_M)
    return (xp, xs, new_c, new_n, new_m)
```

```python
import functools
import math

import numpy as np
import jax
import jax.numpy as jnp
from jax import lax
from jax.experimental import pallas as pl
from jax.experimental.pallas import tpu as pltpu

D_MODEL = 1024
DEPTH = 4
GRID_W = 64
EPS = 1e-6
D_FOURIER = 256
N_FOURIER_GROUPS = 4
D_FOURIER_GROUP = D_FOURIER // N_FOURIER_GROUPS
D_CONV = 256
CONV_WIDTH = 31
CONV_PAD = CONV_WIDTH // 2
D_MLSTM = 512
N_HEADS_M = 4
HEAD_DIM_M = D_MLSTM // N_HEADS_M
N_DIRS = 2
CHUNK = 128
N_GATES = 2 * N_DIRS * N_HEADS_M
N_UNITS = N_DIRS * N_HEADS_M
D_MIX = D_FOURIER + D_CONV + D_MLSTM
D_IN_PROJ = D_FOURIER + 2 * D_CONV + 4 * D_MLSTM + N_GATES
N_EXPERTS = 16
N_EXPERT_GROUPS = 4
EXPERTS_PER_GROUP = N_EXPERTS // N_EXPERT_GROUPS
D_EXPERT = 256
N_MOD = 6

O_B = D_FOURIER
O_Q = O_B + 2 * D_CONV
O_K = O_Q + D_MLSTM
O_V = O_K + D_MLSTM
O_O = O_V + D_MLSTM
O_G = O_O + D_MLSTM

ROWS = 1024
CONV_ROW_TILE = 64
PAD_LO = 16
MOD_ROWS = 16
SUBLANES = 8
SHIFT_ROWS = 24
AUG = 16
HEAD_AUG = HEAD_DIM_M + AUG
GATE_ROWS = 8
MOE_BLK = 128
MOE_CAP = ROWS + N_EXPERT_GROUPS * MOE_BLK
MOE_EXT = 128
VMEM_LIMIT = 56 * 1024 * 1024

F32 = jnp.float32
BF16 = jnp.bfloat16


def _dot(a, b):
    return jnp.dot(a, b, preferred_element_type=F32)


def _sigmoid(x):
    return 1.0 / (1.0 + jnp.exp(-x))


def _log_sigmoid(x):
    return jnp.minimum(x, 0.0) - jnp.log(1.0 + jnp.exp(-jnp.abs(x)))


def _hi_lo(a_f32):
    hi = a_f32.astype(BF16)
    return hi, (a_f32 - hi.astype(F32)).astype(BF16)


def _split_dot(a_f32, b_bf16):
    hi, lo = _hi_lo(a_f32)
    return _dot(hi, b_bf16) + _dot(lo, b_bf16)


def _split_dot_left(b_bf16, a_f32):
    hi, lo = _hi_lo(a_f32)
    return _dot(b_bf16, hi) + _dot(b_bf16, lo)


def _rmsnorm_rows(x, g):
    ms = jnp.mean(x * x, axis=-1, keepdims=True)
    return x * lax.rsqrt(ms + EPS) * g


def _ada_body(c_ref, w_ref, b_ref, o_ref):
    cv = c_ref[...]
    s_hi, s_lo = _hi_lo(cv * _sigmoid(cv))
    w_hi, w_lo = _hi_lo(w_ref[0])
    both = _dot(jnp.concatenate([s_hi, s_lo], axis=0), w_hi)
    o_ref[0] = both[0:MOD_ROWS] + both[MOD_ROWS:2 * MOD_ROWS] + _dot(s_hi, w_lo) + b_ref[0]


def _ada_call(cv, w_ada, b_ada):
    n_col = N_MOD * D_MODEL
    tn = D_MODEL
    return pl.pallas_call(
        _ada_body,
        grid=(DEPTH, n_col // tn),
        in_specs=[
            pl.BlockSpec((MOD_ROWS, D_MODEL), lambda l, j: (0, 0)),
            pl.BlockSpec((1, D_MODEL, tn), lambda l, j: (l, 0, j)),
            pl.BlockSpec((1, 1, tn), lambda l, j: (l, 0, j)),
        ],
        out_specs=pl.BlockSpec((1, MOD_ROWS, tn), lambda l, j: (l, 0, j)),
        out_shape=jax.ShapeDtypeStruct((DEPTH, MOD_ROWS, n_col), F32),
        compiler_params=pltpu.CompilerParams(dimension_semantics=("arbitrary", "arbitrary")),
        name="ada",
    )(cv, w_ada, b_ada.reshape(DEPTH, 1, n_col))


def _scan_max_lanes(x, reverse):
    lane = lax.broadcasted_iota(jnp.int32, x.shape, 1)
    neg_inf = jnp.float32(-jnp.inf)
    sh = 1
    while sh < CHUNK:
        if reverse:
            y = pltpu.roll(x, CHUNK - sh, axis=1)
            x = jnp.maximum(x, jnp.where(lane < CHUNK - sh, y, neg_inf))
        else:
            y = pltpu.roll(x, sh, axis=1)
            x = jnp.maximum(x, jnp.where(lane >= sh, y, neg_inf))
        sh *= 2
    return x


def _mixer_body(bb, t, has_init, emit_state, add_pos, *refs):
    rows = bb * t
    nc = t // CHUNK
    n_blk = rows // CHUNK
    refs = list(refs)
    x_ref = refs.pop(0)
    if add_pos:
        posr_ref, posc_ref = refs[:2]
        refs = refs[2:]
    (mod_ref, g1_ref, win_ref, bg_ref, blk_ref, dc_ref, ds_ref, wfn_ref,
     wdw_ref, bdw_ref, lng_ref, lnb_ref, wpw_ref, gmh_ref) = refs[:14]
    refs = refs[14:]
    if has_init:
        c0_ref, n0_ref, m0_ref = refs[:3]
        refs = refs[3:]
    wout_ref = refs.pop(0)
    x1_ref = refs.pop(0)
    if emit_state:
        co_ref, no_ref, mo_ref = refs[:3]
        refs = refs[3:]
    (xa_s, pad_s, shf_s, cact_s, qt_s, k_s, vta_s, so_s, rows_s, acol_s, mprev_s, ctab_s,
     mix_s, cta_s, mst_s) = refs

    if add_pos:
        half_d = D_MODEL // 2
        for g in range(t // GRID_W):
            rs = slice(g * GRID_W, (g + 1) * GRID_W)
            emb = jnp.concatenate([jnp.broadcast_to(posr_ref[g:g + 1, :], (GRID_W, half_d)), posc_ref[...]], axis=1)
            x1_ref[0, rs, :] = x_ref[0, rs, :] + emb

    def load_x():
        src = x1_ref if add_pos else x_ref
        return src[...].reshape(rows, D_MODEL)

    def w_in(lo, hi):
        return win_ref[0, :, lo:hi]

    x = load_x()
    mod = mod_ref[0]
    sh1 = mod[:, 0:D_MODEL]
    sc1 = mod[:, D_MODEL:2 * D_MODEL]
    h = _rmsnorm_rows(x, g1_ref[0]) * (1.0 + sc1) + sh1
    hb = h.astype(BF16)

    pb = _dot(hb, w_in(O_B, O_Q))
    u = pb[:, :D_CONV] * _sigmoid(pb[:, D_CONV:])
    zpad = jnp.zeros((PAD_LO, D_CONV), F32)
    for i in range(bb):
        pad_s[i, 0:PAD_LO, :] = zpad
        pad_s[i, PAD_LO:PAD_LO + t, :] = u[i * t:(i + 1) * t]
        pad_s[i, PAD_LO + t:2 * PAD_LO + t, :] = zpad

    for i in range(bb):
        for s8 in range(1, SUBLANES):
            shf_s[s8 - 1, i] = pad_s[i, s8:s8 + t + SHIFT_ROWS, :]

    def conv_tile(seq, r0):
        acc = jnp.broadcast_to(bdw_ref[0], (CONV_ROW_TILE, D_CONV))
        for j in range(CONV_WIDTH):
            q = j + PAD_LO - CONV_PAD
            win_rows = pl.ds(pl.multiple_of(r0 + (q // SUBLANES) * SUBLANES, SUBLANES), CONV_ROW_TILE)
            win = pad_s[seq, win_rows, :] if q % SUBLANES == 0 else shf_s[q % SUBLANES - 1, seq, win_rows, :]
            acc = acc + win * wdw_ref[0, j:j + 1, :]
        mu = jnp.mean(acc, axis=-1, keepdims=True)
        cen = acc - mu
        var = jnp.mean(cen * cen, axis=-1, keepdims=True)
        uf = cen * lax.rsqrt(var + EPS) * lng_ref[0] + lnb_ref[0]
        return (uf * _sigmoid(uf)).astype(BF16)

    xa_s[...] = _dot(hb, w_in(0, O_B)).astype(BF16)
    k_s[...] = (_dot(hb, w_in(O_K, O_V)) * (HEAD_DIM_M ** -0.5)).astype(BF16)
    og = _dot(hb, w_in(O_O, D_IN_PROJ))
    so_s[...] = _sigmoid(og[:, 0:D_MLSTM]).astype(BF16)
    qf = _dot(hb, w_in(O_Q, O_K))
    for b in range(n_blk):
        qt_s[b] = qf[b * CHUNK:(b + 1) * CHUNK, :].T.astype(BF16)
    vf = _dot(hb, w_in(O_V, O_O))
    ones_rows = jnp.where(lax.broadcasted_iota(jnp.int32, (AUG, CHUNK), 0) == 0, 1.0, 0.0).astype(BF16)
    for b in range(n_blk):
        vt = vf[b * CHUNK:(b + 1) * CHUNK, :].T.astype(BF16)
        for hh in range(N_HEADS_M):
            vta_s[b, hh, 0:HEAD_DIM_M, :] = vt[hh * HEAD_DIM_M:(hh + 1) * HEAD_DIM_M, :]
            vta_s[b, hh, HEAD_DIM_M:HEAD_AUG, :] = ones_rows

    r_i = lax.broadcasted_iota(jnp.int32, (CHUNK, CHUNK), 0)
    c_i = lax.broadcasted_iota(jnp.int32, (CHUNK, CHUNK), 1)
    lower = r_i >= c_i
    upper = r_i <= c_i
    tri_lo = lower.astype(BF16)
    tri_up = upper.astype(BF16)
    neg_inf = jnp.float32(-jnp.inf)
    n_gr = n_blk * GATE_ROWS

    gcol = og[:, D_MLSTM:D_MLSTM + N_GATES] + bg_ref[0]
    kind_c = (lax.broadcasted_iota(jnp.int32, gcol.shape, 1) // N_HEADS_M) % 2
    gcol = jnp.where(kind_c == 1, _log_sigmoid(gcol), gcol)
    grow = jnp.concatenate([gcol, jnp.zeros((rows, CHUNK - N_GATES), F32)], axis=1).T[0:N_GATES, :]
    for d in range(N_DIRS):
        g0 = d * 2 * N_HEADS_M
        ll = CHUNK - 1 if d == 0 else 0
        xg = jnp.concatenate([grow[g0:g0 + GATE_ROWS, b * CHUNK:(b + 1) * CHUNK] for b in range(n_blk)], axis=0)
        b_rows = pltpu.roll(_split_dot(xg, tri_up if d == 0 else tri_lo), n_gr - N_HEADS_M, axis=0)
        a_rows = xg - b_rows
        a_run = _scan_max_lanes(a_rows, d == 1)
        b_last = jnp.broadcast_to(b_rows[:, ll:ll + 1], (n_gr, CHUNK))
        a_max = jnp.broadcast_to(a_run[:, ll:ll + 1], (n_gr, CHUNK))
        rows_s[d, 0] = b_rows
        rows_s[d, 1] = a_run
        rows_s[d, 2] = b_last
        rows_s[d, 3] = a_max
        rows_s[d, 4] = jnp.exp(a_rows - a_max)
        for b in range(n_blk):
            gc = gcol[b * CHUNK:(b + 1) * CHUNK, :]
            bcol = _split_dot_left(tri_lo if d == 0 else tri_up, gc)
            acol_s[d, b] = gc[:, g0:g0 + N_HEADS_M] - bcol[:, g0 + N_HEADS_M:g0 + 2 * N_HEADS_M]

    for i in range(bb):
        xa_i = xa_s[i * t:(i + 1) * t, :]
        uu = _dot(xa_i, blk_ref[...])
        y = (_dot(dc_ref[...], uu[:, :D_FOURIER].astype(BF16))
             + _dot(ds_ref[...], uu[:, D_FOURIER:].astype(BF16)))
        mix_s[i * t:(i + 1) * t, 0:D_FOURIER] = _dot(y.astype(BF16), wfn_ref[0]).astype(BF16)

    def state_step(pair, carry):
        seq = (2 * pair) // nc
        c0 = (2 * pair) % nc

        @pl.when(c0 == 0)
        def _():
            if has_init:
                first_row = lax.broadcasted_iota(jnp.int32, (AUG, HEAD_DIM_M), 0) == 0
                for j in range(N_UNITS):
                    cta_s[j, 0:HEAD_DIM_M, :] = c0_ref[seq, 0, j].T
                    cta_s[j, HEAD_DIM_M:HEAD_AUG, :] = jnp.where(first_row, n0_ref[seq, 0, j:j + 1, :], 0.0)
                mst_s[...] = m0_ref[seq, 0]
            else:
                cta_s[...] = jnp.zeros(cta_s.shape, F32)
                mst_s[...] = jnp.zeros(mst_s.shape, F32)

        m_cur = [mst_s[d] for d in range(N_DIRS)]
        steps = []
        for sub in range(2):
            c = c0 + sub
            units = []
            for d in range(N_DIRS):
                blk = seq * nc + (c if d == 0 else nc - 1 - c)
                row0 = pl.multiple_of(blk * CHUNK, CHUNK)
                gr0 = pl.multiple_of(blk * GATE_ROWS, GATE_ROWS)
                b_last = rows_s[d, 2, pl.ds(gr0, GATE_ROWS), :]
                a_max = rows_s[d, 3, pl.ds(gr0, GATE_ROWS), :]
                w_rows = rows_s[d, 4, pl.ds(gr0, GATE_ROWS), :]
                m_prev = m_cur[d]
                mprev_s[d, blk] = m_prev
                m_new = b_last + jnp.maximum(m_prev, a_max)
                m_cur[d] = m_new
                decay = jnp.exp(b_last + m_prev - m_new)
                fac = jnp.exp(a_max + b_last - m_new)
                for hh in range(N_HEADS_M):
                    hs = slice(hh * HEAD_DIM_M, (hh + 1) * HEAD_DIM_M)
                    units.append(dict(d=d, hh=hh, j=d * N_HEADS_M + hh, blk=blk,
                                      kc=k_s[pl.ds(row0, CHUNK), hs], vta=vta_s[blk, hh], w=w_rows[hh:hh + 1, :],
                                      decay=decay[hh:hh + 1, :], fac=fac[hh:hh + 1, :]))
            steps.append(units)
        for d in range(N_DIRS):
            mst_s[d] = m_cur[d]
        for units in steps:
            for un in units:
                vw = (un["vta"].astype(F32) * un["w"]).astype(BF16)
                un["upd"] = _dot(vw, un["kc"])
        for sub in range(2):
            it = 2 * pair + sub
            cact_s[pl.ds(pl.multiple_of(it * CHUNK, CHUNK), CHUNK // 2), :] = conv_tile(seq, (c0 + sub) * CHUNK)
        cta = [cta_s[j] for j in range(N_UNITS)]
        for units in steps:
            for un in units:
                j = un["j"]
                ctab_s[un["d"], un["blk"], un["hh"]] = cta[j].astype(BF16)
                cta[j] = un["decay"] * cta[j] + un["fac"] * un["upd"]
        for j in range(N_UNITS):
            cta_s[j] = cta[j]

        if emit_state:
            @pl.when(c0 == nc - 2)
            def _():
                for j in range(N_UNITS):
                    co_ref[seq, j] = cta_s[j, 0:HEAD_DIM_M, :].T
                    no_ref[seq, j:j + 1, :] = cta_s[j, HEAD_DIM_M:HEAD_DIM_M + 1, :]
                mo_ref[seq] = mst_s[...]
        return carry

    lax.fori_loop(0, n_blk // 2, state_step, 0)

    def chunk_out(pair, carry):
        heads = []
        for blk in (2 * pair, 2 * pair + 1):
            row0 = pl.multiple_of(blk * CHUNK, CHUNK)
            gr0 = pl.multiple_of(blk * GATE_ROWS, GATE_ROWS)
            dirs = []
            for d in range(N_DIRS):
                m_prev = mprev_s[d, blk]
                mm = jnp.maximum(m_prev, rows_s[d, 1, pl.ds(gr0, GATE_ROWS), :])
                floor = jnp.exp(-(rows_s[d, 0, pl.ds(gr0, GATE_ROWS), :] + mm))
                dirs.append(dict(mm=mm, inter=jnp.exp(m_prev - mm), floor=floor,
                                 acol=acol_s[d, blk], mask=upper if d == 0 else lower))
            for hh in range(N_HEADS_M):
                hs = slice(hh * HEAD_DIM_M, (hh + 1) * HEAD_DIM_M)
                hd = dict(hh=hh, hs=hs, blk=blk, row0=row0, dirs=dirs,
                          kc=k_s[pl.ds(row0, CHUNK), hs], qt=qt_s[blk, hs, :], vta=vta_s[blk, hh])
                hd["st"] = _dot(hd["kc"], hd["qt"])
                hd["p1"] = [_dot(ctab_s[d, blk, hh], hd["qt"]) for d in range(N_DIRS)]
                heads.append(hd)
        for hd in heads:
            hh = hd["hh"]
            sm = []
            for dd in hd["dirs"]:
                z = dd["acol"][:, hh:hh + 1] - dd["mm"][hh:hh + 1, :]
                sm.append((hd["st"] * jnp.exp(jnp.where(dd["mask"], z, neg_inf))).astype(BF16))
            hd["p2"] = _dot(hd["vta"], jnp.concatenate(sm, axis=1))
        half = CHUNK // 2
        for blk in (2 * pair, 2 * pair + 1):
            row0 = pl.multiple_of(blk * CHUNK, CHUNK)
            cact_s[pl.ds(row0 + half, half), :] = conv_tile(blk // nc, (blk % nc) * CHUNK + half)
        for b2 in range(2):
            parts = []
            for hd in heads[b2 * N_HEADS_M:(b2 + 1) * N_HEADS_M]:
                hh = hd["hh"]
                hsum = None
                for d, dd in enumerate(hd["dirs"]):
                    numa = dd["inter"][hh:hh + 1, :] * hd["p1"][d] + hd["p2"][:, d * CHUNK:(d + 1) * CHUNK]
                    den = numa[HEAD_DIM_M:HEAD_DIM_M + 1, :]
                    ht = numa[0:HEAD_DIM_M, :] / jnp.maximum(jnp.abs(den), dd["floor"][hh:hh + 1, :])
                    hsum = ht if hsum is None else hsum + ht
                r = lax.rsqrt(jnp.mean(hsum * hsum, axis=0, keepdims=True) + EPS)
                parts.append((hsum * r * gmh_ref[0, hd["hs"], :]).T)
            row0 = heads[b2 * N_HEADS_M]["row0"]
            oc = jnp.concatenate(parts, axis=1) * so_s[pl.ds(row0, CHUNK), :].astype(F32)
            mix_s[pl.ds(row0, CHUNK), D_FOURIER + D_CONV:D_MIX] = oc.astype(BF16)
        return carry

    lax.fori_loop(0, n_blk // 2, chunk_out, 0)
    mix_s[:, D_FOURIER:D_FOURIER + D_CONV] = _dot(cact_s[...], wpw_ref[0]).astype(BF16)

    ga1 = mod_ref[0][:, 2 * D_MODEL:3 * D_MODEL]
    res = _dot(mix_s[...], wout_ref[0])
    x1_ref[...] = (load_x() + ga1 * res).reshape(bb, t, D_MODEL)


def _whole(a):
    nd = a.ndim
    return pl.BlockSpec(a.shape, lambda g: (0,) * nd, pipeline_mode=pl.Buffered(1))


def _layer_block(a, l):
    nd = a.ndim
    return pl.BlockSpec((1,) + a.shape[1:], lambda g: (l,) + (0,) * (nd - 1), pipeline_mode=pl.Buffered(1))


def _mixer_call(x, pos, mod_rows, mod_base, per_batch_mod, l, pw, state0, emit_state):
    nb, t, _ = x.shape
    bb = ROWS // t
    assert bb * t == ROWS and nb % bb == 0 and t % CHUNK == 0
    rows = ROWS
    n_blk = rows // CHUNK
    has_init = state0 is not None
    add_pos = pos is not None

    args = [x]
    in_specs = [pl.BlockSpec((bb, t, D_MODEL), lambda g: (g, 0, 0))]
    if add_pos:
        assert bb == 1
        args += list(pos)
        in_specs += [_whole(a) for a in pos]
    args.append(mod_rows)
    if per_batch_mod:
        assert bb == 1
        in_specs.append(pl.BlockSpec((1, 1, N_MOD * D_MODEL), lambda g: (mod_base + g, 0, 0)))
    else:
        in_specs.append(pl.BlockSpec((1, 1, N_MOD * D_MODEL), lambda g: (mod_base, 0, 0)))
    for name in ("g1", "win", "bg"):
        args.append(pw[name])
        in_specs.append(_layer_block(pw[name], l))
    for a in (pw["blk"], pw["dc"][t], pw["ds"][t]):
        args.append(a)
        in_specs.append(_whole(a))
    for name in ("wfn", "wdw", "bdw", "lng", "lnb", "wpw", "gmh"):
        args.append(pw[name])
        in_specs.append(_layer_block(pw[name], l))
    if has_init:
        c0, n0, m0 = state0
        args += [c0, n0, m0]
        in_specs += [
            pl.BlockSpec((bb, 1, N_UNITS, HEAD_DIM_M, HEAD_DIM_M), lambda g: (g, l, 0, 0, 0)),
            pl.BlockSpec((bb, 1, N_UNITS, HEAD_DIM_M), lambda g: (g, l, 0, 0)),
            pl.BlockSpec((bb, 1, N_DIRS, GATE_ROWS, CHUNK), lambda g: (g, l, 0, 0, 0)),
        ]
    args.append(pw["wout"])
    in_specs.append(_layer_block(pw["wout"], l))

    out_shape = [jax.ShapeDtypeStruct((nb, t, D_MODEL), F32)]
    out_specs = [pl.BlockSpec((bb, t, D_MODEL), lambda g: (g, 0, 0), pipeline_mode=pl.Buffered(1))]
    if emit_state:
        out_shape += [
            jax.ShapeDtypeStruct((nb, N_UNITS, HEAD_DIM_M, HEAD_DIM_M), F32),
            jax.ShapeDtypeStruct((nb, N_UNITS, HEAD_DIM_M), F32),
            jax.ShapeDtypeStruct((nb, N_DIRS, GATE_ROWS, CHUNK), F32),
        ]
        out_specs += [
            pl.BlockSpec((bb, N_UNITS, HEAD_DIM_M, HEAD_DIM_M), lambda g: (g, 0, 0, 0)),
            pl.BlockSpec((bb, N_UNITS, HEAD_DIM_M), lambda g: (g, 0, 0)),
            pl.BlockSpec((bb, N_DIRS, GATE_ROWS, CHUNK), lambda g: (g, 0, 0, 0)),
        ]
    scratch = [
        pltpu.VMEM((rows, D_FOURIER), BF16),
        pltpu.VMEM((bb, t + 2 * PAD_LO, D_CONV), F32),
        pltpu.VMEM((SUBLANES - 1, bb, t + SHIFT_ROWS, D_CONV), F32),
        pltpu.VMEM((rows, D_CONV), BF16),
        pltpu.VMEM((n_blk, D_MLSTM, CHUNK), BF16),
        pltpu.VMEM((rows, D_MLSTM), BF16),
        pltpu.VMEM((n_blk, N_HEADS_M, HEAD_AUG, CHUNK), BF16),
        pltpu.VMEM((rows, D_MLSTM), BF16),
        pltpu.VMEM((N_DIRS, 5, n_blk * GATE_ROWS, CHUNK), F32),
        pltpu.VMEM((N_DIRS, n_blk, CHUNK, N_HEADS_M), F32),
        pltpu.VMEM((N_DIRS, n_blk, GATE_ROWS, CHUNK), F32),
        pltpu.VMEM((N_DIRS, n_blk, N_HEADS_M, HEAD_AUG, HEAD_DIM_M), BF16),
        pltpu.VMEM((rows, D_MIX), BF16),
        pltpu.VMEM((N_UNITS, HEAD_AUG, HEAD_DIM_M), F32),
        pltpu.VMEM((N_DIRS, GATE_ROWS, CHUNK), F32),
    ]
    body = functools.partial(_mixer_body, bb, t, has_init, emit_state, add_pos)
    return pl.pallas_call(
        body,
        grid=(nb // bb,),
        in_specs=in_specs,
        out_specs=out_specs,
        out_shape=out_shape,
        scratch_shapes=scratch,
        compiler_params=pltpu.CompilerParams(dimension_semantics=("arbitrary",),
                                             vmem_limit_bytes=VMEM_LIMIT),
        name="mixer_t%d" % t,
    )(*args)


def _route_rows(logits_t, b_router_col):
    scores = _sigmoid(logits_t)
    sel = scores + b_router_col
    sel_r = [sel[e:e + 1, :] for e in range(N_EXPERTS)]
    sc_r = [scores[e:e + 1, :] for e in range(N_EXPERTS)]

    best = None
    best_v = None
    for g in range(N_EXPERT_GROUPS):
        a, b, c, d = sel_r[g * EXPERTS_PER_GROUP:(g + 1) * EXPERTS_PER_GROUP]
        hi1, lo1 = jnp.maximum(a, b), jnp.minimum(a, b)
        hi2, lo2 = jnp.maximum(c, d), jnp.minimum(c, d)
        gs = jnp.maximum(hi1, hi2) + jnp.maximum(jnp.minimum(hi1, hi2), jnp.maximum(lo1, lo2))
        if g == 0:
            best = jnp.zeros(gs.shape, jnp.int32)
            best_v = gs
        else:
            upd = gs > best_v
            best = jnp.where(upd, g, best)
            best_v = jnp.where(upd, gs, best_v)

    def pick(rows, j):
        out = rows[(N_EXPERT_GROUPS - 1) * EXPERTS_PER_GROUP + j]
        for g in range(N_EXPERT_GROUPS - 2, -1, -1):
            out = jnp.where(best == g, rows[g * EXPERTS_PER_GROUP + j], out)
        return out

    s = [pick(sel_r, j) for j in range(EXPERTS_PER_GROUP)]
    sc = [pick(sc_r, j) for j in range(EXPERTS_PER_GROUP)]
    rank = [jnp.zeros(best.shape, jnp.int32) for _ in range(EXPERTS_PER_GROUP)]
    for a in range(EXPERTS_PER_GROUP):
        for b in range(a + 1, EXPERTS_PER_GROUP):
            b_first = s[b] > s[a]
            rank[a] = rank[a] + b_first.astype(jnp.int32)
            rank[b] = rank[b] + (1 - b_first.astype(jnp.int32))
    w = [jnp.where(rank[j] < 2, sc[j], 0.0) for j in range(EXPERTS_PER_GROUP)]
    tot = w[0] + w[1] + w[2] + w[3]
    return best, [wj / tot for wj in w]


def _moe_body(is_last, x_ref, mod_ref, g2_ref, wr2_ref, brc_ref, utri_ref, wg_ref, wu_ref, wd_ref, fg_ref,
              o_ref, he_s, dest_s, destl_s, pt_s, ys_s, sm_s):
    gi = pl.program_id(1)
    tm = ROWS

    @pl.when(gi == 0)
    def _():
        mod = mod_ref[0]
        sh2 = mod[:, 3 * D_MODEL:4 * D_MODEL]
        sc2 = mod[:, 4 * D_MODEL:5 * D_MODEL]
        h = _rmsnorm_rows(x_ref[...], g2_ref[0]) * (1.0 + sc2) + sh2
        h_hi, h_lo = _hi_lo(h)
        he_s[:, 0:D_MODEL] = h_hi
        lg = _dot(jnp.concatenate([h_hi, h_lo], axis=1), wr2_ref[...])
        lg = lg[:, 0:N_EXPERTS] + lg[:, N_EXPERTS:2 * N_EXPERTS]
        logits_t = jnp.concatenate([lg, jnp.zeros((tm, MOE_EXT - N_EXPERTS), F32)], axis=1).T[0:N_EXPERTS, :]
        best, cw = _route_rows(logits_t, brc_ref[...])

        row_i = lax.broadcasted_iota(jnp.int32, (16, tm), 0)
        onehot = (row_i == best).astype(F32)
        before = _dot(onehot.astype(BF16), utri_ref[...])
        dest = jnp.zeros((1, tm), F32)
        off_blk = jnp.int32(0)
        for g in range(N_EXPERT_GROUPS):
            cnt = jnp.sum(onehot[g:g + 1, :]).astype(jnp.int32)
            n_blk = lax.shift_right_logical(cnt + (MOE_BLK - 1), int(math.log2(MOE_BLK)))
            sm_s[g] = off_blk
            sm_s[N_EXPERT_GROUPS + g] = n_blk
            base = (off_blk * MOE_BLK).astype(F32)
            dest = dest + onehot[g:g + 1, :] * (before[g:g + 1, :] + base)
            off_blk = off_blk + n_blk

        cw_hi = [wj.astype(BF16).astype(F32) for wj in cw]
        cw_lo = [wj - hj for wj, hj in zip(cw, cw_hi)]
        stack = jnp.concatenate([dest] + cw_hi + cw_lo + [jnp.zeros((MOE_EXT - 9, tm), F32)], axis=0)
        he_s[:, D_MODEL:D_MODEL + MOE_EXT] = stack.T.astype(BF16)
        dest_s[...] = jnp.broadcast_to(dest, (SUBLANES, tm))
        destl_s[...] = jnp.broadcast_to(dest, (MOE_BLK, tm)).T
        pt_s[...] = jnp.zeros(pt_s.shape, BF16)
        ys_s[...] = jnp.zeros(ys_s.shape, BF16)

    first_blk = sm_s[gi]
    n_blk = sm_s[N_EXPERT_GROUPS + gi]

    def block(i, carry):
        r0 = pl.multiple_of((first_blk + i) * MOE_BLK, MOE_BLK)
        rows_f = (lax.broadcasted_iota(jnp.int32, (MOE_BLK, 1), 0) + r0).astype(F32)
        p_blk = jnp.where(dest_s[0:1, :] == rows_f, 1.0, 0.0).astype(BF16)
        cols_f = (lax.broadcasted_iota(jnp.int32, (1, MOE_BLK), 1) + r0).astype(F32)
        pt_s[:, pl.ds(r0, MOE_BLK)] = jnp.where(destl_s[...] == cols_f, 1.0, 0.0).astype(BF16)
        xe = _dot(p_blk, he_s[...])
        xs = xe[:, 0:D_MODEL].astype(BF16)
        acts = []
        for j in range(EXPERTS_PER_GROUP):
            gate = _dot(xs, wg_ref[0, j])
            up = _dot(xs, wu_ref[0, j])
            cwj = (xe[:, D_MODEL + 1 + j:D_MODEL + 2 + j]
                   + xe[:, D_MODEL + 1 + EXPERTS_PER_GROUP + j:D_MODEL + 2 + EXPERTS_PER_GROUP + j])
            acts.append((gate * _sigmoid(gate) * up * cwj).astype(BF16))
        ys_s[pl.ds(r0, MOE_BLK), :] = _dot(jnp.concatenate(acts, axis=1), wd_ref[0, 0]).astype(BF16)
        return carry

    lax.fori_loop(0, n_blk, block, 0)

    @pl.when(gi == N_EXPERT_GROUPS - 1)
    def _():
        ga2 = mod_ref[0][:, 5 * D_MODEL:6 * D_MODEL]
        xo = x_ref[...] + ga2 * _dot(pt_s[...], ys_s[...])
        if is_last:
            xo = _rmsnorm_rows(xo, fg_ref[...])
        o_ref[...] = xo


def _moe_call(x2d, mod_rows, mod_base, per_tile_mod, l, pw, is_last):
    n_tok = x2d.shape[0]
    tm = ROWS
    assert n_tok % tm == 0
    if per_tile_mod:
        mod_spec = pl.BlockSpec((1, 1, N_MOD * D_MODEL), lambda i, g: (mod_base + i, 0, 0))
    else:
        mod_spec = pl.BlockSpec((1, 1, N_MOD * D_MODEL), lambda i, g: (mod_base, 0, 0))

    def const(shape):
        return pl.BlockSpec(shape, lambda i, g: (0,) * len(shape), pipeline_mode=pl.Buffered(1))

    def grp(shape):
        return pl.BlockSpec((1, EXPERTS_PER_GROUP) + shape, lambda i, g: (l, g, 0, 0))

    return pl.pallas_call(
        functools.partial(_moe_body, is_last),
        grid=(n_tok // tm, N_EXPERT_GROUPS),
        in_specs=[
            pl.BlockSpec((tm, D_MODEL), lambda i, g: (i, 0)),
            mod_spec,
            pl.BlockSpec((1, 1, D_MODEL), lambda i, g: (l, 0, 0), pipeline_mode=pl.Buffered(1)),
            const((2 * D_MODEL, 2 * N_EXPERTS)),
            const((N_EXPERTS, 1)),
            const((tm, tm)),
            grp((D_MODEL, D_EXPERT)),
            grp((D_MODEL, D_EXPERT)),
            pl.BlockSpec((1, 1, EXPERTS_PER_GROUP * D_EXPERT, D_MODEL), lambda i, g: (l, g, 0, 0)),
            const((1, D_MODEL)),
        ],
        out_specs=pl.BlockSpec((tm, D_MODEL), lambda i, g: (i, 0)),
        out_shape=jax.ShapeDtypeStruct((n_tok, D_MODEL), F32),
        scratch_shapes=[
            pltpu.VMEM((tm, D_MODEL + MOE_EXT), BF16),
            pltpu.VMEM((SUBLANES, tm), F32),
            pltpu.VMEM((tm, MOE_BLK), F32),
            pltpu.VMEM((tm, MOE_CAP), BF16),
            pltpu.VMEM((MOE_CAP, D_MODEL), BF16),
            pltpu.SMEM((2 * N_EXPERT_GROUPS,), jnp.int32),
        ],
        compiler_params=pltpu.CompilerParams(dimension_semantics=("arbitrary", "arbitrary"),
                                             vmem_limit_bytes=VMEM_LIMIT),
        name="moe",
    )(x2d, mod_rows, pw["g2"], pw["wr2"], pw["brc"], pw["utri"], pw["weg"], pw["weu"], pw["wed"], pw["fg"])


def _dft_tables(t):
    idx = np.arange(t, dtype=np.int64)
    ang = 2.0 * np.pi * ((idx[:, None] * idx[None, :]) % t).astype(np.float64) / t
    scale = 1.0 / math.sqrt(t * D_FOURIER_GROUP)
    return (np.cos(ang) * scale).astype(np.float32), (-np.sin(ang) * scale).astype(np.float32)


def _group_tables():
    idx = np.arange(D_FOURIER_GROUP, dtype=np.int64)
    ang = 2.0 * np.pi * ((idx[:, None] * idx[None, :]) % D_FOURIER_GROUP).astype(np.float64) / D_FOURIER_GROUP
    eye = np.eye(N_FOURIER_GROUPS)
    return np.concatenate([np.kron(eye, np.cos(ang)), np.kron(eye, np.sin(ang))], axis=1).astype(np.float32)


def _grid_pos_tables(n_tokens, dtype):
    quarter = D_MODEL // 4
    omega = 1.0 / (10000.0 ** (jnp.arange(quarter, dtype=jnp.float32) / quarter))

    def enc(p):
        a = p[:, None] * omega[None, :]
        return jnp.concatenate([jnp.sin(a), jnp.cos(a)], axis=-1).astype(dtype)

    return (enc(jnp.arange(n_tokens // GRID_W, dtype=jnp.float32)), enc(jnp.arange(GRID_W, dtype=jnp.float32)))


def _router_hi_lo(w_router):
    w_hi = w_router.astype(BF16)
    w_lo = (w_router - w_hi.astype(F32)).astype(BF16)
    return jnp.concatenate([jnp.concatenate([w_hi, w_lo], axis=1),
                            jnp.concatenate([w_hi, jnp.zeros_like(w_lo)], axis=1)], axis=0)


def _prepare(seq_lens, norm1_g, norm2_g, w_in, w_fnet, w_dw, b_dw, conv_ln_g, conv_ln_b, w_pw, b_gate, g_mh,
             w_out, w_router, b_router, w_exp_gate, w_exp_up, w_exp_down, final_g):
    dft = {t: _dft_tables(t) for t in sorted(set(seq_lens))}
    w_in_b = w_in.astype(BF16)
    tok = np.arange(ROWS)
    return dict(
        g1=norm1_g.reshape(DEPTH, 1, D_MODEL), g2=norm2_g.reshape(DEPTH, 1, D_MODEL),
        win=w_in_b, bg=b_gate.reshape(DEPTH, 1, N_GATES),
        blk=jnp.asarray(_group_tables()).astype(BF16),
        dc={t: jnp.asarray(v[0]).astype(BF16) for t, v in dft.items()},
        ds={t: jnp.asarray(v[1]).astype(BF16) for t, v in dft.items()},
        wfn=w_fnet.astype(BF16),
        wdw=jnp.concatenate([w_dw, jnp.zeros((DEPTH, 1, D_CONV), F32)], axis=1),
        bdw=b_dw.reshape(DEPTH, 1, D_CONV), lng=conv_ln_g.reshape(DEPTH, 1, D_CONV),
        lnb=conv_ln_b.reshape(DEPTH, 1, D_CONV), wpw=w_pw.astype(BF16),
        gmh=jnp.broadcast_to(g_mh.reshape(DEPTH, D_MLSTM, 1), (DEPTH, D_MLSTM, CHUNK)),
        wout=w_out.astype(BF16),
        wr2=_router_hi_lo(w_router), brc=b_router.reshape(N_EXPERTS, 1),
        utri=jnp.asarray((tok[:, None] < tok[None, :]).astype(np.float32)).astype(BF16),
        weg=w_exp_gate.astype(BF16), weu=w_exp_up.astype(BF16),
        wed=w_exp_down.reshape(DEPTH, N_EXPERT_GROUPS, EXPERTS_PER_GROUP * D_EXPERT, D_MODEL).astype(BF16),
        fg=final_g.reshape(1, D_MODEL),
    )


def kernel(x_prompt, x_sample, state_C, state_n, state_m, c, c_ctx, w_ada, b_ada, norm1_g, norm2_g, w_in, w_fnet, w_dw, b_dw, conv_ln_g, conv_ln_b, w_pw, b_gate, g_mh, w_out, w_router, b_router, w_exp_gate, w_exp_up, w_exp_down, final_g):
    bp, tp, _ = x_prompt.shape
    bs, ts, _ = x_sample.shape

    cv = jnp.zeros((MOD_ROWS, D_MODEL), F32).at[:bs].set(c).at[bs].set(c_ctx)
    mod_all = _ada_call(cv, w_ada, b_ada)
    mod_rows = mod_all.reshape(DEPTH * MOD_ROWS, 1, N_MOD * D_MODEL)

    pw = _prepare((tp, ts), norm1_g, norm2_g, w_in, w_fnet, w_dw, b_dw, conv_ln_g, conv_ln_b, w_pw, b_gate,
                  g_mh, w_out, w_router, b_router, w_exp_gate, w_exp_up, w_exp_down, final_g)
    pos = _grid_pos_tables(ts, x_sample.dtype)

    c0 = state_C.reshape(bs, DEPTH, N_UNITS, HEAD_DIM_M, HEAD_DIM_M)
    n0 = state_n.reshape(bs, DEPTH, N_UNITS, HEAD_DIM_M)
    m0 = jnp.pad(state_m, ((0, 0), (0, 0), (0, 0), (0, GATE_ROWS - N_HEADS_M)))
    m0 = jnp.broadcast_to(m0[..., None], (bs, DEPTH, N_DIRS, GATE_ROWS, CHUNK))

    xp, xs = x_prompt, x_sample
    cs, ns, ms = [], [], []
    for l in range(DEPTH):
        is_last = l == DEPTH - 1
        base = l * MOD_ROWS
        xp, c_l, n_l, m_l = _mixer_call(xp, None, mod_rows, base + bs, False, l, pw, None, True)
        cs.append(c_l)
        ns.append(n_l)
        ms.append(m_l[:, :, :N_HEADS_M, 0])
        xp = _moe_call(xp.reshape(bp * tp, D_MODEL), mod_rows, base + bs, False, l, pw, is_last)
        xp = xp.reshape(bp, tp, D_MODEL)

        (xs,) = _mixer_call(xs, pos if l == 0 else None, mod_rows, base, True, l, pw, (c0, n0, m0), False)
        xs = _moe_call(xs.reshape(bs * ts, D_MODEL), mod_rows, base, True, l, pw, is_last)
        xs = xs.reshape(bs, ts, D_MODEL)

    new_c = jnp.stack(cs, axis=1).reshape(bp, DEPTH, N_DIRS, N_HEADS_M, HEAD_DIM_M, HEAD_DIM_M)
    new_n = jnp.stack(ns, axis=1).reshape(bp, DEPTH, N_DIRS, N_HEADS_M, HEAD_DIM_M)
    new_m = jnp.stack(ms, axis=1).reshape(bp, DEPTH, N_DIRS, N_HEADS_M)
    return (xp, xs, new_c, new_n, new_m)
```

```python
import functools
import math

import numpy as np
import jax
import jax.numpy as jnp
from jax import lax
from jax.experimental import pallas as pl
from jax.experimental.pallas import tpu as pltpu

D_MODEL = 1024
DEPTH = 4
GRID_W = 64
EPS = 1e-6
D_FOURIER = 256
N_FOURIER_GROUPS = 4
D_FOURIER_GROUP = D_FOURIER // N_FOURIER_GROUPS
D_CONV = 256
CONV_WIDTH = 31
CONV_PAD = CONV_WIDTH // 2
D_MLSTM = 512
N_HEADS_M = 4
HEAD_DIM_M = D_MLSTM // N_HEADS_M
N_DIRS = 2
CHUNK = 128
N_GATES = 2 * N_DIRS * N_HEADS_M
N_UNITS = N_DIRS * N_HEADS_M
D_MIX = D_FOURIER + D_CONV + D_MLSTM
D_IN_PROJ = D_FOURIER + 2 * D_CONV + 4 * D_MLSTM + N_GATES
N_EXPERTS = 16
N_EXPERT_GROUPS = 4
EXPERTS_PER_GROUP = N_EXPERTS // N_EXPERT_GROUPS
D_EXPERT = 256
N_MOD = 6

O_B = D_FOURIER
O_Q = O_B + 2 * D_CONV
O_K = O_Q + D_MLSTM
O_V = O_K + D_MLSTM
O_O = O_V + D_MLSTM
O_G = O_O + D_MLSTM

ROWS = 1024
CONV_ROW_TILE = 64
PAD_LO = 16
MOD_ROWS = 16
SUBLANES = 8
SHIFT_ROWS = 24
AUG = 16
HEAD_AUG = HEAD_DIM_M + AUG
GATE_ROWS = 8
MOE_BLK = 128
MOE_CAP = ROWS + N_EXPERT_GROUPS * MOE_BLK
MOE_EXT = 128
VMEM_LIMIT = 60 * 1024 * 1024

F32 = jnp.float32
BF16 = jnp.bfloat16


def _dot(a, b):
    return jnp.dot(a, b, preferred_element_type=F32)


def _sigmoid(x):
    return 1.0 / (1.0 + jnp.exp(-x))


def _log_sigmoid(x):
    return jnp.minimum(x, 0.0) - jnp.log(1.0 + jnp.exp(-jnp.abs(x)))


def _hi_lo(a_f32):
    hi = a_f32.astype(BF16)
    return hi, (a_f32 - hi.astype(F32)).astype(BF16)


def _split_dot(a_f32, b_bf16):
    hi, lo = _hi_lo(a_f32)
    return _dot(hi, b_bf16) + _dot(lo, b_bf16)


def _split_dot_left(b_bf16, a_f32):
    hi, lo = _hi_lo(a_f32)
    return _dot(b_bf16, hi) + _dot(b_bf16, lo)


def _rmsnorm_rows(x, g):
    ms = jnp.mean(x * x, axis=-1, keepdims=True)
    return x * lax.rsqrt(ms + EPS) * g


def _ada_body(c_ref, w_ref, b_ref, o_ref):
    cv = c_ref[...]
    s_hi, s_lo = _hi_lo(cv * _sigmoid(cv))
    w_hi, w_lo = _hi_lo(w_ref[0])
    both = _dot(jnp.concatenate([s_hi, s_lo], axis=0), w_hi)
    o_ref[0] = both[0:MOD_ROWS] + both[MOD_ROWS:2 * MOD_ROWS] + _dot(s_hi, w_lo) + b_ref[0]


def _ada_call(cv, w_ada, b_ada):
    n_col = N_MOD * D_MODEL
    tn = D_MODEL
    return pl.pallas_call(
        _ada_body,
        grid=(DEPTH, n_col // tn),
        in_specs=[
            pl.BlockSpec((MOD_ROWS, D_MODEL), lambda l, j: (0, 0)),
            pl.BlockSpec((1, D_MODEL, tn), lambda l, j: (l, 0, j)),
            pl.BlockSpec((1, 1, tn), lambda l, j: (l, 0, j)),
        ],
        out_specs=pl.BlockSpec((1, MOD_ROWS, tn), lambda l, j: (l, 0, j)),
        out_shape=jax.ShapeDtypeStruct((DEPTH, MOD_ROWS, n_col), F32),
        compiler_params=pltpu.CompilerParams(dimension_semantics=("arbitrary", "arbitrary")),
        name="ada",
    )(cv, w_ada, b_ada.reshape(DEPTH, 1, n_col))


def _scan_max_lanes(x, reverse):
    lane = lax.broadcasted_iota(jnp.int32, x.shape, 1)
    neg_inf = jnp.float32(-jnp.inf)
    sh = 1
    while sh < CHUNK:
        if reverse:
            y = pltpu.roll(x, CHUNK - sh, axis=1)
            x = jnp.maximum(x, jnp.where(lane < CHUNK - sh, y, neg_inf))
        else:
            y = pltpu.roll(x, sh, axis=1)
            x = jnp.maximum(x, jnp.where(lane >= sh, y, neg_inf))
        sh *= 2
    return x


def _mixer_body(bb, t, has_init, emit_state, add_pos, *refs):
    rows = bb * t
    nc = t // CHUNK
    n_blk = rows // CHUNK
    refs = list(refs)
    x_ref = refs.pop(0)
    if add_pos:
        posr_ref, posc_ref = refs[:2]
        refs = refs[2:]
    (mod_ref, g1_ref, win_ref, bg_ref, blk_ref, dc_ref, ds_ref, wfn_ref,
     wdw_ref, bdw_ref, lng_ref, lnb_ref, wpw_ref, gmh_ref) = refs[:14]
    refs = refs[14:]
    if has_init:
        c0_ref, n0_ref, m0_ref = refs[:3]
        refs = refs[3:]
    wout_ref = refs.pop(0)
    x1_ref = refs.pop(0)
    if emit_state:
        co_ref, no_ref, mo_ref = refs[:3]
        refs = refs[3:]
    (xa_s, pad_s, shf_s, cact_s, qt_s, k_s, vta_s, so_s, rows_s, acol_s, mprev_s, ctab_s,
     mix_s, cta_s, mst_s) = refs

    if add_pos:
        half_d = D_MODEL // 2
        for g in range(t // GRID_W):
            rs = slice(g * GRID_W, (g + 1) * GRID_W)
            emb = jnp.concatenate([jnp.broadcast_to(posr_ref[g:g + 1, :], (GRID_W, half_d)), posc_ref[...]], axis=1)
            x1_ref[0, rs, :] = x_ref[0, rs, :] + emb

    def load_x():
        src = x1_ref if add_pos else x_ref
        return src[...].reshape(rows, D_MODEL)

    def w_in(lo, hi):
        return win_ref[0, :, lo:hi]

    x = load_x()
    mod = mod_ref[0]
    sh1 = mod[:, 0:D_MODEL]
    sc1 = mod[:, D_MODEL:2 * D_MODEL]
    h = _rmsnorm_rows(x, g1_ref[0]) * (1.0 + sc1) + sh1
    hb = h.astype(BF16)

    pb = _dot(hb, w_in(O_B, O_Q))
    u = pb[:, :D_CONV] * _sigmoid(pb[:, D_CONV:])
    zpad = jnp.zeros((PAD_LO, D_CONV), F32)
    for i in range(bb):
        pad_s[i, 0:PAD_LO, :] = zpad
        pad_s[i, PAD_LO:PAD_LO + t, :] = u[i * t:(i + 1) * t]
        pad_s[i, PAD_LO + t:2 * PAD_LO + t, :] = zpad

    for i in range(bb):
        for s8 in range(1, SUBLANES):
            shf_s[s8 - 1, i] = pad_s[i, s8:s8 + t + SHIFT_ROWS, :]

    def conv_tile(seq, r0):
        acc = jnp.broadcast_to(bdw_ref[0], (CONV_ROW_TILE, D_CONV))
        for j in range(CONV_WIDTH):
            q = j + PAD_LO - CONV_PAD
            win_rows = pl.ds(pl.multiple_of(r0 + (q // SUBLANES) * SUBLANES, SUBLANES), CONV_ROW_TILE)
            win = pad_s[seq, win_rows, :] if q % SUBLANES == 0 else shf_s[q % SUBLANES - 1, seq, win_rows, :]
            acc = acc + win * wdw_ref[0, j:j + 1, :]
        mu = jnp.mean(acc, axis=-1, keepdims=True)
        cen = acc - mu
        var = jnp.mean(cen * cen, axis=-1, keepdims=True)
        uf = cen * lax.rsqrt(var + EPS) * lng_ref[0] + lnb_ref[0]
        return (uf * _sigmoid(uf)).astype(BF16)

    xa_s[...] = _dot(hb, w_in(0, O_B)).astype(BF16)
    k_s[...] = (_dot(hb, w_in(O_K, O_V)) * (HEAD_DIM_M ** -0.5)).astype(BF16)
    og = _dot(hb, w_in(O_O, D_IN_PROJ))
    so_s[...] = _sigmoid(og[:, 0:D_MLSTM]).astype(BF16)
    qf = _dot(hb, w_in(O_Q, O_K))
    for b in range(n_blk):
        qt_s[b] = qf[b * CHUNK:(b + 1) * CHUNK, :].T.astype(BF16)
    vf = _dot(hb, w_in(O_V, O_O))
    ones_rows = jnp.where(lax.broadcasted_iota(jnp.int32, (AUG, CHUNK), 0) == 0, 1.0, 0.0).astype(BF16)
    for b in range(n_blk):
        vt = vf[b * CHUNK:(b + 1) * CHUNK, :].T.astype(BF16)
        for hh in range(N_HEADS_M):
            vta_s[b, hh, 0:HEAD_DIM_M, :] = vt[hh * HEAD_DIM_M:(hh + 1) * HEAD_DIM_M, :]
            vta_s[b, hh, HEAD_DIM_M:HEAD_AUG, :] = ones_rows

    r_i = lax.broadcasted_iota(jnp.int32, (CHUNK, CHUNK), 0)
    c_i = lax.broadcasted_iota(jnp.int32, (CHUNK, CHUNK), 1)
    lower = r_i >= c_i
    upper = r_i <= c_i
    tri_lo = lower.astype(BF16)
    tri_up = upper.astype(BF16)
    neg_inf = jnp.float32(-jnp.inf)
    n_gr = n_blk * GATE_ROWS

    gcol = og[:, D_MLSTM:D_MLSTM + N_GATES] + bg_ref[0]
    kind_c = (lax.broadcasted_iota(jnp.int32, gcol.shape, 1) // N_HEADS_M) % 2
    gcol = jnp.where(kind_c == 1, _log_sigmoid(gcol), gcol)
    grow = jnp.concatenate([gcol, jnp.zeros((rows, CHUNK - N_GATES), F32)], axis=1).T[0:N_GATES, :]
    for d in range(N_DIRS):
        g0 = d * 2 * N_HEADS_M
        ll = CHUNK - 1 if d == 0 else 0
        xg = jnp.concatenate([grow[g0:g0 + GATE_ROWS, b * CHUNK:(b + 1) * CHUNK] for b in range(n_blk)], axis=0)
        b_rows = pltpu.roll(_split_dot(xg, tri_up if d == 0 else tri_lo), n_gr - N_HEADS_M, axis=0)
        a_rows = xg - b_rows
        a_run = _scan_max_lanes(a_rows, d == 1)
        b_last = jnp.broadcast_to(b_rows[:, ll:ll + 1], (n_gr, CHUNK))
        a_max = jnp.broadcast_to(a_run[:, ll:ll + 1], (n_gr, CHUNK))
        rows_s[d, 0] = b_rows
        rows_s[d, 1] = a_run
        rows_s[d, 2] = b_last
        rows_s[d, 3] = a_max
        rows_s[d, 4] = jnp.exp(a_rows - a_max)
        for b in range(n_blk):
            gc = gcol[b * CHUNK:(b + 1) * CHUNK, :]
            bcol = _split_dot_left(tri_lo if d == 0 else tri_up, gc)
            acol_s[d, b] = gc[:, g0:g0 + N_HEADS_M] - bcol[:, g0 + N_HEADS_M:g0 + 2 * N_HEADS_M]

    for i in range(bb):
        xa_i = xa_s[i * t:(i + 1) * t, :]
        uu = _dot(xa_i, blk_ref[...])
        y = (_dot(dc_ref[...], uu[:, :D_FOURIER].astype(BF16))
             + _dot(ds_ref[...], uu[:, D_FOURIER:].astype(BF16)))
        mix_s[i * t:(i + 1) * t, 0:D_FOURIER] = _dot(y.astype(BF16), wfn_ref[0]).astype(BF16)

    def state_step(pair, carry):
        seq = (2 * pair) // nc
        c0 = (2 * pair) % nc

        @pl.when(c0 == 0)
        def _():
            if has_init:
                first_row = lax.broadcasted_iota(jnp.int32, (AUG, HEAD_DIM_M), 0) == 0
                for j in range(N_UNITS):
                    cta_s[j, 0:HEAD_DIM_M, :] = c0_ref[seq, 0, j].T
                    cta_s[j, HEAD_DIM_M:HEAD_AUG, :] = jnp.where(first_row, n0_ref[seq, 0, j:j + 1, :], 0.0)
                mst_s[...] = m0_ref[seq, 0]
            else:
                cta_s[...] = jnp.zeros(cta_s.shape, F32)
                mst_s[...] = jnp.zeros(mst_s.shape, F32)

        m_cur = [mst_s[d] for d in range(N_DIRS)]
        steps = []
        for sub in range(2):
            c = c0 + sub
            units = []
            for d in range(N_DIRS):
                blk = seq * nc + (c if d == 0 else nc - 1 - c)
                row0 = pl.multiple_of(blk * CHUNK, CHUNK)
                gr0 = pl.multiple_of(blk * GATE_ROWS, GATE_ROWS)
                b_last = rows_s[d, 2, pl.ds(gr0, GATE_ROWS), :]
                a_max = rows_s[d, 3, pl.ds(gr0, GATE_ROWS), :]
                w_rows = rows_s[d, 4, pl.ds(gr0, GATE_ROWS), :]
                m_prev = m_cur[d]
                mprev_s[d, blk] = m_prev
                m_new = b_last + jnp.maximum(m_prev, a_max)
                m_cur[d] = m_new
                decay = jnp.exp(b_last + m_prev - m_new)
                fac = jnp.exp(a_max + b_last - m_new)
                for hh in range(N_HEADS_M):
                    hs = slice(hh * HEAD_DIM_M, (hh + 1) * HEAD_DIM_M)
                    units.append(dict(d=d, hh=hh, j=d * N_HEADS_M + hh, blk=blk,
                                      kc=k_s[pl.ds(row0, CHUNK), hs], vta=vta_s[blk, hh], w=w_rows[hh:hh + 1, :],
                                      decay=decay[hh:hh + 1, :], fac=fac[hh:hh + 1, :]))
            steps.append(units)
        for d in range(N_DIRS):
            mst_s[d] = m_cur[d]
        for units in steps:
            for un in units:
                vw = (un["vta"].astype(F32) * un["w"]).astype(BF16)
                un["upd"] = _dot(vw, un["kc"])
        for sub in range(2):
            it = 2 * pair + sub
            cact_s[pl.ds(pl.multiple_of(it * CHUNK, CHUNK), CHUNK // 2), :] = conv_tile(seq, (c0 + sub) * CHUNK)
        cta = [cta_s[j] for j in range(N_UNITS)]
        for units in steps:
            for un in units:
                j = un["j"]
                ctab_s[un["d"], un["blk"], un["hh"]] = cta[j].astype(BF16)
                cta[j] = un["decay"] * cta[j] + un["fac"] * un["upd"]
        for j in range(N_UNITS):
            cta_s[j] = cta[j]

        if emit_state:
            @pl.when(c0 == nc - 2)
            def _():
                for j in range(N_UNITS):
                    co_ref[seq, j] = cta_s[j, 0:HEAD_DIM_M, :].T
                    no_ref[seq, j:j + 1, :] = cta_s[j, HEAD_DIM_M:HEAD_DIM_M + 1, :]
                mo_ref[seq] = mst_s[...]
        return carry

    lax.fori_loop(0, n_blk // 2, state_step, 0)

    def chunk_out(pair, carry):
        heads = []
        for blk in (2 * pair, 2 * pair + 1):
            row0 = pl.multiple_of(blk * CHUNK, CHUNK)
            gr0 = pl.multiple_of(blk * GATE_ROWS, GATE_ROWS)
            dirs = []
            for d in range(N_DIRS):
                m_prev = mprev_s[d, blk]
                mm = jnp.maximum(m_prev, rows_s[d, 1, pl.ds(gr0, GATE_ROWS), :])
                floor = jnp.exp(-(rows_s[d, 0, pl.ds(gr0, GATE_ROWS), :] + mm))
                dirs.append(dict(mm=mm, inter=jnp.exp(m_prev - mm), floor=floor,
                                 acol=acol_s[d, blk], mask=upper if d == 0 else lower))
            for hh in range(N_HEADS_M):
                hs = slice(hh * HEAD_DIM_M, (hh + 1) * HEAD_DIM_M)
                hd = dict(hh=hh, hs=hs, blk=blk, row0=row0, dirs=dirs,
                          kc=k_s[pl.ds(row0, CHUNK), hs], qt=qt_s[blk, hs, :], vta=vta_s[blk, hh])
                hd["st"] = _dot(hd["kc"], hd["qt"])
                hd["p1"] = [_dot(ctab_s[d, blk, hh], hd["qt"]) for d in range(N_DIRS)]
                heads.append(hd)
        for hd in heads:
            hh = hd["hh"]
            sm = []
            for dd in hd["dirs"]:
                z = dd["acol"][:, hh:hh + 1] - dd["mm"][hh:hh + 1, :]
                sm.append((hd["st"] * jnp.exp(jnp.where(dd["mask"], z, neg_inf))).astype(BF16))
            hd["p2"] = _dot(hd["vta"], jnp.concatenate(sm, axis=1))
        half = CHUNK // 2
        for blk in (2 * pair, 2 * pair + 1):
            row0 = pl.multiple_of(blk * CHUNK, CHUNK)
            cact_s[pl.ds(row0 + half, half), :] = conv_tile(blk // nc, (blk % nc) * CHUNK + half)
        for b2 in range(2):
            parts = []
            for hd in heads[b2 * N_HEADS_M:(b2 + 1) * N_HEADS_M]:
                hh = hd["hh"]
                hsum = None
                for d, dd in enumerate(hd["dirs"]):
                    numa = dd["inter"][hh:hh + 1, :] * hd["p1"][d] + hd["p2"][:, d * CHUNK:(d + 1) * CHUNK]
                    den = numa[HEAD_DIM_M:HEAD_DIM_M + 1, :]
                    ht = numa[0:HEAD_DIM_M, :] / jnp.maximum(jnp.abs(den), dd["floor"][hh:hh + 1, :])
                    hsum = ht if hsum is None else hsum + ht
                r = lax.rsqrt(jnp.mean(hsum * hsum, axis=0, keepdims=True) + EPS)
                parts.append((hsum * r * gmh_ref[0, hd["hs"], :]).T)
            row0 = heads[b2 * N_HEADS_M]["row0"]
            oc = jnp.concatenate(parts, axis=1) * so_s[pl.ds(row0, CHUNK), :].astype(F32)
            mix_s[pl.ds(row0, CHUNK), D_FOURIER + D_CONV:D_MIX] = oc.astype(BF16)
        return carry

    lax.fori_loop(0, n_blk // 2, chunk_out, 0)
    mix_s[:, D_FOURIER:D_FOURIER + D_CONV] = _dot(cact_s[...], wpw_ref[0]).astype(BF16)

    ga1 = mod_ref[0][:, 2 * D_MODEL:3 * D_MODEL]
    res = _dot(mix_s[...], wout_ref[0])
    x1_ref[...] = (load_x() + ga1 * res).reshape(bb, t, D_MODEL)


def _whole(a):
    nd = a.ndim
    return pl.BlockSpec(a.shape, lambda g: (0,) * nd, pipeline_mode=pl.Buffered(1))


def _layer_block(a, l):
    nd = a.ndim
    return pl.BlockSpec((1,) + a.shape[1:], lambda g: (l,) + (0,) * (nd - 1), pipeline_mode=pl.Buffered(1))


def _mixer_call(x, pos, mod_rows, mod_base, per_batch_mod, l, pw, state0, emit_state):
    nb, t, _ = x.shape
    bb = ROWS // t
    assert bb * t == ROWS and nb % bb == 0 and t % CHUNK == 0
    rows = ROWS
    n_blk = rows // CHUNK
    has_init = state0 is not None
    add_pos = pos is not None

    args = [x]
    in_specs = [pl.BlockSpec((bb, t, D_MODEL), lambda g: (g, 0, 0))]
    if add_pos:
        assert bb == 1
        args += list(pos)
        in_specs += [_whole(a) for a in pos]
    args.append(mod_rows)
    if per_batch_mod:
        assert bb == 1
        in_specs.append(pl.BlockSpec((1, 1, N_MOD * D_MODEL), lambda g: (mod_base + g, 0, 0)))
    else:
        in_specs.append(pl.BlockSpec((1, 1, N_MOD * D_MODEL), lambda g: (mod_base, 0, 0)))
    for name in ("g1", "win", "bg"):
        args.append(pw[name])
        in_specs.append(_layer_block(pw[name], l))
    for a in (pw["blk"], pw["dc"][t], pw["ds"][t]):
        args.append(a)
        in_specs.append(_whole(a))
    for name in ("wfn", "wdw", "bdw", "lng", "lnb", "wpw", "gmh"):
        args.append(pw[name])
        in_specs.append(_layer_block(pw[name], l))
    if has_init:
        c0, n0, m0 = state0
        args += [c0, n0, m0]
        in_specs += [
            pl.BlockSpec((bb, 1, N_UNITS, HEAD_DIM_M, HEAD_DIM_M), lambda g: (g, l, 0, 0, 0)),
            pl.BlockSpec((bb, 1, N_UNITS, HEAD_DIM_M), lambda g: (g, l, 0, 0)),
            pl.BlockSpec((bb, 1, N_DIRS, GATE_ROWS, CHUNK), lambda g: (g, l, 0, 0, 0)),
        ]
    args.append(pw["wout"])
    in_specs.append(_layer_block(pw["wout"], l))

    out_shape = [jax.ShapeDtypeStruct((nb, t, D_MODEL), F32)]
    out_specs = [pl.BlockSpec((bb, t, D_MODEL), lambda g: (g, 0, 0))]
    if emit_state:
        out_shape += [
            jax.ShapeDtypeStruct((nb, N_UNITS, HEAD_DIM_M, HEAD_DIM_M), F32),
            jax.ShapeDtypeStruct((nb, N_UNITS, HEAD_DIM_M), F32),
            jax.ShapeDtypeStruct((nb, N_DIRS, GATE_ROWS, CHUNK), F32),
        ]
        out_specs += [
            pl.BlockSpec((bb, N_UNITS, HEAD_DIM_M, HEAD_DIM_M), lambda g: (g, 0, 0, 0)),
            pl.BlockSpec((bb, N_UNITS, HEAD_DIM_M), lambda g: (g, 0, 0)),
            pl.BlockSpec((bb, N_DIRS, GATE_ROWS, CHUNK), lambda g: (g, 0, 0, 0)),
        ]
    scratch = [
        pltpu.VMEM((rows, D_FOURIER), BF16),
        pltpu.VMEM((bb, t + 2 * PAD_LO, D_CONV), F32),
        pltpu.VMEM((SUBLANES - 1, bb, t + SHIFT_ROWS, D_CONV), F32),
        pltpu.VMEM((rows, D_CONV), BF16),
        pltpu.VMEM((n_blk, D_MLSTM, CHUNK), BF16),
        pltpu.VMEM((rows, D_MLSTM), BF16),
        pltpu.VMEM((n_blk, N_HEADS_M, HEAD_AUG, CHUNK), BF16),
        pltpu.VMEM((rows, D_MLSTM), BF16),
        pltpu.VMEM((N_DIRS, 5, n_blk * GATE_ROWS, CHUNK), F32),
        pltpu.VMEM((N_DIRS, n_blk, CHUNK, N_HEADS_M), F32),
        pltpu.VMEM((N_DIRS, n_blk, GATE_ROWS, CHUNK), F32),
        pltpu.VMEM((N_DIRS, n_blk, N_HEADS_M, HEAD_AUG, HEAD_DIM_M), BF16),
        pltpu.VMEM((rows, D_MIX), BF16),
        pltpu.VMEM((N_UNITS, HEAD_AUG, HEAD_DIM_M), F32),
        pltpu.VMEM((N_DIRS, GATE_ROWS, CHUNK), F32),
    ]
    body = functools.partial(_mixer_body, bb, t, has_init, emit_state, add_pos)
    return pl.pallas_call(
        body,
        grid=(nb // bb,),
        in_specs=in_specs,
        out_specs=out_specs,
        out_shape=out_shape,
        scratch_shapes=scratch,
        compiler_params=pltpu.CompilerParams(dimension_semantics=("arbitrary",),
                                             vmem_limit_bytes=VMEM_LIMIT),
        name="mixer_t%d" % t,
    )(*args)


def _route_rows(logits_t, b_router_col):
    scores = _sigmoid(logits_t)
    sel = scores + b_router_col
    sel_r = [sel[e:e + 1, :] for e in range(N_EXPERTS)]
    sc_r = [scores[e:e + 1, :] for e in range(N_EXPERTS)]

    best = None
    best_v = None
    for g in range(N_EXPERT_GROUPS):
        a, b, c, d = sel_r[g * EXPERTS_PER_GROUP:(g + 1) * EXPERTS_PER_GROUP]
        hi1, lo1 = jnp.maximum(a, b), jnp.minimum(a, b)
        hi2, lo2 = jnp.maximum(c, d), jnp.minimum(c, d)
        gs = jnp.maximum(hi1, hi2) + jnp.maximum(jnp.minimum(hi1, hi2), jnp.maximum(lo1, lo2))
        if g == 0:
            best = jnp.zeros(gs.shape, jnp.int32)
            best_v = gs
        else:
            upd = gs > best_v
            best = jnp.where(upd, g, best)
            best_v = jnp.where(upd, gs, best_v)

    def pick(rows, j):
        out = rows[(N_EXPERT_GROUPS - 1) * EXPERTS_PER_GROUP + j]
        for g in range(N_EXPERT_GROUPS - 2, -1, -1):
            out = jnp.where(best == g, rows[g * EXPERTS_PER_GROUP + j], out)
        return out

    s = [pick(sel_r, j) for j in range(EXPERTS_PER_GROUP)]
    sc = [pick(sc_r, j) for j in range(EXPERTS_PER_GROUP)]
    rank = [jnp.zeros(best.shape, jnp.int32) for _ in range(EXPERTS_PER_GROUP)]
    for a in range(EXPERTS_PER_GROUP):
        for b in range(a + 1, EXPERTS_PER_GROUP):
            b_first = s[b] > s[a]
            rank[a] = rank[a] + b_first.astype(jnp.int32)
            rank[b] = rank[b] + (1 - b_first.astype(jnp.int32))
    w = [jnp.where(rank[j] < 2, sc[j], 0.0) for j in range(EXPERTS_PER_GROUP)]
    tot = w[0] + w[1] + w[2] + w[3]
    return best, [wj / tot for wj in w]


def _moe_body(is_last, x_ref, mod_ref, g2_ref, wr2_ref, brc_ref, utri_ref, wg_ref, wu_ref, wd_ref, fg_ref,
              o_ref, he_s, dest_s, destl_s, pt_s, ys_s, sm_s):
    gi = pl.program_id(1)
    tm = ROWS

    @pl.when(gi == 0)
    def _():
        mod = mod_ref[0]
        sh2 = mod[:, 3 * D_MODEL:4 * D_MODEL]
        sc2 = mod[:, 4 * D_MODEL:5 * D_MODEL]
        h = _rmsnorm_rows(x_ref[...], g2_ref[0]) * (1.0 + sc2) + sh2
        h_hi, h_lo = _hi_lo(h)
        he_s[:, 0:D_MODEL] = h_hi
        lg = _dot(jnp.concatenate([h_hi, h_lo], axis=1), wr2_ref[...])
        lg = lg[:, 0:N_EXPERTS] + lg[:, N_EXPERTS:2 * N_EXPERTS]
        logits_t = jnp.concatenate([lg, jnp.zeros((tm, MOE_EXT - N_EXPERTS), F32)], axis=1).T[0:N_EXPERTS, :]
        best, cw = _route_rows(logits_t, brc_ref[...])

        row_i = lax.broadcasted_iota(jnp.int32, (16, tm), 0)
        onehot = (row_i == best).astype(F32)
        before = _dot(onehot.astype(BF16), utri_ref[...])
        dest = jnp.zeros((1, tm), F32)
        off_blk = jnp.int32(0)
        for g in range(N_EXPERT_GROUPS):
            cnt = jnp.sum(onehot[g:g + 1, :]).astype(jnp.int32)
            n_blk = lax.shift_right_logical(cnt + (MOE_BLK - 1), int(math.log2(MOE_BLK)))
            sm_s[g] = off_blk
            sm_s[N_EXPERT_GROUPS + g] = n_blk
            base = (off_blk * MOE_BLK).astype(F32)
            dest = dest + onehot[g:g + 1, :] * (before[g:g + 1, :] + base)
            off_blk = off_blk + n_blk

        cw_hi = [wj.astype(BF16).astype(F32) for wj in cw]
        cw_lo = [wj - hj for wj, hj in zip(cw, cw_hi)]
        stack = jnp.concatenate([dest] + cw_hi + cw_lo + [jnp.zeros((MOE_EXT - 9, tm), F32)], axis=0)
        he_s[:, D_MODEL:D_MODEL + MOE_EXT] = stack.T.astype(BF16)
        dest_s[...] = jnp.broadcast_to(dest, (SUBLANES, tm))
        destl_s[...] = jnp.broadcast_to(dest, (MOE_BLK, tm)).T
        pt_s[...] = jnp.zeros(pt_s.shape, BF16)
        ys_s[...] = jnp.zeros(ys_s.shape, BF16)

    first_blk = sm_s[gi]
    n_blk = sm_s[N_EXPERT_GROUPS + gi]

    def block(i, carry):
        r0 = pl.multiple_of((first_blk + i) * MOE_BLK, MOE_BLK)
        rows_f = (lax.broadcasted_iota(jnp.int32, (MOE_BLK, 1), 0) + r0).astype(F32)
        p_blk = jnp.where(dest_s[0:1, :] == rows_f, 1.0, 0.0).astype(BF16)
        cols_f = (lax.broadcasted_iota(jnp.int32, (1, MOE_BLK), 1) + r0).astype(F32)
        pt_s[:, pl.ds(r0, MOE_BLK)] = jnp.where(destl_s[...] == cols_f, 1.0, 0.0).astype(BF16)
        xe = _dot(p_blk, he_s[...])
        xs = xe[:, 0:D_MODEL].astype(BF16)
        acts = []
        for j in range(EXPERTS_PER_GROUP):
            gate = _dot(xs, wg_ref[0, j])
            up = _dot(xs, wu_ref[0, j])
            cwj = (xe[:, D_MODEL + 1 + j:D_MODEL + 2 + j]
                   + xe[:, D_MODEL + 1 + EXPERTS_PER_GROUP + j:D_MODEL + 2 + EXPERTS_PER_GROUP + j])
            acts.append((gate * _sigmoid(gate) * up * cwj).astype(BF16))
        ys_s[pl.ds(r0, MOE_BLK), :] = _dot(jnp.concatenate(acts, axis=1), wd_ref[0, 0]).astype(BF16)
        return carry

    lax.fori_loop(0, n_blk, block, 0)

    @pl.when(gi == N_EXPERT_GROUPS - 1)
    def _():
        ga2 = mod_ref[0][:, 5 * D_MODEL:6 * D_MODEL]
        xo = x_ref[...] + ga2 * _dot(pt_s[...], ys_s[...])
        if is_last:
            xo = _rmsnorm_rows(xo, fg_ref[...])
        o_ref[...] = xo


def _moe_call(x2d, mod_rows, mod_base, per_tile_mod, l, pw, is_last):
    n_tok = x2d.shape[0]
    tm = ROWS
    assert n_tok % tm == 0
    if per_tile_mod:
        mod_spec = pl.BlockSpec((1, 1, N_MOD * D_MODEL), lambda i, g: (mod_base + i, 0, 0))
    else:
        mod_spec = pl.BlockSpec((1, 1, N_MOD * D_MODEL), lambda i, g: (mod_base, 0, 0))

    def const(shape):
        return pl.BlockSpec(shape, lambda i, g: (0,) * len(shape), pipeline_mode=pl.Buffered(1))

    def grp(shape):
        return pl.BlockSpec((1, EXPERTS_PER_GROUP) + shape, lambda i, g: (l, g, 0, 0))

    return pl.pallas_call(
        functools.partial(_moe_body, is_last),
        grid=(n_tok // tm, N_EXPERT_GROUPS),
        in_specs=[
            pl.BlockSpec((tm, D_MODEL), lambda i, g: (i, 0)),
            mod_spec,
            pl.BlockSpec((1, 1, D_MODEL), lambda i, g: (l, 0, 0), pipeline_mode=pl.Buffered(1)),
            const((2 * D_MODEL, 2 * N_EXPERTS)),
            const((N_EXPERTS, 1)),
            const((tm, tm)),
            grp((D_MODEL, D_EXPERT)),
            grp((D_MODEL, D_EXPERT)),
            pl.BlockSpec((1, 1, EXPERTS_PER_GROUP * D_EXPERT, D_MODEL), lambda i, g: (l, g, 0, 0)),
            const((1, D_MODEL)),
        ],
        out_specs=pl.BlockSpec((tm, D_MODEL), lambda i, g: (i, 0)),
        out_shape=jax.ShapeDtypeStruct((n_tok, D_MODEL), F32),
        scratch_shapes=[
            pltpu.VMEM((tm, D_MODEL + MOE_EXT), BF16),
            pltpu.VMEM((SUBLANES, tm), F32),
            pltpu.VMEM((tm, MOE_BLK), F32),
            pltpu.VMEM((tm, MOE_CAP), BF16),
            pltpu.VMEM((MOE_CAP, D_MODEL), BF16),
            pltpu.SMEM((2 * N_EXPERT_GROUPS,), jnp.int32),
        ],
        compiler_params=pltpu.CompilerParams(dimension_semantics=("arbitrary", "arbitrary"),
                                             vmem_limit_bytes=VMEM_LIMIT),
        name="moe",
    )(x2d, mod_rows, pw["g2"], pw["wr2"], pw["brc"], pw["utri"], pw["weg"], pw["weu"], pw["wed"], pw["fg"])


def _dft_tables(t):
    idx = np.arange(t, dtype=np.int64)
    ang = 2.0 * np.pi * ((idx[:, None] * idx[None, :]) % t).astype(np.float64) / t
    scale = 1.0 / math.sqrt(t * D_FOURIER_GROUP)
    return (np.cos(ang) * scale).astype(np.float32), (-np.sin(ang) * scale).astype(np.float32)


def _group_tables():
    idx = np.arange(D_FOURIER_GROUP, dtype=np.int64)
    ang = 2.0 * np.pi * ((idx[:, None] * idx[None, :]) % D_FOURIER_GROUP).astype(np.float64) / D_FOURIER_GROUP
    eye = np.eye(N_FOURIER_GROUPS)
    return np.concatenate([np.kron(eye, np.cos(ang)), np.kron(eye, np.sin(ang))], axis=1).astype(np.float32)


def _grid_pos_tables(n_tokens, dtype):
    quarter = D_MODEL // 4
    omega = 1.0 / (10000.0 ** (jnp.arange(quarter, dtype=jnp.float32) / quarter))

    def enc(p):
        a = p[:, None] * omega[None, :]
        return jnp.concatenate([jnp.sin(a), jnp.cos(a)], axis=-1).astype(dtype)

    return (enc(jnp.arange(n_tokens // GRID_W, dtype=jnp.float32)), enc(jnp.arange(GRID_W, dtype=jnp.float32)))


def _router_hi_lo(w_router):
    w_hi = w_router.astype(BF16)
    w_lo = (w_router - w_hi.astype(F32)).astype(BF16)
    return jnp.concatenate([jnp.concatenate([w_hi, w_lo], axis=1),
                            jnp.concatenate([w_hi, jnp.zeros_like(w_lo)], axis=1)], axis=0)


def _prepare(seq_lens, norm1_g, norm2_g, w_in, w_fnet, w_dw, b_dw, conv_ln_g, conv_ln_b, w_pw, b_gate, g_mh,
             w_out, w_router, b_router, w_exp_gate, w_exp_up, w_exp_down, final_g):
    dft = {t: _dft_tables(t) for t in sorted(set(seq_lens))}
    w_in_b = w_in.astype(BF16)
    tok = np.arange(ROWS)
    return dict(
        g1=norm1_g.reshape(DEPTH, 1, D_MODEL), g2=norm2_g.reshape(DEPTH, 1, D_MODEL),
        win=w_in_b, bg=b_gate.reshape(DEPTH, 1, N_GATES),
        blk=jnp.asarray(_group_tables()).astype(BF16),
        dc={t: jnp.asarray(v[0]).astype(BF16) for t, v in dft.items()},
        ds={t: jnp.asarray(v[1]).astype(BF16) for t, v in dft.items()},
        wfn=w_fnet.astype(BF16),
        wdw=jnp.concatenate([w_dw, jnp.zeros((DEPTH, 1, D_CONV), F32)], axis=1),
        bdw=b_dw.reshape(DEPTH, 1, D_CONV), lng=conv_ln_g.reshape(DEPTH, 1, D_CONV),
        lnb=conv_ln_b.reshape(DEPTH, 1, D_CONV), wpw=w_pw.astype(BF16),
        gmh=jnp.broadcast_to(g_mh.reshape(DEPTH, D_MLSTM, 1), (DEPTH, D_MLSTM, CHUNK)),
        wout=w_out.astype(BF16),
        wr2=_router_hi_lo(w_router), brc=b_router.reshape(N_EXPERTS, 1),
        utri=jnp.asarray((tok[:, None] < tok[None, :]).astype(np.float32)).astype(BF16),
        weg=w_exp_gate.astype(BF16), weu=w_exp_up.astype(BF16),
        wed=w_exp_down.reshape(DEPTH, N_EXPERT_GROUPS, EXPERTS_PER_GROUP * D_EXPERT, D_MODEL).astype(BF16),
        fg=final_g.reshape(1, D_MODEL),
    )


def kernel(x_prompt, x_sample, state_C, state_n, state_m, c, c_ctx, w_ada, b_ada, norm1_g, norm2_g, w_in, w_fnet, w_dw, b_dw, conv_ln_g, conv_ln_b, w_pw, b_gate, g_mh, w_out, w_router, b_router, w_exp_gate, w_exp_up, w_exp_down, final_g):
    bp, tp, _ = x_prompt.shape
    bs, ts, _ = x_sample.shape

    cv = jnp.zeros((MOD_ROWS, D_MODEL), F32).at[:bs].set(c).at[bs].set(c_ctx)
    mod_all = _ada_call(cv, w_ada, b_ada)
    mod_rows = mod_all.reshape(DEPTH * MOD_ROWS, 1, N_MOD * D_MODEL)

    pw = _prepare((tp, ts), norm1_g, norm2_g, w_in, w_fnet, w_dw, b_dw, conv_ln_g, conv_ln_b, w_pw, b_gate,
                  g_mh, w_out, w_router, b_router, w_exp_gate, w_exp_up, w_exp_down, final_g)
    pos = _grid_pos_tables(ts, x_sample.dtype)

    c0 = state_C.reshape(bs, DEPTH, N_UNITS, HEAD_DIM_M, HEAD_DIM_M)
    n0 = state_n.reshape(bs, DEPTH, N_UNITS, HEAD_DIM_M)
    m0 = jnp.pad(state_m, ((0, 0), (0, 0), (0, 0), (0, GATE_ROWS - N_HEADS_M)))
    m0 = jnp.broadcast_to(m0[..., None], (bs, DEPTH, N_DIRS, GATE_ROWS, CHUNK))

    xp, xs = x_prompt, x_sample
    cs, ns, ms = [], [], []
    for l in range(DEPTH):
        is_last = l == DEPTH - 1
        base = l * MOD_ROWS
        xp, c_l, n_l, m_l = _mixer_call(xp, None, mod_rows, base + bs, False, l, pw, None, True)
        cs.append(c_l)
        ns.append(n_l)
        ms.append(m_l[:, :, :N_HEADS_M, 0])
        xp = _moe_call(xp.reshape(bp * tp, D_MODEL), mod_rows, base + bs, False, l, pw, is_last)
        xp = xp.reshape(bp, tp, D_MODEL)

        (xs,) = _mixer_call(xs, pos if l == 0 else None, mod_rows, base, True, l, pw, (c0, n0, m0), False)
        xs = _moe_call(xs.reshape(bs * ts, D_MODEL), mod_rows, base, True, l, pw, is_last)
        xs = xs.reshape(bs, ts, D_MODEL)

    new_c = jnp.stack(cs, axis=1).reshape(bp, DEPTH, N_DIRS, N_HEADS_M, HEAD_DIM_M, HEAD_DIM_M)
    new_n = jnp.stack(ns, axis=1).reshape(bp, DEPTH, N_DIRS, N_HEADS_M, HEAD_DIM_M)
    new_m = jnp.stack(ms, axis=1).reshape(bp, DEPTH, N_DIRS, N_HEADS_M)
    return (xp, xs, new_c, new_n, new_m)
```

```python
import functools
import math

import numpy as np
import jax
import jax.numpy as jnp
from jax import lax
from jax.experimental import pallas as pl
from jax.experimental.pallas import tpu as pltpu

D_MODEL = 1024
DEPTH = 4
GRID_W = 64
EPS = 1e-6
D_FOURIER = 256
N_FOURIER_GROUPS = 4
D_FOURIER_GROUP = D_FOURIER // N_FOURIER_GROUPS
D_CONV = 256
CONV_WIDTH = 31
CONV_PAD = CONV_WIDTH // 2
D_MLSTM = 512
N_HEADS_M = 4
HEAD_DIM_M = D_MLSTM // N_HEADS_M
N_DIRS = 2
CHUNK = 128
N_GATES = 2 * N_DIRS * N_HEADS_M
N_UNITS = N_DIRS * N_HEADS_M
D_MIX = D_FOURIER + D_CONV + D_MLSTM
D_IN_PROJ = D_FOURIER + 2 * D_CONV + 4 * D_MLSTM + N_GATES
N_EXPERTS = 16
N_EXPERT_GROUPS = 4
EXPERTS_PER_GROUP = N_EXPERTS // N_EXPERT_GROUPS
D_EXPERT = 256
N_MOD = 6

O_B = D_FOURIER
O_Q = O_B + 2 * D_CONV
O_K = O_Q + D_MLSTM
O_V = O_K + D_MLSTM
O_O = O_V + D_MLSTM
O_G = O_O + D_MLSTM

ROWS = 1024
CONV_ROW_TILE = 64
PAD_LO = 16
MOD_ROWS = 16
SUBLANES = 8
SHIFT_ROWS = 24
AUG = 16
HEAD_AUG = HEAD_DIM_M + AUG
GATE_ROWS = 8
MOE_BLK = 128
MOE_CAP = ROWS + N_EXPERT_GROUPS * MOE_BLK
MOE_EXT = 128
VMEM_LIMIT = 60 * 1024 * 1024

F32 = jnp.float32
BF16 = jnp.bfloat16


def _dot(a, b):
    return jnp.dot(a, b, preferred_element_type=F32)


def _sigmoid(x):
    return 1.0 / (1.0 + jnp.exp(-x))


def _log_sigmoid(x):
    return jnp.minimum(x, 0.0) - jnp.log(1.0 + jnp.exp(-jnp.abs(x)))


def _hi_lo(a_f32):
    hi = a_f32.astype(BF16)
    return hi, (a_f32 - hi.astype(F32)).astype(BF16)


def _split_dot(a_f32, b_bf16):
    hi, lo = _hi_lo(a_f32)
    return _dot(hi, b_bf16) + _dot(lo, b_bf16)


def _split_dot_left(b_bf16, a_f32):
    hi, lo = _hi_lo(a_f32)
    return _dot(b_bf16, hi) + _dot(b_bf16, lo)


def _rmsnorm_rows(x, g):
    ms = jnp.mean(x * x, axis=-1, keepdims=True)
    return x * lax.rsqrt(ms + EPS) * g


def _ada_body(c_ref, w_ref, b_ref, o_ref):
    cv = c_ref[...]
    s_hi, s_lo = _hi_lo(cv * _sigmoid(cv))
    w_hi, w_lo = _hi_lo(w_ref[0])
    both = _dot(jnp.concatenate([s_hi, s_lo], axis=0), w_hi)
    o_ref[0] = both[0:MOD_ROWS] + both[MOD_ROWS:2 * MOD_ROWS] + _dot(s_hi, w_lo) + b_ref[0]


def _ada_call(cv, w_ada, b_ada):
    n_col = N_MOD * D_MODEL
    tn = D_MODEL
    return pl.pallas_call(
        _ada_body,
        grid=(DEPTH, n_col // tn),
        in_specs=[
            pl.BlockSpec((MOD_ROWS, D_MODEL), lambda l, j: (0, 0)),
            pl.BlockSpec((1, D_MODEL, tn), lambda l, j: (l, 0, j)),
            pl.BlockSpec((1, 1, tn), lambda l, j: (l, 0, j)),
        ],
        out_specs=pl.BlockSpec((1, MOD_ROWS, tn), lambda l, j: (l, 0, j)),
        out_shape=jax.ShapeDtypeStruct((DEPTH, MOD_ROWS, n_col), F32),
        compiler_params=pltpu.CompilerParams(dimension_semantics=("arbitrary", "arbitrary")),
        name="ada",
    )(cv, w_ada, b_ada.reshape(DEPTH, 1, n_col))


def _scan_max_lanes(x, reverse):
    lane = lax.broadcasted_iota(jnp.int32, x.shape, 1)
    neg_inf = jnp.float32(-jnp.inf)
    sh = 1
    while sh < CHUNK:
        if reverse:
            y = pltpu.roll(x, CHUNK - sh, axis=1)
            x = jnp.maximum(x, jnp.where(lane < CHUNK - sh, y, neg_inf))
        else:
            y = pltpu.roll(x, sh, axis=1)
            x = jnp.maximum(x, jnp.where(lane >= sh, y, neg_inf))
        sh *= 2
    return x


def _mixer_body(bb, t, has_init, emit_state, add_pos, *refs):
    rows = bb * t
    nc = t // CHUNK
    n_blk = rows // CHUNK
    refs = list(refs)
    x_ref = refs.pop(0)
    if add_pos:
        posr_ref, posc_ref = refs[:2]
        refs = refs[2:]
    (mod_ref, g1_ref, win_ref, bg_ref, blk_ref, dc_ref, ds_ref, wfn_ref,
     wdw_ref, bdw_ref, lng_ref, lnb_ref, wpw_ref, gmh_ref) = refs[:14]
    refs = refs[14:]
    if has_init:
        c0_ref, n0_ref, m0_ref = refs[:3]
        refs = refs[3:]
    wout_ref = refs.pop(0)
    x1_ref = refs.pop(0)
    if emit_state:
        co_ref, no_ref, mo_ref = refs[:3]
        refs = refs[3:]
    (xa_s, pad_s, shf_s, cact_s, qt_s, k_s, vta_s, so_s, rows_s, acol_s, mprev_s, ctab_s,
     mix_s, cta_s, mst_s) = refs

    if add_pos:
        half_d = D_MODEL // 2
        for g in range(t // GRID_W):
            rs = slice(g * GRID_W, (g + 1) * GRID_W)
            emb = jnp.concatenate([jnp.broadcast_to(posr_ref[g:g + 1, :], (GRID_W, half_d)), posc_ref[...]], axis=1)
            x1_ref[0, rs, :] = x_ref[0, rs, :] + emb

    def load_x():
        src = x1_ref if add_pos else x_ref
        return src[...].reshape(rows, D_MODEL)

    def w_in(lo, hi):
        return win_ref[0, :, lo:hi]

    x = load_x()
    mod = mod_ref[0]
    sh1 = mod[:, 0:D_MODEL]
    sc1 = mod[:, D_MODEL:2 * D_MODEL]
    h = _rmsnorm_rows(x, g1_ref[0]) * (1.0 + sc1) + sh1
    hb = h.astype(BF16)

    pb = _dot(hb, w_in(O_B, O_Q))
    u = pb[:, :D_CONV] * _sigmoid(pb[:, D_CONV:])
    zpad = jnp.zeros((PAD_LO, D_CONV), F32)
    for i in range(bb):
        pad_s[i, 0:PAD_LO, :] = zpad
        pad_s[i, PAD_LO:PAD_LO + t, :] = u[i * t:(i + 1) * t]
        pad_s[i, PAD_LO + t:2 * PAD_LO + t, :] = zpad

    for i in range(bb):
        for s8 in range(1, SUBLANES):
            shf_s[s8 - 1, i] = pad_s[i, s8:s8 + t + SHIFT_ROWS, :]

    def conv_tile(seq, r0):
        acc = jnp.broadcast_to(bdw_ref[0], (CONV_ROW_TILE, D_CONV))
        for j in range(CONV_WIDTH):
            q = j + PAD_LO - CONV_PAD
            win_rows = pl.ds(pl.multiple_of(r0 + (q // SUBLANES) * SUBLANES, SUBLANES), CONV_ROW_TILE)
            win = pad_s[seq, win_rows, :] if q % SUBLANES == 0 else shf_s[q % SUBLANES - 1, seq, win_rows, :]
            acc = acc + win * wdw_ref[0, j:j + 1, :]
        mu = jnp.mean(acc, axis=-1, keepdims=True)
        cen = acc - mu
        var = jnp.mean(cen * cen, axis=-1, keepdims=True)
        uf = cen * lax.rsqrt(var + EPS) * lng_ref[0] + lnb_ref[0]
        return (uf * _sigmoid(uf)).astype(BF16)

    xa_s[...] = _dot(hb, w_in(0, O_B)).astype(BF16)
    k_s[...] = (_dot(hb, w_in(O_K, O_V)) * (HEAD_DIM_M ** -0.5)).astype(BF16)
    og = _dot(hb, w_in(O_O, D_IN_PROJ))
    so_s[...] = _sigmoid(og[:, 0:D_MLSTM]).astype(BF16)
    qf = _dot(hb, w_in(O_Q, O_K))
    for b in range(n_blk):
        qt_s[b] = qf[b * CHUNK:(b + 1) * CHUNK, :].T.astype(BF16)
    vf = _dot(hb, w_in(O_V, O_O))
    ones_rows = jnp.where(lax.broadcasted_iota(jnp.int32, (AUG, CHUNK), 0) == 0, 1.0, 0.0).astype(BF16)
    for b in range(n_blk):
        vt = vf[b * CHUNK:(b + 1) * CHUNK, :].T.astype(BF16)
        for hh in range(N_HEADS_M):
            vta_s[b, hh, 0:HEAD_DIM_M, :] = vt[hh * HEAD_DIM_M:(hh + 1) * HEAD_DIM_M, :]
            vta_s[b, hh, HEAD_DIM_M:HEAD_AUG, :] = ones_rows

    r_i = lax.broadcasted_iota(jnp.int32, (CHUNK, CHUNK), 0)
    c_i = lax.broadcasted_iota(jnp.int32, (CHUNK, CHUNK), 1)
    lower = r_i >= c_i
    upper = r_i <= c_i
    tri_lo = lower.astype(BF16)
    tri_up = upper.astype(BF16)
    neg_inf = jnp.float32(-jnp.inf)
    n_gr = n_blk * GATE_ROWS

    gcol = og[:, D_MLSTM:D_MLSTM + N_GATES] + bg_ref[0]
    kind_c = (lax.broadcasted_iota(jnp.int32, gcol.shape, 1) // N_HEADS_M) % 2
    gcol = jnp.where(kind_c == 1, _log_sigmoid(gcol), gcol)
    grow = jnp.concatenate([gcol, jnp.zeros((rows, CHUNK - N_GATES), F32)], axis=1).T[0:N_GATES, :]
    for d in range(N_DIRS):
        g0 = d * 2 * N_HEADS_M
        ll = CHUNK - 1 if d == 0 else 0
        xg = jnp.concatenate([grow[g0:g0 + GATE_ROWS, b * CHUNK:(b + 1) * CHUNK] for b in range(n_blk)], axis=0)
        b_rows = pltpu.roll(_split_dot(xg, tri_up if d == 0 else tri_lo), n_gr - N_HEADS_M, axis=0)
        a_rows = xg - b_rows
        a_run = _scan_max_lanes(a_rows, d == 1)
        b_last = jnp.broadcast_to(b_rows[:, ll:ll + 1], (n_gr, CHUNK))
        a_max = jnp.broadcast_to(a_run[:, ll:ll + 1], (n_gr, CHUNK))
        rows_s[d, 0] = b_rows
        rows_s[d, 1] = a_run
        rows_s[d, 2] = b_last
        rows_s[d, 3] = a_max
        rows_s[d, 4] = jnp.exp(a_rows - a_max)
        for b in range(n_blk):
            gc = gcol[b * CHUNK:(b + 1) * CHUNK, :]
            bcol = _split_dot_left(tri_lo if d == 0 else tri_up, gc)
            acol_s[d, b] = gc[:, g0:g0 + N_HEADS_M] - bcol[:, g0 + N_HEADS_M:g0 + 2 * N_HEADS_M]

    for i in range(bb):
        xa_i = xa_s[i * t:(i + 1) * t, :]
        uu = _dot(xa_i, blk_ref[...])
        y = (_dot(dc_ref[...], uu[:, :D_FOURIER].astype(BF16))
             + _dot(ds_ref[...], uu[:, D_FOURIER:].astype(BF16)))
        mix_s[i * t:(i + 1) * t, 0:D_FOURIER] = _dot(y.astype(BF16), wfn_ref[0]).astype(BF16)

    def state_step(pair, carry):
        seq = (2 * pair) // nc
        c0 = (2 * pair) % nc

        @pl.when(c0 == 0)
        def _():
            if has_init:
                first_row = lax.broadcasted_iota(jnp.int32, (AUG, HEAD_DIM_M), 0) == 0
                for j in range(N_UNITS):
                    cta_s[j, 0:HEAD_DIM_M, :] = c0_ref[seq, 0, j].T
                    cta_s[j, HEAD_DIM_M:HEAD_AUG, :] = jnp.where(first_row, n0_ref[seq, 0, j:j + 1, :], 0.0)
                mst_s[...] = m0_ref[seq, 0]
            else:
                cta_s[...] = jnp.zeros(cta_s.shape, F32)
                mst_s[...] = jnp.zeros(mst_s.shape, F32)

        m_cur = [mst_s[d] for d in range(N_DIRS)]
        steps = []
        for sub in range(2):
            c = c0 + sub
            units = []
            for d in range(N_DIRS):
                blk = seq * nc + (c if d == 0 else nc - 1 - c)
                row0 = pl.multiple_of(blk * CHUNK, CHUNK)
                gr0 = pl.multiple_of(blk * GATE_ROWS, GATE_ROWS)
                b_last = rows_s[d, 2, pl.ds(gr0, GATE_ROWS), :]
                a_max = rows_s[d, 3, pl.ds(gr0, GATE_ROWS), :]
                w_rows = rows_s[d, 4, pl.ds(gr0, GATE_ROWS), :]
                m_prev = m_cur[d]
                mprev_s[d, blk] = m_prev
                m_new = b_last + jnp.maximum(m_prev, a_max)
                m_cur[d] = m_new
                decay = jnp.exp(b_last + m_prev - m_new)
                fac = jnp.exp(a_max + b_last - m_new)
                for hh in range(N_HEADS_M):
                    hs = slice(hh * HEAD_DIM_M, (hh + 1) * HEAD_DIM_M)
                    units.append(dict(d=d, hh=hh, j=d * N_HEADS_M + hh, blk=blk,
                                      kc=k_s[pl.ds(row0, CHUNK), hs], vta=vta_s[blk, hh], w=w_rows[hh:hh + 1, :],
                                      decay=decay[hh:hh + 1, :], fac=fac[hh:hh + 1, :]))
            steps.append(units)
        for d in range(N_DIRS):
            mst_s[d] = m_cur[d]
        for units in steps:
            for un in units:
                vw = (un["vta"].astype(F32) * un["w"]).astype(BF16)
                un["upd"] = _dot(vw, un["kc"])
        for sub in range(2):
            it = 2 * pair + sub
            cact_s[pl.ds(pl.multiple_of(it * CHUNK, CHUNK), CHUNK // 2), :] = conv_tile(seq, (c0 + sub) * CHUNK)
        cta = [cta_s[j] for j in range(N_UNITS)]
        for units in steps:
            for un in units:
                j = un["j"]
                ctab_s[un["d"], un["blk"], un["hh"]] = cta[j].astype(BF16)
                cta[j] = un["decay"] * cta[j] + un["fac"] * un["upd"]
        for j in range(N_UNITS):
            cta_s[j] = cta[j]

        if emit_state:
            @pl.when(c0 == nc - 2)
            def _():
                for j in range(N_UNITS):
                    co_ref[seq, j] = cta_s[j, 0:HEAD_DIM_M, :].T
                    no_ref[seq, j:j + 1, :] = cta_s[j, HEAD_DIM_M:HEAD_DIM_M + 1, :]
                mo_ref[seq] = mst_s[...]
        return carry

    lax.fori_loop(0, n_blk // 2, state_step, 0)

    def chunk_out(pair, carry):
        heads = []
        for blk in (2 * pair, 2 * pair + 1):
            row0 = pl.multiple_of(blk * CHUNK, CHUNK)
            gr0 = pl.multiple_of(blk * GATE_ROWS, GATE_ROWS)
            dirs = []
            for d in range(N_DIRS):
                m_prev = mprev_s[d, blk]
                mm = jnp.maximum(m_prev, rows_s[d, 1, pl.ds(gr0, GATE_ROWS), :])
                floor = jnp.exp(-(rows_s[d, 0, pl.ds(gr0, GATE_ROWS), :] + mm))
                dirs.append(dict(mm=mm, inter=jnp.exp(m_prev - mm), floor=floor,
                                 acol=acol_s[d, blk], mask=upper if d == 0 else lower))
            for hh in range(N_HEADS_M):
                hs = slice(hh * HEAD_DIM_M, (hh + 1) * HEAD_DIM_M)
                hd = dict(hh=hh, hs=hs, blk=blk, row0=row0, dirs=dirs,
                          kc=k_s[pl.ds(row0, CHUNK), hs], qt=qt_s[blk, hs, :], vta=vta_s[blk, hh])
                hd["st"] = _dot(hd["kc"], hd["qt"])
                hd["p1"] = [_dot(ctab_s[d, blk, hh], hd["qt"]) for d in range(N_DIRS)]
                heads.append(hd)
        for hd in heads:
            hh = hd["hh"]
            sm = []
            for dd in hd["dirs"]:
                z = dd["acol"][:, hh:hh + 1] - dd["mm"][hh:hh + 1, :]
                sm.append((hd["st"] * jnp.exp(jnp.where(dd["mask"], z, neg_inf))).astype(BF16))
            hd["p2"] = _dot(hd["vta"], jnp.concatenate(sm, axis=1))
        half = CHUNK // 2
        for blk in (2 * pair, 2 * pair + 1):
            row0 = pl.multiple_of(blk * CHUNK, CHUNK)
            cact_s[pl.ds(row0 + half, half), :] = conv_tile(blk // nc, (blk % nc) * CHUNK + half)
        for b2 in range(2):
            parts = []
            for hd in heads[b2 * N_HEADS_M:(b2 + 1) * N_HEADS_M]:
                hh = hd["hh"]
                hsum = None
                for d, dd in enumerate(hd["dirs"]):
                    numa = dd["inter"][hh:hh + 1, :] * hd["p1"][d] + hd["p2"][:, d * CHUNK:(d + 1) * CHUNK]
                    den = numa[HEAD_DIM_M:HEAD_DIM_M + 1, :]
                    ht = numa[0:HEAD_DIM_M, :] / jnp.maximum(jnp.abs(den), dd["floor"][hh:hh + 1, :])
                    hsum = ht if hsum is None else hsum + ht
                r = lax.rsqrt(jnp.mean(hsum * hsum, axis=0, keepdims=True) + EPS)
                parts.append((hsum * r * gmh_ref[0, hd["hs"], :]).T)
            row0 = heads[b2 * N_HEADS_M]["row0"]
            oc = jnp.concatenate(parts, axis=1) * so_s[pl.ds(row0, CHUNK), :].astype(F32)
            mix_s[pl.ds(row0, CHUNK), D_FOURIER + D_CONV:D_MIX] = oc.astype(BF16)
        return carry

    lax.fori_loop(0, n_blk // 2, chunk_out, 0)
    mix_s[:, D_FOURIER:D_FOURIER + D_CONV] = _dot(cact_s[...], wpw_ref[0]).astype(BF16)

    ga1 = mod_ref[0][:, 2 * D_MODEL:3 * D_MODEL]
    res = _dot(mix_s[...], wout_ref[0])
    x1_ref[...] = (load_x() + ga1 * res).reshape(bb, t, D_MODEL)


def _whole(a):
    nd = a.ndim
    return pl.BlockSpec(a.shape, lambda g: (0,) * nd, pipeline_mode=pl.Buffered(1))


def _layer_block(a, l):
    nd = a.ndim
    return pl.BlockSpec((1,) + a.shape[1:], lambda g: (l,) + (0,) * (nd - 1), pipeline_mode=pl.Buffered(1))


def _mixer_call(x, pos, mod_rows, mod_base, per_batch_mod, l, pw, state0, emit_state):
    nb, t, _ = x.shape
    bb = ROWS // t
    assert bb * t == ROWS and nb % bb == 0 and t % CHUNK == 0
    rows = ROWS
    n_blk = rows // CHUNK
    has_init = state0 is not None
    add_pos = pos is not None

    args = [x]
    in_specs = [pl.BlockSpec((bb, t, D_MODEL), lambda g: (g, 0, 0))]
    if add_pos:
        assert bb == 1
        args += list(pos)
        in_specs += [_whole(a) for a in pos]
    args.append(mod_rows)
    if per_batch_mod:
        assert bb == 1
        in_specs.append(pl.BlockSpec((1, 1, N_MOD * D_MODEL), lambda g: (mod_base + g, 0, 0)))
    else:
        in_specs.append(pl.BlockSpec((1, 1, N_MOD * D_MODEL), lambda g: (mod_base, 0, 0)))
    for name in ("g1", "win", "bg"):
        args.append(pw[name])
        in_specs.append(_layer_block(pw[name], l))
    for a in (pw["blk"], pw["dc"][t], pw["ds"][t]):
        args.append(a)
        in_specs.append(_whole(a))
    for name in ("wfn", "wdw", "bdw", "lng", "lnb", "wpw", "gmh"):
        args.append(pw[name])
        in_specs.append(_layer_block(pw[name], l))
    if has_init:
        c0, n0, m0 = state0
        args += [c0, n0, m0]
        in_specs += [
            pl.BlockSpec((bb, 1, N_UNITS, HEAD_DIM_M, HEAD_DIM_M), lambda g: (g, l, 0, 0, 0)),
            pl.BlockSpec((bb, 1, N_UNITS, HEAD_DIM_M), lambda g: (g, l, 0, 0)),
            pl.BlockSpec((bb, 1, N_DIRS, GATE_ROWS, CHUNK), lambda g: (g, l, 0, 0, 0)),
        ]
    args.append(pw["wout"])
    in_specs.append(_layer_block(pw["wout"], l))

    out_shape = [jax.ShapeDtypeStruct((nb, t, D_MODEL), F32)]
    out_specs = [pl.BlockSpec((bb, t, D_MODEL), lambda g: (g, 0, 0))]
    if emit_state:
        out_shape += [
            jax.ShapeDtypeStruct((nb, N_UNITS, HEAD_DIM_M, HEAD_DIM_M), F32),
            jax.ShapeDtypeStruct((nb, N_UNITS, HEAD_DIM_M), F32),
            jax.ShapeDtypeStruct((nb, N_DIRS, GATE_ROWS, CHUNK), F32),
        ]
        out_specs += [
            pl.BlockSpec((bb, N_UNITS, HEAD_DIM_M, HEAD_DIM_M), lambda g: (g, 0, 0, 0)),
            pl.BlockSpec((bb, N_UNITS, HEAD_DIM_M), lambda g: (g, 0, 0)),
            pl.BlockSpec((bb, N_DIRS, GATE_ROWS, CHUNK), lambda g: (g, 0, 0, 0)),
        ]
    scratch = [
        pltpu.VMEM((rows, D_FOURIER), BF16),
        pltpu.VMEM((bb, t + 2 * PAD_LO, D_CONV), F32),
        pltpu.VMEM((SUBLANES - 1, bb, t + SHIFT_ROWS, D_CONV), F32),
        pltpu.VMEM((rows, D_CONV), BF16),
        pltpu.VMEM((n_blk, D_MLSTM, CHUNK), BF16),
        pltpu.VMEM((rows, D_MLSTM), BF16),
        pltpu.VMEM((n_blk, N_HEADS_M, HEAD_AUG, CHUNK), BF16),
        pltpu.VMEM((rows, D_MLSTM), BF16),
        pltpu.VMEM((N_DIRS, 5, n_blk * GATE_ROWS, CHUNK), F32),
        pltpu.VMEM((N_DIRS, n_blk, CHUNK, N_HEADS_M), F32),
        pltpu.VMEM((N_DIRS, n_blk, GATE_ROWS, CHUNK), F32),
        pltpu.VMEM((N_DIRS, n_blk, N_HEADS_M, HEAD_AUG, HEAD_DIM_M), BF16),
        pltpu.VMEM((rows, D_MIX), BF16),
        pltpu.VMEM((N_UNITS, HEAD_AUG, HEAD_DIM_M), F32),
        pltpu.VMEM((N_DIRS, GATE_ROWS, CHUNK), F32),
    ]
    body = functools.partial(_mixer_body, bb, t, has_init, emit_state, add_pos)
    return pl.pallas_call(
        body,
        grid=(nb // bb,),
        in_specs=in_specs,
        out_specs=out_specs,
        out_shape=out_shape,
        scratch_shapes=scratch,
        compiler_params=pltpu.CompilerParams(dimension_semantics=("arbitrary",),
                                             vmem_limit_bytes=VMEM_LIMIT),
        name="mixer_t%d" % t,
    )(*args)


def _route_rows(logits_t, b_router_col):
    scores = _sigmoid(logits_t)
    sel = scores + b_router_col
    sel_r = [sel[e:e + 1, :] for e in range(N_EXPERTS)]
    sc_r = [scores[e:e + 1, :] for e in range(N_EXPERTS)]

    best = None
    best_v = None
    for g in range(N_EXPERT_GROUPS):
        a, b, c, d = sel_r[g * EXPERTS_PER_GROUP:(g + 1) * EXPERTS_PER_GROUP]
        hi1, lo1 = jnp.maximum(a, b), jnp.minimum(a, b)
        hi2, lo2 = jnp.maximum(c, d), jnp.minimum(c, d)
        gs = jnp.maximum(hi1, hi2) + jnp.maximum(jnp.minimum(hi1, hi2), jnp.maximum(lo1, lo2))
        if g == 0:
            best = jnp.zeros(gs.shape, jnp.int32)
            best_v = gs
        else:
            upd = gs > best_v
            best = jnp.where(upd, g, best)
            best_v = jnp.where(upd, gs, best_v)

    def pick(rows, j):
        out = rows[(N_EXPERT_GROUPS - 1) * EXPERTS_PER_GROUP + j]
        for g in range(N_EXPERT_GROUPS - 2, -1, -1):
            out = jnp.where(best == g, rows[g * EXPERTS_PER_GROUP + j], out)
        return out

    s = [pick(sel_r, j) for j in range(EXPERTS_PER_GROUP)]
    sc = [pick(sc_r, j) for j in range(EXPERTS_PER_GROUP)]
    rank = [jnp.zeros(best.shape, jnp.int32) for _ in range(EXPERTS_PER_GROUP)]
    for a in range(EXPERTS_PER_GROUP):
        for b in range(a + 1, EXPERTS_PER_GROUP):
            b_first = s[b] > s[a]
            rank[a] = rank[a] + b_first.astype(jnp.int32)
            rank[b] = rank[b] + (1 - b_first.astype(jnp.int32))
    w = [jnp.where(rank[j] < 2, sc[j], 0.0) for j in range(EXPERTS_PER_GROUP)]
    tot = w[0] + w[1] + w[2] + w[3]
    return best, [wj / tot for wj in w]


def _moe_body(is_last, x_ref, mod_ref, g2_ref, wr2_ref, brc_ref, utri_ref, wg_ref, wu_ref, wd_ref, fg_ref,
              o_ref, he_s, dest_s, destl_s, pt_s, ys_s, sm_s):
    gi = pl.program_id(1)
    tm = ROWS

    @pl.when(gi == 0)
    def _():
        mod = mod_ref[0]
        sh2 = mod[:, 3 * D_MODEL:4 * D_MODEL]
        sc2 = mod[:, 4 * D_MODEL:5 * D_MODEL]
        h = _rmsnorm_rows(x_ref[...], g2_ref[0]) * (1.0 + sc2) + sh2
        h_hi, h_lo = _hi_lo(h)
        he_s[:, 0:D_MODEL] = h_hi
        lg = _dot(jnp.concatenate([h_hi, h_lo], axis=1), wr2_ref[...])
        lg = lg[:, 0:N_EXPERTS] + lg[:, N_EXPERTS:2 * N_EXPERTS]
        logits_t = jnp.concatenate([lg, jnp.zeros((tm, MOE_EXT - N_EXPERTS), F32)], axis=1).T[0:N_EXPERTS, :]
        best, cw = _route_rows(logits_t, brc_ref[...])

        row_i = lax.broadcasted_iota(jnp.int32, (16, tm), 0)
        onehot = (row_i == best).astype(F32)
        before = _dot(onehot.astype(BF16), utri_ref[...])
        dest = jnp.zeros((1, tm), F32)
        off_blk = jnp.int32(0)
        for g in range(N_EXPERT_GROUPS):
            cnt = jnp.sum(onehot[g:g + 1, :]).astype(jnp.int32)
            n_blk = lax.shift_right_logical(cnt + (MOE_BLK - 1), int(math.log2(MOE_BLK)))
            sm_s[g] = off_blk
            sm_s[N_EXPERT_GROUPS + g] = n_blk
            base = (off_blk * MOE_BLK).astype(F32)
            dest = dest + onehot[g:g + 1, :] * (before[g:g + 1, :] + base)
            off_blk = off_blk + n_blk

        cw_hi = [wj.astype(BF16).astype(F32) for wj in cw]
        cw_lo = [wj - hj for wj, hj in zip(cw, cw_hi)]
        stack = jnp.concatenate([dest] + cw_hi + cw_lo + [jnp.zeros((MOE_EXT - 9, tm), F32)], axis=0)
        he_s[:, D_MODEL:D_MODEL + MOE_EXT] = stack.T.astype(BF16)
        dest_s[...] = jnp.broadcast_to(dest, (SUBLANES, tm))
        destl_s[...] = jnp.broadcast_to(dest, (MOE_BLK, tm)).T
        pt_s[:, ROWS:] = jnp.zeros((tm, MOE_CAP - ROWS), BF16)
        ys_s[ROWS:, :] = jnp.zeros((MOE_CAP - ROWS, D_MODEL), BF16)

    first_blk = sm_s[gi]
    n_blk = sm_s[N_EXPERT_GROUPS + gi]

    def expert_rows(r0, m):
        rows_f = (lax.broadcasted_iota(jnp.int32, (m, 1), 0) + r0).astype(F32)
        p_blk = jnp.where(dest_s[0:1, :] == rows_f, 1.0, 0.0).astype(BF16)
        for k in range(m // MOE_BLK):
            cols_f = (lax.broadcasted_iota(jnp.int32, (1, MOE_BLK), 1) + (r0 + k * MOE_BLK)).astype(F32)
            pt_s[:, pl.ds(pl.multiple_of(r0 + k * MOE_BLK, MOE_BLK), MOE_BLK)] = jnp.where(
                destl_s[...] == cols_f, 1.0, 0.0).astype(BF16)
        xe = _dot(p_blk, he_s[...])
        xs = xe[:, 0:D_MODEL].astype(BF16)
        acts = []
        for j in range(EXPERTS_PER_GROUP):
            gate = _dot(xs, wg_ref[0, j])
            up = _dot(xs, wu_ref[0, j])
            cwj = (xe[:, D_MODEL + 1 + j:D_MODEL + 2 + j]
                   + xe[:, D_MODEL + 1 + EXPERTS_PER_GROUP + j:D_MODEL + 2 + EXPERTS_PER_GROUP + j])
            acts.append((gate * _sigmoid(gate) * up * cwj).astype(BF16))
        ys_s[pl.ds(r0, m), :] = _dot(jnp.concatenate(acts, axis=1), wd_ref[0, 0]).astype(BF16)

    def block_pair(i, carry):
        expert_rows(pl.multiple_of((first_blk + 2 * i) * MOE_BLK, MOE_BLK), 2 * MOE_BLK)
        return carry

    lax.fori_loop(0, lax.shift_right_logical(n_blk, 1), block_pair, 0)

    @pl.when(lax.rem(n_blk, 2) == 1)
    def _():
        expert_rows(pl.multiple_of((first_blk + n_blk - 1) * MOE_BLK, MOE_BLK), MOE_BLK)

    @pl.when(gi == N_EXPERT_GROUPS - 1)
    def _():
        ga2 = mod_ref[0][:, 5 * D_MODEL:6 * D_MODEL]
        xo = x_ref[...] + ga2 * _dot(pt_s[...], ys_s[...])
        if is_last:
            xo = _rmsnorm_rows(xo, fg_ref[...])
        o_ref[...] = xo


def _moe_call(x2d, mod_rows, mod_base, per_tile_mod, l, pw, is_last):
    n_tok = x2d.shape[0]
    tm = ROWS
    assert n_tok % tm == 0
    if per_tile_mod:
        mod_spec = pl.BlockSpec((1, 1, N_MOD * D_MODEL), lambda i, g: (mod_base + i, 0, 0))
    else:
        mod_spec = pl.BlockSpec((1, 1, N_MOD * D_MODEL), lambda i, g: (mod_base, 0, 0))

    def const(shape):
        return pl.BlockSpec(shape, lambda i, g: (0,) * len(shape), pipeline_mode=pl.Buffered(1))

    def grp(shape):
        return pl.BlockSpec((1, EXPERTS_PER_GROUP) + shape, lambda i, g: (l, g, 0, 0))

    return pl.pallas_call(
        functools.partial(_moe_body, is_last),
        grid=(n_tok // tm, N_EXPERT_GROUPS),
        in_specs=[
            pl.BlockSpec((tm, D_MODEL), lambda i, g: (i, 0)),
            mod_spec,
            pl.BlockSpec((1, 1, D_MODEL), lambda i, g: (l, 0, 0), pipeline_mode=pl.Buffered(1)),
            const((2 * D_MODEL, 2 * N_EXPERTS)),
            const((N_EXPERTS, 1)),
            const((tm, tm)),
            grp((D_MODEL, D_EXPERT)),
            grp((D_MODEL, D_EXPERT)),
            pl.BlockSpec((1, 1, EXPERTS_PER_GROUP * D_EXPERT, D_MODEL), lambda i, g: (l, g, 0, 0)),
            const((1, D_MODEL)),
        ],
        out_specs=pl.BlockSpec((tm, D_MODEL), lambda i, g: (i, 0)),
        out_shape=jax.ShapeDtypeStruct((n_tok, D_MODEL), F32),
        scratch_shapes=[
            pltpu.VMEM((tm, D_MODEL + MOE_EXT), BF16),
            pltpu.VMEM((SUBLANES, tm), F32),
            pltpu.VMEM((tm, MOE_BLK), F32),
            pltpu.VMEM((tm, MOE_CAP), BF16),
            pltpu.VMEM((MOE_CAP, D_MODEL), BF16),
            pltpu.SMEM((2 * N_EXPERT_GROUPS,), jnp.int32),
        ],
        compiler_params=pltpu.CompilerParams(dimension_semantics=("arbitrary", "arbitrary"),
                                             vmem_limit_bytes=VMEM_LIMIT),
        name="moe",
    )(x2d, mod_rows, pw["g2"], pw["wr2"], pw["brc"], pw["utri"], pw["weg"], pw["weu"], pw["wed"], pw["fg"])


def _dft_tables(t):
    idx = np.arange(t, dtype=np.int64)
    ang = 2.0 * np.pi * ((idx[:, None] * idx[None, :]) % t).astype(np.float64) / t
    scale = 1.0 / math.sqrt(t * D_FOURIER_GROUP)
    return (np.cos(ang) * scale).astype(np.float32), (-np.sin(ang) * scale).astype(np.float32)


def _group_tables():
    idx = np.arange(D_FOURIER_GROUP, dtype=np.int64)
    ang = 2.0 * np.pi * ((idx[:, None] * idx[None, :]) % D_FOURIER_GROUP).astype(np.float64) / D_FOURIER_GROUP
    eye = np.eye(N_FOURIER_GROUPS)
    return np.concatenate([np.kron(eye, np.cos(ang)), np.kron(eye, np.sin(ang))], axis=1).astype(np.float32)


def _grid_pos_tables(n_tokens, dtype):
    quarter = D_MODEL // 4
    omega = 1.0 / (10000.0 ** (jnp.arange(quarter, dtype=jnp.float32) / quarter))

    def enc(p):
        a = p[:, None] * omega[None, :]
        return jnp.concatenate([jnp.sin(a), jnp.cos(a)], axis=-1).astype(dtype)

    return (enc(jnp.arange(n_tokens // GRID_W, dtype=jnp.float32)), enc(jnp.arange(GRID_W, dtype=jnp.float32)))


def _router_hi_lo(w_router):
    w_hi = w_router.astype(BF16)
    w_lo = (w_router - w_hi.astype(F32)).astype(BF16)
    return jnp.concatenate([jnp.concatenate([w_hi, w_lo], axis=1),
                            jnp.concatenate([w_hi, jnp.zeros_like(w_lo)], axis=1)], axis=0)


def _prepare(seq_lens, norm1_g, norm2_g, w_in, w_fnet, w_dw, b_dw, conv_ln_g, conv_ln_b, w_pw, b_gate, g_mh,
             w_out, w_router, b_router, w_exp_gate, w_exp_up, w_exp_down, final_g):
    dft = {t: _dft_tables(t) for t in sorted(set(seq_lens))}
    w_in_b = w_in.astype(BF16)
    tok = np.arange(ROWS)
    return dict(
        g1=norm1_g.reshape(DEPTH, 1, D_MODEL), g2=norm2_g.reshape(DEPTH, 1, D_MODEL),
        win=w_in_b, bg=b_gate.reshape(DEPTH, 1, N_GATES),
        blk=jnp.asarray(_group_tables()).astype(BF16),
        dc={t: jnp.asarray(v[0]).astype(BF16) for t, v in dft.items()},
        ds={t: jnp.asarray(v[1]).astype(BF16) for t, v in dft.items()},
        wfn=w_fnet.astype(BF16),
        wdw=jnp.concatenate([w_dw, jnp.zeros((DEPTH, 1, D_CONV), F32)], axis=1),
        bdw=b_dw.reshape(DEPTH, 1, D_CONV), lng=conv_ln_g.reshape(DEPTH, 1, D_CONV),
        lnb=conv_ln_b.reshape(DEPTH, 1, D_CONV), wpw=w_pw.astype(BF16),
        gmh=jnp.broadcast_to(g_mh.reshape(DEPTH, D_MLSTM, 1), (DEPTH, D_MLSTM, CHUNK)),
        wout=w_out.astype(BF16),
        wr2=_router_hi_lo(w_router), brc=b_router.reshape(N_EXPERTS, 1),
        utri=jnp.asarray((tok[:, None] < tok[None, :]).astype(np.float32)).astype(BF16),
        weg=w_exp_gate.astype(BF16), weu=w_exp_up.astype(BF16),
        wed=w_exp_down.reshape(DEPTH, N_EXPERT_GROUPS, EXPERTS_PER_GROUP * D_EXPERT, D_MODEL).astype(BF16),
        fg=final_g.reshape(1, D_MODEL),
    )


def kernel(x_prompt, x_sample, state_C, state_n, state_m, c, c_ctx, w_ada, b_ada, norm1_g, norm2_g, w_in, w_fnet, w_dw, b_dw, conv_ln_g, conv_ln_b, w_pw, b_gate, g_mh, w_out, w_router, b_router, w_exp_gate, w_exp_up, w_exp_down, final_g):
    bp, tp, _ = x_prompt.shape
    bs, ts, _ = x_sample.shape

    cv = jnp.zeros((MOD_ROWS, D_MODEL), F32).at[:bs].set(c).at[bs].set(c_ctx)
    mod_all = _ada_call(cv, w_ada, b_ada)
    mod_rows = mod_all.reshape(DEPTH * MOD_ROWS, 1, N_MOD * D_MODEL)

    pw = _prepare((tp, ts), norm1_g, norm2_g, w_in, w_fnet, w_dw, b_dw, conv_ln_g, conv_ln_b, w_pw, b_gate,
                  g_mh, w_out, w_router, b_router, w_exp_gate, w_exp_up, w_exp_down, final_g)
    pos = _grid_pos_tables(ts, x_sample.dtype)

    c0 = state_C.reshape(bs, DEPTH, N_UNITS, HEAD_DIM_M, HEAD_DIM_M)
    n0 = state_n.reshape(bs, DEPTH, N_UNITS, HEAD_DIM_M)
    m0 = jnp.pad(state_m, ((0, 0), (0, 0), (0, 0), (0, GATE_ROWS - N_HEADS_M)))
    m0 = jnp.broadcast_to(m0[..., None], (bs, DEPTH, N_DIRS, GATE_ROWS, CHUNK))

    xp, xs = x_prompt, x_sample
    cs, ns, ms = [], [], []
    for l in range(DEPTH):
        is_last = l == DEPTH - 1
        base = l * MOD_ROWS
        xp, c_l, n_l, m_l = _mixer_call(xp, None, mod_rows, base + bs, False, l, pw, None, True)
        cs.append(c_l)
        ns.append(n_l)
        ms.append(m_l[:, :, :N_HEADS_M, 0])
        xp = _moe_call(xp.reshape(bp * tp, D_MODEL), mod_rows, base + bs, False, l, pw, is_last)
        xp = xp.reshape(bp, tp, D_MODEL)

        (xs,) = _mixer_call(xs, pos if l == 0 else None, mod_rows, base, True, l, pw, (c0, n0, m0), False)
        xs = _moe_call(xs.reshape(bs * ts, D_MODEL), mod_rows, base, True, l, pw, is_last)
        xs = xs.reshape(bs, ts, D_MODEL)

    new_c = jnp.stack(cs, axis=1).reshape(bp, DEPTH, N_DIRS, N_HEADS_M, HEAD_DIM_M, HEAD_DIM_M)
    new_n = jnp.stack(ns, axis=1).reshape(bp, DEPTH, N_DIRS, N_HEADS_M, HEAD_DIM_M)
    new_m = jnp.stack(ms, axis=1).reshape(bp, DEPTH, N_DIRS, N_HEADS_M)
    return (xp, xs, new_c, new_n, new_m)
```

```python
import functools
import math

import numpy as np
import jax
import jax.numpy as jnp
from jax import lax
from jax.experimental import pallas as pl
from jax.experimental.pallas import tpu as pltpu

D_MODEL = 1024
DEPTH = 4
GRID_W = 64
EPS = 1e-6
D_FOURIER = 256
N_FOURIER_GROUPS = 4
D_FOURIER_GROUP = D_FOURIER // N_FOURIER_GROUPS
D_CONV = 256
CONV_WIDTH = 31
CONV_PAD = CONV_WIDTH // 2
D_MLSTM = 512
N_HEADS_M = 4
HEAD_DIM_M = D_MLSTM // N_HEADS_M
N_DIRS = 2
CHUNK = 128
N_GATES = 2 * N_DIRS * N_HEADS_M
N_UNITS = N_DIRS * N_HEADS_M
D_MIX = D_FOURIER + D_CONV + D_MLSTM
D_IN_PROJ = D_FOURIER + 2 * D_CONV + 4 * D_MLSTM + N_GATES
N_EXPERTS = 16
N_EXPERT_GROUPS = 4
EXPERTS_PER_GROUP = N_EXPERTS // N_EXPERT_GROUPS
D_EXPERT = 256
N_MOD = 6

O_B = D_FOURIER
O_Q = O_B + 2 * D_CONV
O_K = O_Q + D_MLSTM
O_V = O_K + D_MLSTM
O_O = O_V + D_MLSTM
O_G = O_O + D_MLSTM

ROWS = 1024
CONV_ROW_TILE = 64
PAD_LO = 16
MOD_ROWS = 16
SUBLANES = 8
SHIFT_ROWS = 24
AUG = 16
HEAD_AUG = HEAD_DIM_M + AUG
GATE_ROWS = 8
MOE_BLK = 128
MOE_CAP = ROWS + N_EXPERT_GROUPS * MOE_BLK
MOE_EXT = 128
VMEM_LIMIT = 60 * 1024 * 1024

F32 = jnp.float32
BF16 = jnp.bfloat16


def _dot(a, b):
    return jnp.dot(a, b, preferred_element_type=F32)


def _sigmoid(x):
    return 1.0 / (1.0 + jnp.exp(-x))


def _log_sigmoid(x):
    return jnp.minimum(x, 0.0) - jnp.log(1.0 + jnp.exp(-jnp.abs(x)))


def _hi_lo(a_f32):
    hi = a_f32.astype(BF16)
    return hi, (a_f32 - hi.astype(F32)).astype(BF16)


def _split_dot(a_f32, b_bf16):
    hi, lo = _hi_lo(a_f32)
    return _dot(hi, b_bf16) + _dot(lo, b_bf16)


def _split_dot_left(b_bf16, a_f32):
    hi, lo = _hi_lo(a_f32)
    return _dot(b_bf16, hi) + _dot(b_bf16, lo)


def _rmsnorm_rows(x, g):
    ms = jnp.mean(x * x, axis=-1, keepdims=True)
    return x * lax.rsqrt(ms + EPS) * g


def _ada_body(c_ref, w_ref, b_ref, o_ref):
    cv = c_ref[...]
    s_hi, s_lo = _hi_lo(cv * _sigmoid(cv))
    w_hi, w_lo = _hi_lo(w_ref[0])
    both = _dot(jnp.concatenate([s_hi, s_lo], axis=0), w_hi)
    o_ref[0] = both[0:MOD_ROWS] + both[MOD_ROWS:2 * MOD_ROWS] + _dot(s_hi, w_lo) + b_ref[0]


def _ada_call(cv, w_ada, b_ada):
    n_col = N_MOD * D_MODEL
    tn = D_MODEL
    return pl.pallas_call(
        _ada_body,
        grid=(DEPTH, n_col // tn),
        in_specs=[
            pl.BlockSpec((MOD_ROWS, D_MODEL), lambda l, j: (0, 0)),
            pl.BlockSpec((1, D_MODEL, tn), lambda l, j: (l, 0, j)),
            pl.BlockSpec((1, 1, tn), lambda l, j: (l, 0, j)),
        ],
        out_specs=pl.BlockSpec((1, MOD_ROWS, tn), lambda l, j: (l, 0, j)),
        out_shape=jax.ShapeDtypeStruct((DEPTH, MOD_ROWS, n_col), F32),
        compiler_params=pltpu.CompilerParams(dimension_semantics=("arbitrary", "arbitrary")),
        name="ada",
    )(cv, w_ada, b_ada.reshape(DEPTH, 1, n_col))


def _scan_max_lanes(x, reverse):
    lane = lax.broadcasted_iota(jnp.int32, x.shape, 1)
    neg_inf = jnp.float32(-jnp.inf)
    sh = 1
    while sh < CHUNK:
        if reverse:
            y = pltpu.roll(x, CHUNK - sh, axis=1)
            x = jnp.maximum(x, jnp.where(lane < CHUNK - sh, y, neg_inf))
        else:
            y = pltpu.roll(x, sh, axis=1)
            x = jnp.maximum(x, jnp.where(lane >= sh, y, neg_inf))
        sh *= 2
    return x


def _mixer_body(bb, t, has_init, emit_state, add_pos, *refs):
    rows = bb * t
    nc = t // CHUNK
    n_blk = rows // CHUNK
    refs = list(refs)
    x_ref = refs.pop(0)
    if add_pos:
        posr_ref, posc_ref = refs[:2]
        refs = refs[2:]
    (mod_ref, g1_ref, win_ref, bg_ref, blk_ref, dc_ref, ds_ref, wfn_ref,
     wdw_ref, bdw_ref, lng_ref, lnb_ref, wpw_ref, gmh_ref) = refs[:14]
    refs = refs[14:]
    if has_init:
        c0_ref, n0_ref, m0_ref = refs[:3]
        refs = refs[3:]
    wout_ref = refs.pop(0)
    x1_ref = refs.pop(0)
    if emit_state:
        co_ref, no_ref, mo_ref = refs[:3]
        refs = refs[3:]
    (xa_s, pad_s, shf_s, cact_s, qt_s, k_s, vta_s, so_s, rows_s, acol_s, mprev_s, ctab_s,
     mix_s, cta_s, mst_s) = refs

    if add_pos:
        half_d = D_MODEL // 2
        for g in range(t // GRID_W):
            rs = slice(g * GRID_W, (g + 1) * GRID_W)
            emb = jnp.concatenate([jnp.broadcast_to(posr_ref[g:g + 1, :], (GRID_W, half_d)), posc_ref[...]], axis=1)
            x1_ref[0, rs, :] = x_ref[0, rs, :] + emb

    def load_x():
        src = x1_ref if add_pos else x_ref
        return src[...].reshape(rows, D_MODEL)

    def w_in(lo, hi):
        return win_ref[0, :, lo:hi]

    x = load_x()
    mod = mod_ref[0]
    sh1 = mod[:, 0:D_MODEL]
    sc1 = mod[:, D_MODEL:2 * D_MODEL]
    h = _rmsnorm_rows(x, g1_ref[0]) * (1.0 + sc1) + sh1
    hb = h.astype(BF16)

    pb = _dot(hb, w_in(O_B, O_Q))
    u = pb[:, :D_CONV] * _sigmoid(pb[:, D_CONV:])
    zpad = jnp.zeros((PAD_LO, D_CONV), F32)
    for i in range(bb):
        pad_s[i, 0:PAD_LO, :] = zpad
        pad_s[i, PAD_LO:PAD_LO + t, :] = u[i * t:(i + 1) * t]
        pad_s[i, PAD_LO + t:2 * PAD_LO + t, :] = zpad

    for i in range(bb):
        for s8 in range(1, SUBLANES):
            shf_s[s8 - 1, i] = pad_s[i, s8:s8 + t + SHIFT_ROWS, :]

    def conv_tile(seq, r0):
        acc = jnp.broadcast_to(bdw_ref[0], (CONV_ROW_TILE, D_CONV))
        for j in range(CONV_WIDTH):
            q = j + PAD_LO - CONV_PAD
            win_rows = pl.ds(pl.multiple_of(r0 + (q // SUBLANES) * SUBLANES, SUBLANES), CONV_ROW_TILE)
            win = pad_s[seq, win_rows, :] if q % SUBLANES == 0 else shf_s[q % SUBLANES - 1, seq, win_rows, :]
            acc = acc + win * wdw_ref[0, j:j + 1, :]
        mu = jnp.mean(acc, axis=-1, keepdims=True)
        cen = acc - mu
        var = jnp.mean(cen * cen, axis=-1, keepdims=True)
        uf = cen * lax.rsqrt(var + EPS) * lng_ref[0] + lnb_ref[0]
        return (uf * _sigmoid(uf)).astype(BF16)

    xa_s[...] = _dot(hb, w_in(0, O_B)).astype(BF16)
    k_s[...] = (_dot(hb, w_in(O_K, O_V)) * (HEAD_DIM_M ** -0.5)).astype(BF16)
    og = _dot(hb, w_in(O_O, D_IN_PROJ))
    so_s[...] = _sigmoid(og[:, 0:D_MLSTM]).astype(BF16)
    qf = _dot(hb, w_in(O_Q, O_K))
    for b in range(n_blk):
        qt_s[b] = qf[b * CHUNK:(b + 1) * CHUNK, :].T.astype(BF16)
    vf = _dot(hb, w_in(O_V, O_O))
    ones_rows = jnp.where(lax.broadcasted_iota(jnp.int32, (AUG, CHUNK), 0) == 0, 1.0, 0.0).astype(BF16)
    for b in range(n_blk):
        vt = vf[b * CHUNK:(b + 1) * CHUNK, :].T.astype(BF16)
        for hh in range(N_HEADS_M):
            vta_s[b, hh, 0:HEAD_DIM_M, :] = vt[hh * HEAD_DIM_M:(hh + 1) * HEAD_DIM_M, :]
            vta_s[b, hh, HEAD_DIM_M:HEAD_AUG, :] = ones_rows

    r_i = lax.broadcasted_iota(jnp.int32, (CHUNK, CHUNK), 0)
    c_i = lax.broadcasted_iota(jnp.int32, (CHUNK, CHUNK), 1)
    lower = r_i >= c_i
    upper = r_i <= c_i
    tri_lo = lower.astype(BF16)
    tri_up = upper.astype(BF16)
    neg_inf = jnp.float32(-jnp.inf)
    n_gr = n_blk * GATE_ROWS

    gcol = og[:, D_MLSTM:D_MLSTM + N_GATES] + bg_ref[0]
    kind_c = (lax.broadcasted_iota(jnp.int32, gcol.shape, 1) // N_HEADS_M) % 2
    gcol = jnp.where(kind_c == 1, _log_sigmoid(gcol), gcol)
    grow = jnp.concatenate([gcol, jnp.zeros((rows, CHUNK - N_GATES), F32)], axis=1).T[0:N_GATES, :]
    for d in range(N_DIRS):
        g0 = d * 2 * N_HEADS_M
        ll = CHUNK - 1 if d == 0 else 0
        xg = jnp.concatenate([grow[g0:g0 + GATE_ROWS, b * CHUNK:(b + 1) * CHUNK] for b in range(n_blk)], axis=0)
        b_rows = pltpu.roll(_split_dot(xg, tri_up if d == 0 else tri_lo), n_gr - N_HEADS_M, axis=0)
        a_rows = xg - b_rows
        a_run = _scan_max_lanes(a_rows, d == 1)
        b_last = jnp.broadcast_to(b_rows[:, ll:ll + 1], (n_gr, CHUNK))
        a_max = jnp.broadcast_to(a_run[:, ll:ll + 1], (n_gr, CHUNK))
        rows_s[d, 0] = b_rows
        rows_s[d, 1] = a_run
        rows_s[d, 2] = b_last
        rows_s[d, 3] = a_max
        rows_s[d, 4] = jnp.exp(a_rows - a_max)
        for b in range(n_blk):
            gc = gcol[b * CHUNK:(b + 1) * CHUNK, :]
            bcol = _split_dot_left(tri_lo if d == 0 else tri_up, gc)
            acol_s[d, b] = gc[:, g0:g0 + N_HEADS_M] - bcol[:, g0 + N_HEADS_M:g0 + 2 * N_HEADS_M]

    for i in range(bb):
        xa_i = xa_s[i * t:(i + 1) * t, :]
        uu = _dot(xa_i, blk_ref[...])
        y = (_dot(dc_ref[...], uu[:, :D_FOURIER].astype(BF16))
             + _dot(ds_ref[...], uu[:, D_FOURIER:].astype(BF16)))
        mix_s[i * t:(i + 1) * t, 0:D_FOURIER] = _dot(y.astype(BF16), wfn_ref[0]).astype(BF16)

    def state_step(pair, carry):
        seq = (2 * pair) // nc
        c0 = (2 * pair) % nc

        @pl.when(c0 == 0)
        def _():
            if has_init:
                first_row = lax.broadcasted_iota(jnp.int32, (AUG, HEAD_DIM_M), 0) == 0
                for j in range(N_UNITS):
                    cta_s[j, 0:HEAD_DIM_M, :] = c0_ref[seq, 0, j].T
                    cta_s[j, HEAD_DIM_M:HEAD_AUG, :] = jnp.where(first_row, n0_ref[seq, 0, j:j + 1, :], 0.0)
                mst_s[...] = m0_ref[seq, 0]
            else:
                cta_s[...] = jnp.zeros(cta_s.shape, F32)
                mst_s[...] = jnp.zeros(mst_s.shape, F32)

        m_cur = [mst_s[d] for d in range(N_DIRS)]
        steps = []
        for sub in range(2):
            c = c0 + sub
            units = []
            for d in range(N_DIRS):
                blk = seq * nc + (c if d == 0 else nc - 1 - c)
                row0 = pl.multiple_of(blk * CHUNK, CHUNK)
                gr0 = pl.multiple_of(blk * GATE_ROWS, GATE_ROWS)
                b_last = rows_s[d, 2, pl.ds(gr0, GATE_ROWS), :]
                a_max = rows_s[d, 3, pl.ds(gr0, GATE_ROWS), :]
                w_rows = rows_s[d, 4, pl.ds(gr0, GATE_ROWS), :]
                m_prev = m_cur[d]
                mprev_s[d, blk] = m_prev
                m_new = b_last + jnp.maximum(m_prev, a_max)
                m_cur[d] = m_new
                decay = jnp.exp(b_last + m_prev - m_new)
                fac = jnp.exp(a_max + b_last - m_new)
                for hh in range(N_HEADS_M):
                    hs = slice(hh * HEAD_DIM_M, (hh + 1) * HEAD_DIM_M)
                    units.append(dict(d=d, hh=hh, j=d * N_HEADS_M + hh, blk=blk,
                                      kc=k_s[pl.ds(row0, CHUNK), hs], vta=vta_s[blk, hh], w=w_rows[hh:hh + 1, :],
                                      decay=decay[hh:hh + 1, :], fac=fac[hh:hh + 1, :]))
            steps.append(units)
        for d in range(N_DIRS):
            mst_s[d] = m_cur[d]
        for units in steps:
            for un in units:
                vw = (un["vta"].astype(F32) * un["w"]).astype(BF16)
                un["upd"] = _dot(vw, un["kc"])
        for sub in range(2):
            it = 2 * pair + sub
            cact_s[pl.ds(pl.multiple_of(it * CHUNK, CHUNK), CHUNK // 2), :] = conv_tile(seq, (c0 + sub) * CHUNK)
        cta = [cta_s[j] for j in range(N_UNITS)]
        for units in steps:
            for un in units:
                j = un["j"]
                ctab_s[un["blk"], un["hh"], un["d"] * HEAD_AUG:(un["d"] + 1) * HEAD_AUG, :] = cta[j].astype(BF16)
                cta[j] = un["decay"] * cta[j] + un["fac"] * un["upd"]
        for j in range(N_UNITS):
            cta_s[j] = cta[j]

        if emit_state:
            @pl.when(c0 == nc - 2)
            def _():
                for j in range(N_UNITS):
                    co_ref[seq, j] = cta_s[j, 0:HEAD_DIM_M, :].T
                    no_ref[seq, j:j + 1, :] = cta_s[j, HEAD_DIM_M:HEAD_DIM_M + 1, :]
                mo_ref[seq] = mst_s[...]
        return carry

    lax.fori_loop(0, n_blk // 2, state_step, 0)

    def chunk_out(pair, carry):
        heads = []
        for blk in (2 * pair, 2 * pair + 1):
            row0 = pl.multiple_of(blk * CHUNK, CHUNK)
            gr0 = pl.multiple_of(blk * GATE_ROWS, GATE_ROWS)
            dirs = []
            for d in range(N_DIRS):
                m_prev = mprev_s[d, blk]
                mm = jnp.maximum(m_prev, rows_s[d, 1, pl.ds(gr0, GATE_ROWS), :])
                floor = jnp.exp(-(rows_s[d, 0, pl.ds(gr0, GATE_ROWS), :] + mm))
                dirs.append(dict(mm=mm, inter=jnp.exp(m_prev - mm), floor=floor,
                                 acol=acol_s[d, blk], mask=upper if d == 0 else lower))
            for hh in range(N_HEADS_M):
                hs = slice(hh * HEAD_DIM_M, (hh + 1) * HEAD_DIM_M)
                hd = dict(hh=hh, hs=hs, blk=blk, row0=row0, dirs=dirs,
                          kc=k_s[pl.ds(row0, CHUNK), hs], qt=qt_s[blk, hs, :], vta=vta_s[blk, hh])
                sp = _dot(jnp.concatenate([hd["kc"], ctab_s[blk, hh]], axis=0), hd["qt"])
                hd["st"] = sp[0:CHUNK]
                hd["p1"] = [sp[CHUNK + d * HEAD_AUG:CHUNK + (d + 1) * HEAD_AUG] for d in range(N_DIRS)]
                heads.append(hd)
        for hd in heads:
            hh = hd["hh"]
            sm = []
            for dd in hd["dirs"]:
                z = dd["acol"][:, hh:hh + 1] - dd["mm"][hh:hh + 1, :]
                sm.append((hd["st"] * jnp.exp(jnp.where(dd["mask"], z, neg_inf))).astype(BF16))
            hd["p2"] = _dot(hd["vta"], jnp.concatenate(sm, axis=1))
        half = CHUNK // 2
        for blk in (2 * pair, 2 * pair + 1):
            row0 = pl.multiple_of(blk * CHUNK, CHUNK)
            cact_s[pl.ds(row0 + half, half), :] = conv_tile(blk // nc, (blk % nc) * CHUNK + half)
        for b2 in range(2):
            parts = []
            for hd in heads[b2 * N_HEADS_M:(b2 + 1) * N_HEADS_M]:
                hh = hd["hh"]
                hsum = None
                for d, dd in enumerate(hd["dirs"]):
                    numa = dd["inter"][hh:hh + 1, :] * hd["p1"][d] + hd["p2"][:, d * CHUNK:(d + 1) * CHUNK]
                    den = numa[HEAD_DIM_M:HEAD_DIM_M + 1, :]
                    ht = numa[0:HEAD_DIM_M, :] / jnp.maximum(jnp.abs(den), dd["floor"][hh:hh + 1, :])
                    hsum = ht if hsum is None else hsum + ht
                r = lax.rsqrt(jnp.mean(hsum * hsum, axis=0, keepdims=True) + EPS)
                parts.append((hsum * r * gmh_ref[0, hd["hs"], :]).T)
            row0 = heads[b2 * N_HEADS_M]["row0"]
            oc = jnp.concatenate(parts, axis=1) * so_s[pl.ds(row0, CHUNK), :].astype(F32)
            mix_s[pl.ds(row0, CHUNK), D_FOURIER + D_CONV:D_MIX] = oc.astype(BF16)
        return carry

    lax.fori_loop(0, n_blk // 2, chunk_out, 0)
    mix_s[:, D_FOURIER:D_FOURIER + D_CONV] = _dot(cact_s[...], wpw_ref[0]).astype(BF16)

    ga1 = mod_ref[0][:, 2 * D_MODEL:3 * D_MODEL]
    res = _dot(mix_s[...], wout_ref[0])
    x1_ref[...] = (load_x() + ga1 * res).reshape(bb, t, D_MODEL)


def _whole(a):
    nd = a.ndim
    return pl.BlockSpec(a.shape, lambda g: (0,) * nd, pipeline_mode=pl.Buffered(1))


def _layer_block(a, l):
    nd = a.ndim
    return pl.BlockSpec((1,) + a.shape[1:], lambda g: (l,) + (0,) * (nd - 1), pipeline_mode=pl.Buffered(1))


def _mixer_call(x, pos, mod_rows, mod_base, per_batch_mod, l, pw, state0, emit_state):
    nb, t, _ = x.shape
    bb = ROWS // t
    assert bb * t == ROWS and nb % bb == 0 and t % CHUNK == 0
    rows = ROWS
    n_blk = rows // CHUNK
    has_init = state0 is not None
    add_pos = pos is not None

    args = [x]
    in_specs = [pl.BlockSpec((bb, t, D_MODEL), lambda g: (g, 0, 0))]
    if add_pos:
        assert bb == 1
        args += list(pos)
        in_specs += [_whole(a) for a in pos]
    args.append(mod_rows)
    if per_batch_mod:
        assert bb == 1
        in_specs.append(pl.BlockSpec((1, 1, N_MOD * D_MODEL), lambda g: (mod_base + g, 0, 0)))
    else:
        in_specs.append(pl.BlockSpec((1, 1, N_MOD * D_MODEL), lambda g: (mod_base, 0, 0)))
    for name in ("g1", "win", "bg"):
        args.append(pw[name])
        in_specs.append(_layer_block(pw[name], l))
    for a in (pw["blk"], pw["dc"][t], pw["ds"][t]):
        args.append(a)
        in_specs.append(_whole(a))
    for name in ("wfn", "wdw", "bdw", "lng", "lnb", "wpw", "gmh"):
        args.append(pw[name])
        in_specs.append(_layer_block(pw[name], l))
    if has_init:
        c0, n0, m0 = state0
        args += [c0, n0, m0]
        in_specs += [
            pl.BlockSpec((bb, 1, N_UNITS, HEAD_DIM_M, HEAD_DIM_M), lambda g: (g, l, 0, 0, 0)),
            pl.BlockSpec((bb, 1, N_UNITS, HEAD_DIM_M), lambda g: (g, l, 0, 0)),
            pl.BlockSpec((bb, 1, N_DIRS, GATE_ROWS, CHUNK), lambda g: (g, l, 0, 0, 0)),
        ]
    args.append(pw["wout"])
    in_specs.append(_layer_block(pw["wout"], l))

    out_shape = [jax.ShapeDtypeStruct((nb, t, D_MODEL), F32)]
    out_specs = [pl.BlockSpec((bb, t, D_MODEL), lambda g: (g, 0, 0))]
    if emit_state:
        out_shape += [
            jax.ShapeDtypeStruct((nb, N_UNITS, HEAD_DIM_M, HEAD_DIM_M), F32),
            jax.ShapeDtypeStruct((nb, N_UNITS, HEAD_DIM_M), F32),
            jax.ShapeDtypeStruct((nb, N_DIRS, GATE_ROWS, CHUNK), F32),
        ]
        out_specs += [
            pl.BlockSpec((bb, N_UNITS, HEAD_DIM_M, HEAD_DIM_M), lambda g: (g, 0, 0, 0)),
            pl.BlockSpec((bb, N_UNITS, HEAD_DIM_M), lambda g: (g, 0, 0)),
            pl.BlockSpec((bb, N_DIRS, GATE_ROWS, CHUNK), lambda g: (g, 0, 0, 0)),
        ]
    scratch = [
        pltpu.VMEM((rows, D_FOURIER), BF16),
        pltpu.VMEM((bb, t + 2 * PAD_LO, D_CONV), F32),
        pltpu.VMEM((SUBLANES - 1, bb, t + SHIFT_ROWS, D_CONV), F32),
        pltpu.VMEM((rows, D_CONV), BF16),
        pltpu.VMEM((n_blk, D_MLSTM, CHUNK), BF16),
        pltpu.VMEM((rows, D_MLSTM), BF16),
        pltpu.VMEM((n_blk, N_HEADS_M, HEAD_AUG, CHUNK), BF16),
        pltpu.VMEM((rows, D_MLSTM), BF16),
        pltpu.VMEM((N_DIRS, 5, n_blk * GATE_ROWS, CHUNK), F32),
        pltpu.VMEM((N_DIRS, n_blk, CHUNK, N_HEADS_M), F32),
        pltpu.VMEM((N_DIRS, n_blk, GATE_ROWS, CHUNK), F32),
        pltpu.VMEM((n_blk, N_HEADS_M, N_DIRS * HEAD_AUG, HEAD_DIM_M), BF16),
        pltpu.VMEM((rows, D_MIX), BF16),
        pltpu.VMEM((N_UNITS, HEAD_AUG, HEAD_DIM_M), F32),
        pltpu.VMEM((N_DIRS, GATE_ROWS, CHUNK), F32),
    ]
    body = functools.partial(_mixer_body, bb, t, has_init, emit_state, add_pos)
    return pl.pallas_call(
        body,
        grid=(nb // bb,),
        in_specs=in_specs,
        out_specs=out_specs,
        out_shape=out_shape,
        scratch_shapes=scratch,
        compiler_params=pltpu.CompilerParams(dimension_semantics=("arbitrary",),
                                             vmem_limit_bytes=VMEM_LIMIT),
        name="mixer_t%d" % t,
    )(*args)


def _route_rows(logits_t, b_router_col):
    scores = _sigmoid(logits_t)
    sel = scores + b_router_col
    sel_r = [sel[e:e + 1, :] for e in range(N_EXPERTS)]
    sc_r = [scores[e:e + 1, :] for e in range(N_EXPERTS)]

    best = None
    best_v = None
    for g in range(N_EXPERT_GROUPS):
        a, b, c, d = sel_r[g * EXPERTS_PER_GROUP:(g + 1) * EXPERTS_PER_GROUP]
        hi1, lo1 = jnp.maximum(a, b), jnp.minimum(a, b)
        hi2, lo2 = jnp.maximum(c, d), jnp.minimum(c, d)
        gs = jnp.maximum(hi1, hi2) + jnp.maximum(jnp.minimum(hi1, hi2), jnp.maximum(lo1, lo2))
        if g == 0:
            best = jnp.zeros(gs.shape, jnp.int32)
            best_v = gs
        else:
            upd = gs > best_v
            best = jnp.where(upd, g, best)
            best_v = jnp.where(upd, gs, best_v)

    def pick(rows, j):
        out = rows[(N_EXPERT_GROUPS - 1) * EXPERTS_PER_GROUP + j]
        for g in range(N_EXPERT_GROUPS - 2, -1, -1):
            out = jnp.where(best == g, rows[g * EXPERTS_PER_GROUP + j], out)
        return out

    s = [pick(sel_r, j) for j in range(EXPERTS_PER_GROUP)]
    sc = [pick(sc_r, j) for j in range(EXPERTS_PER_GROUP)]
    rank = [jnp.zeros(best.shape, jnp.int32) for _ in range(EXPERTS_PER_GROUP)]
    for a in range(EXPERTS_PER_GROUP):
        for b in range(a + 1, EXPERTS_PER_GROUP):
            b_first = s[b] > s[a]
            rank[a] = rank[a] + b_first.astype(jnp.int32)
            rank[b] = rank[b] + (1 - b_first.astype(jnp.int32))
    w = [jnp.where(rank[j] < 2, sc[j], 0.0) for j in range(EXPERTS_PER_GROUP)]
    tot = w[0] + w[1] + w[2] + w[3]
    return best, [wj / tot for wj in w]


def _moe_body(is_last, x_ref, mod_ref, g2_ref, wr2_ref, brc_ref, utri_ref, wg_ref, wu_ref, wd_ref, fg_ref,
              o_ref, he_s, dest_s, destl_s, pt_s, ys_s, sm_s):
    gi = pl.program_id(1)
    tm = ROWS

    @pl.when(gi == 0)
    def _():
        mod = mod_ref[0]
        sh2 = mod[:, 3 * D_MODEL:4 * D_MODEL]
        sc2 = mod[:, 4 * D_MODEL:5 * D_MODEL]
        h = _rmsnorm_rows(x_ref[...], g2_ref[0]) * (1.0 + sc2) + sh2
        h_hi, h_lo = _hi_lo(h)
        he_s[:, 0:D_MODEL] = h_hi
        lg = _dot(jnp.concatenate([h_hi, h_lo], axis=1), wr2_ref[...])
        lg = lg[:, 0:N_EXPERTS] + lg[:, N_EXPERTS:2 * N_EXPERTS]
        logits_t = jnp.concatenate([lg, jnp.zeros((tm, MOE_EXT - N_EXPERTS), F32)], axis=1).T[0:N_EXPERTS, :]
        best, cw = _route_rows(logits_t, brc_ref[...])

        row_i = lax.broadcasted_iota(jnp.int32, (16, tm), 0)
        onehot = (row_i == best).astype(F32)
        before = _dot(onehot.astype(BF16), utri_ref[...])
        dest = jnp.zeros((1, tm), F32)
        off_blk = jnp.int32(0)
        for g in range(N_EXPERT_GROUPS):
            cnt = jnp.sum(onehot[g:g + 1, :]).astype(jnp.int32)
            n_blk = lax.shift_right_logical(cnt + (MOE_BLK - 1), int(math.log2(MOE_BLK)))
            sm_s[g] = off_blk
            sm_s[N_EXPERT_GROUPS + g] = n_blk
            base = (off_blk * MOE_BLK).astype(F32)
            dest = dest + onehot[g:g + 1, :] * (before[g:g + 1, :] + base)
            off_blk = off_blk + n_blk

        cw_hi = [wj.astype(BF16).astype(F32) for wj in cw]
        cw_lo = [wj - hj for wj, hj in zip(cw, cw_hi)]
        stack = jnp.concatenate([dest] + cw_hi + cw_lo + [jnp.zeros((MOE_EXT - 9, tm), F32)], axis=0)
        he_s[:, D_MODEL:D_MODEL + MOE_EXT] = stack.T.astype(BF16)
        dest_s[...] = jnp.broadcast_to(dest, (SUBLANES, tm))
        destl_s[...] = jnp.broadcast_to(dest, (MOE_BLK, tm)).T
        pt_s[:, ROWS:] = jnp.zeros((tm, MOE_CAP - ROWS), BF16)
        ys_s[ROWS:, :] = jnp.zeros((MOE_CAP - ROWS, D_MODEL), BF16)

    first_blk = sm_s[gi]
    n_blk = sm_s[N_EXPERT_GROUPS + gi]

    def expert_rows(r0, m):
        rows_f = (lax.broadcasted_iota(jnp.int32, (m, 1), 0) + r0).astype(F32)
        p_blk = jnp.where(dest_s[0:1, :] == rows_f, 1.0, 0.0).astype(BF16)
        for k in range(m // MOE_BLK):
            cols_f = (lax.broadcasted_iota(jnp.int32, (1, MOE_BLK), 1) + (r0 + k * MOE_BLK)).astype(F32)
            pt_s[:, pl.ds(pl.multiple_of(r0 + k * MOE_BLK, MOE_BLK), MOE_BLK)] = jnp.where(
                destl_s[...] == cols_f, 1.0, 0.0).astype(BF16)
        xe = _dot(p_blk, he_s[...])
        xs = xe[:, 0:D_MODEL].astype(BF16)
        acts = []
        for j in range(EXPERTS_PER_GROUP):
            gate = _dot(xs, wg_ref[0, j])
            up = _dot(xs, wu_ref[0, j])
            cwj = (xe[:, D_MODEL + 1 + j:D_MODEL + 2 + j]
                   + xe[:, D_MODEL + 1 + EXPERTS_PER_GROUP + j:D_MODEL + 2 + EXPERTS_PER_GROUP + j])
            acts.append((gate * _sigmoid(gate) * up * cwj).astype(BF16))
        ys_s[pl.ds(r0, m), :] = _dot(jnp.concatenate(acts, axis=1), wd_ref[0, 0]).astype(BF16)

    odd = lax.rem(n_blk, 2) == 1
    lead = jnp.where(odd, jnp.where(n_blk >= 3, 3, 1), 0)

    def block_pair(i, carry):
        expert_rows(pl.multiple_of((first_blk + lead + 2 * i) * MOE_BLK, MOE_BLK), 2 * MOE_BLK)
        return carry

    @pl.when(lead == 3)
    def _():
        expert_rows(pl.multiple_of(first_blk * MOE_BLK, MOE_BLK), 3 * MOE_BLK)

    @pl.when(lead == 1)
    def _():
        expert_rows(pl.multiple_of(first_blk * MOE_BLK, MOE_BLK), MOE_BLK)

    lax.fori_loop(0, lax.shift_right_logical(n_blk - lead, 1), block_pair, 0)

    @pl.when(gi == N_EXPERT_GROUPS - 1)
    def _():
        ga2 = mod_ref[0][:, 5 * D_MODEL:6 * D_MODEL]
        xo = x_ref[...] + ga2 * _dot(pt_s[...], ys_s[...])
        if is_last:
            xo = _rmsnorm_rows(xo, fg_ref[...])
        o_ref[...] = xo


def _moe_call(x2d, mod_rows, mod_base, per_tile_mod, l, pw, is_last):
    n_tok = x2d.shape[0]
    tm = ROWS
    assert n_tok % tm == 0
    if per_tile_mod:
        mod_spec = pl.BlockSpec((1, 1, N_MOD * D_MODEL), lambda i, g: (mod_base + i, 0, 0))
    else:
        mod_spec = pl.BlockSpec((1, 1, N_MOD * D_MODEL), lambda i, g: (mod_base, 0, 0))

    def const(shape):
        return pl.BlockSpec(shape, lambda i, g: (0,) * len(shape), pipeline_mode=pl.Buffered(1))

    def grp(shape):
        return pl.BlockSpec((1, EXPERTS_PER_GROUP) + shape, lambda i, g: (l, g, 0, 0))

    return pl.pallas_call(
        functools.partial(_moe_body, is_last),
        grid=(n_tok // tm, N_EXPERT_GROUPS),
        in_specs=[
            pl.BlockSpec((tm, D_MODEL), lambda i, g: (i, 0)),
            mod_spec,
            pl.BlockSpec((1, 1, D_MODEL), lambda i, g: (l, 0, 0), pipeline_mode=pl.Buffered(1)),
            const((2 * D_MODEL, 2 * N_EXPERTS)),
            const((N_EXPERTS, 1)),
            const((tm, tm)),
            grp((D_MODEL, D_EXPERT)),
            grp((D_MODEL, D_EXPERT)),
            pl.BlockSpec((1, 1, EXPERTS_PER_GROUP * D_EXPERT, D_MODEL), lambda i, g: (l, g, 0, 0)),
            const((1, D_MODEL)),
        ],
        out_specs=pl.BlockSpec((tm, D_MODEL), lambda i, g: (i, 0)),
        out_shape=jax.ShapeDtypeStruct((n_tok, D_MODEL), F32),
        scratch_shapes=[
            pltpu.VMEM((tm, D_MODEL + MOE_EXT), BF16),
            pltpu.VMEM((SUBLANES, tm), F32),
            pltpu.VMEM((tm, MOE_BLK), F32),
            pltpu.VMEM((tm, MOE_CAP), BF16),
            pltpu.VMEM((MOE_CAP, D_MODEL), BF16),
            pltpu.SMEM((2 * N_EXPERT_GROUPS,), jnp.int32),
        ],
        compiler_params=pltpu.CompilerParams(dimension_semantics=("arbitrary", "arbitrary"),
                                             vmem_limit_bytes=VMEM_LIMIT),
        name="moe",
    )(x2d, mod_rows, pw["g2"], pw["wr2"], pw["brc"], pw["utri"], pw["weg"], pw["weu"], pw["wed"], pw["fg"])


def _dft_tables(t):
    idx = np.arange(t, dtype=np.int64)
    ang = 2.0 * np.pi * ((idx[:, None] * idx[None, :]) % t).astype(np.float64) / t
    scale = 1.0 / math.sqrt(t * D_FOURIER_GROUP)
    return (np.cos(ang) * scale).astype(np.float32), (-np.sin(ang) * scale).astype(np.float32)


def _group_tables():
    idx = np.arange(D_FOURIER_GROUP, dtype=np.int64)
    ang = 2.0 * np.pi * ((idx[:, None] * idx[None, :]) % D_FOURIER_GROUP).astype(np.float64) / D_FOURIER_GROUP
    eye = np.eye(N_FOURIER_GROUPS)
    return np.concatenate([np.kron(eye, np.cos(ang)), np.kron(eye, np.sin(ang))], axis=1).astype(np.float32)


def _grid_pos_tables(n_tokens, dtype):
    quarter = D_MODEL // 4
    omega = 1.0 / (10000.0 ** (jnp.arange(quarter, dtype=jnp.float32) / quarter))

    def enc(p):
        a = p[:, None] * omega[None, :]
        return jnp.concatenate([jnp.sin(a), jnp.cos(a)], axis=-1).astype(dtype)

    return (enc(jnp.arange(n_tokens // GRID_W, dtype=jnp.float32)), enc(jnp.arange(GRID_W, dtype=jnp.float32)))


def _router_hi_lo(w_router):
    w_hi = w_router.astype(BF16)
    w_lo = (w_router - w_hi.astype(F32)).astype(BF16)
    return jnp.concatenate([jnp.concatenate([w_hi, w_lo], axis=1),
                            jnp.concatenate([w_hi, jnp.zeros_like(w_lo)], axis=1)], axis=0)


def _prepare(seq_lens, norm1_g, norm2_g, w_in, w_fnet, w_dw, b_dw, conv_ln_g, conv_ln_b, w_pw, b_gate, g_mh,
             w_out, w_router, b_router, w_exp_gate, w_exp_up, w_exp_down, final_g):
    dft = {t: _dft_tables(t) for t in sorted(set(seq_lens))}
    w_in_b = w_in.astype(BF16)
    tok = np.arange(ROWS)
    return dict(
        g1=norm1_g.reshape(DEPTH, 1, D_MODEL), g2=norm2_g.reshape(DEPTH, 1, D_MODEL),
        win=w_in_b, bg=b_gate.reshape(DEPTH, 1, N_GATES),
        blk=jnp.asarray(_group_tables()).astype(BF16),
        dc={t: jnp.asarray(v[0]).astype(BF16) for t, v in dft.items()},
        ds={t: jnp.asarray(v[1]).astype(BF16) for t, v in dft.items()},
        wfn=w_fnet.astype(BF16),
        wdw=jnp.concatenate([w_dw, jnp.zeros((DEPTH, 1, D_CONV), F32)], axis=1),
        bdw=b_dw.reshape(DEPTH, 1, D_CONV), lng=conv_ln_g.reshape(DEPTH, 1, D_CONV),
        lnb=conv_ln_b.reshape(DEPTH, 1, D_CONV), wpw=w_pw.astype(BF16),
        gmh=jnp.broadcast_to(g_mh.reshape(DEPTH, D_MLSTM, 1), (DEPTH, D_MLSTM, CHUNK)),
        wout=w_out.astype(BF16),
        wr2=_router_hi_lo(w_router), brc=b_router.reshape(N_EXPERTS, 1),
        utri=jnp.asarray((tok[:, None] < tok[None, :]).astype(np.float32)).astype(BF16),
        weg=w_exp_gate.astype(BF16), weu=w_exp_up.astype(BF16),
        wed=w_exp_down.reshape(DEPTH, N_EXPERT_GROUPS, EXPERTS_PER_GROUP * D_EXPERT, D_MODEL).astype(BF16),
        fg=final_g.reshape(1, D_MODEL),
    )


def kernel(x_prompt, x_sample, state_C, state_n, state_m, c, c_ctx, w_ada, b_ada, norm1_g, norm2_g, w_in, w_fnet, w_dw, b_dw, conv_ln_g, conv_ln_b, w_pw, b_gate, g_mh, w_out, w_router, b_router, w_exp_gate, w_exp_up, w_exp_down, final_g):
    bp, tp, _ = x_prompt.shape
    bs, ts, _ = x_sample.shape

    cv = jnp.zeros((MOD_ROWS, D_MODEL), F32).at[:bs].set(c).at[bs].set(c_ctx)
    mod_all = _ada_call(cv, w_ada, b_ada)
    mod_rows = mod_all.reshape(DEPTH * MOD_ROWS, 1, N_MOD * D_MODEL)

    pw = _prepare((tp, ts), norm1_g, norm2_g, w_in, w_fnet, w_dw, b_dw, conv_ln_g, conv_ln_b, w_pw, b_gate,
                  g_mh, w_out, w_router, b_router, w_exp_gate, w_exp_up, w_exp_down, final_g)
    pos = _grid_pos_tables(ts, x_sample.dtype)

    c0 = state_C.reshape(bs, DEPTH, N_UNITS, HEAD_DIM_M, HEAD_DIM_M)
    n0 = state_n.reshape(bs, DEPTH, N_UNITS, HEAD_DIM_M)
    m0 = jnp.pad(state_m, ((0, 0), (0, 0), (0, 0), (0, GATE_ROWS - N_HEADS_M)))
    m0 = jnp.broadcast_to(m0[..., None], (bs, DEPTH, N_DIRS, GATE_ROWS, CHUNK))

    xp, xs = x_prompt, x_sample
    cs, ns, ms = [], [], []
    for l in range(DEPTH):
        is_last = l == DEPTH - 1
        base = l * MOD_ROWS
        xp, c_l, n_l, m_l = _mixer_call(xp, None, mod_rows, base + bs, False, l, pw, None, True)
        cs.append(c_l)
        ns.append(n_l)
        ms.append(m_l[:, :, :N_HEADS_M, 0])
        xp = _moe_call(xp.reshape(bp * tp, D_MODEL), mod_rows, base + bs, False, l, pw, is_last)
        xp = xp.reshape(bp, tp, D_MODEL)

        (xs,) = _mixer_call(xs, pos if l == 0 else None, mod_rows, base, True, l, pw, (c0, n0, m0), False)
        xs = _moe_call(xs.reshape(bs * ts, D_MODEL), mod_rows, base, True, l, pw, is_last)
        xs = xs.reshape(bs, ts, D_MODEL)

    new_c = jnp.stack(cs, axis=1).reshape(bp, DEPTH, N_DIRS, N_HEADS_M, HEAD_DIM_M, HEAD_DIM_M)
    new_n = jnp.stack(ns, axis=1).reshape(bp, DEPTH, N_DIRS, N_HEADS_M, HEAD_DIM_M)
    new_m = jnp.stack(ms, axis=1).reshape(bp, DEPTH, N_DIRS, N_HEADS_M)
    return (xp, xs, new_c, new_n, new_m)
```

```python
import functools
import math

import numpy as np
import jax
import jax.numpy as jnp
from jax import lax
from jax.experimental import pallas as pl
from jax.experimental.pallas import tpu as pltpu

D_MODEL = 1024
DEPTH = 4
GRID_W = 64
EPS = 1e-6
D_FOURIER = 256
N_FOURIER_GROUPS = 4
D_FOURIER_GROUP = D_FOURIER // N_FOURIER_GROUPS
D_CONV = 256
CONV_WIDTH = 31
CONV_PAD = CONV_WIDTH // 2
D_MLSTM = 512
N_HEADS_M = 4
HEAD_DIM_M = D_MLSTM // N_HEADS_M
N_DIRS = 2
CHUNK = 128
N_GATES = 2 * N_DIRS * N_HEADS_M
N_UNITS = N_DIRS * N_HEADS_M
D_MIX = D_FOURIER + D_CONV + D_MLSTM
D_IN_PROJ = D_FOURIER + 2 * D_CONV + 4 * D_MLSTM + N_GATES
N_EXPERTS = 16
N_EXPERT_GROUPS = 4
EXPERTS_PER_GROUP = N_EXPERTS // N_EXPERT_GROUPS
D_EXPERT = 256
N_MOD = 6

O_B = D_FOURIER
O_Q = O_B + 2 * D_CONV
O_K = O_Q + D_MLSTM
O_V = O_K + D_MLSTM
O_O = O_V + D_MLSTM
O_G = O_O + D_MLSTM

ROWS = 1024
CONV_ROW_TILE = 64
PAD_LO = 16
MOD_ROWS = 16
SUBLANES = 8
SHIFT_ROWS = 24
AUG = 16
HEAD_AUG = HEAD_DIM_M + AUG
GATE_ROWS = 8
MOE_BLK = 128
MOE_CAP = ROWS + N_EXPERT_GROUPS * MOE_BLK
MOE_EXT = 128
VMEM_LIMIT = 60 * 1024 * 1024

F32 = jnp.float32
BF16 = jnp.bfloat16


def _dot(a, b):
    return jnp.dot(a, b, preferred_element_type=F32)


def _dot_nt(a, b):
    return lax.dot_general(a, b, (((1,), (1,)), ((), ())), preferred_element_type=F32)


def _sigmoid(x):
    return 1.0 / (1.0 + jnp.exp(-x))


def _log_sigmoid(x):
    return jnp.minimum(x, 0.0) - jnp.log(1.0 + jnp.exp(-jnp.abs(x)))


def _hi_lo(a_f32):
    hi = a_f32.astype(BF16)
    return hi, (a_f32 - hi.astype(F32)).astype(BF16)


def _split_dot(a_f32, b_bf16):
    hi, lo = _hi_lo(a_f32)
    return _dot(hi, b_bf16) + _dot(lo, b_bf16)


def _split_dot_left(b_bf16, a_f32):
    hi, lo = _hi_lo(a_f32)
    return _dot(b_bf16, hi) + _dot(b_bf16, lo)


def _rmsnorm_rows(x, g):
    ms = jnp.mean(x * x, axis=-1, keepdims=True)
    return x * lax.rsqrt(ms + EPS) * g


def _ada_body(c_ref, w_ref, b_ref, o_ref):
    cv = c_ref[...]
    s_hi, s_lo = _hi_lo(cv * _sigmoid(cv))
    w_hi, w_lo = _hi_lo(w_ref[0])
    both = _dot(jnp.concatenate([s_hi, s_lo], axis=0), w_hi)
    o_ref[0] = both[0:MOD_ROWS] + both[MOD_ROWS:2 * MOD_ROWS] + _dot(s_hi, w_lo) + b_ref[0]


def _ada_call(cv, w_ada, b_ada):
    n_col = N_MOD * D_MODEL
    tn = D_MODEL
    return pl.pallas_call(
        _ada_body,
        grid=(DEPTH, n_col // tn),
        in_specs=[
            pl.BlockSpec((MOD_ROWS, D_MODEL), lambda l, j: (0, 0)),
            pl.BlockSpec((1, D_MODEL, tn), lambda l, j: (l, 0, j)),
            pl.BlockSpec((1, 1, tn), lambda l, j: (l, 0, j)),
        ],
        out_specs=pl.BlockSpec((1, MOD_ROWS, tn), lambda l, j: (l, 0, j)),
        out_shape=jax.ShapeDtypeStruct((DEPTH, MOD_ROWS, n_col), F32),
        compiler_params=pltpu.CompilerParams(dimension_semantics=("arbitrary", "arbitrary")),
        name="ada",
    )(cv, w_ada, b_ada.reshape(DEPTH, 1, n_col))


def _scan_max_lanes(x, reverse):
    lane = lax.broadcasted_iota(jnp.int32, x.shape, 1)
    neg_inf = jnp.float32(-jnp.inf)
    sh = 1
    while sh < CHUNK:
        if reverse:
            y = pltpu.roll(x, CHUNK - sh, axis=1)
            x = jnp.maximum(x, jnp.where(lane < CHUNK - sh, y, neg_inf))
        else:
            y = pltpu.roll(x, sh, axis=1)
            x = jnp.maximum(x, jnp.where(lane >= sh, y, neg_inf))
        sh *= 2
    return x


def _mixer_body(bb, t, has_init, emit_state, add_pos, *refs):
    rows = bb * t
    nc = t // CHUNK
    n_blk = rows // CHUNK
    refs = list(refs)
    x_ref = refs.pop(0)
    if add_pos:
        posr_ref, posc_ref = refs[:2]
        refs = refs[2:]
    (mod_ref, g1_ref, win_ref, bg_ref, blk_ref, dc_ref, ds_ref, wfn_ref,
     wdw_ref, bdw_ref, lng_ref, lnb_ref, wpw_ref, gmh_ref) = refs[:14]
    refs = refs[14:]
    if has_init:
        c0_ref, n0_ref, m0_ref = refs[:3]
        refs = refs[3:]
    wout_ref = refs.pop(0)
    x1_ref = refs.pop(0)
    if emit_state:
        co_ref, no_ref, mo_ref = refs[:3]
        refs = refs[3:]
    (xa_s, pad_s, shf_s, cact_s, qt_s, k_s, vta_s, so_s, rows_s, acol_s, mprev_s, ctab_s,
     mix_s, cta_s, mst_s) = refs

    if add_pos:
        half_d = D_MODEL // 2
        for g in range(t // GRID_W):
            rs = slice(g * GRID_W, (g + 1) * GRID_W)
            emb = jnp.concatenate([jnp.broadcast_to(posr_ref[g:g + 1, :], (GRID_W, half_d)), posc_ref[...]], axis=1)
            x1_ref[0, rs, :] = x_ref[0, rs, :] + emb

    def load_x():
        src = x1_ref if add_pos else x_ref
        return src[...].reshape(rows, D_MODEL)

    def w_in(lo, hi):
        return win_ref[0, lo:hi, :]

    x = load_x()
    mod = mod_ref[0]
    sh1 = mod[:, 0:D_MODEL]
    sc1 = mod[:, D_MODEL:2 * D_MODEL]
    h = _rmsnorm_rows(x, g1_ref[0]) * (1.0 + sc1) + sh1
    hb = h.astype(BF16)

    pb = _dot_nt(hb, w_in(O_B, O_Q))
    u = pb[:, :D_CONV] * _sigmoid(pb[:, D_CONV:])
    zpad = jnp.zeros((PAD_LO, D_CONV), F32)
    for i in range(bb):
        pad_s[i, 0:PAD_LO, :] = zpad
        pad_s[i, PAD_LO:PAD_LO + t, :] = u[i * t:(i + 1) * t]
        pad_s[i, PAD_LO + t:2 * PAD_LO + t, :] = zpad

    for i in range(bb):
        for s8 in range(1, SUBLANES):
            shf_s[s8 - 1, i] = pad_s[i, s8:s8 + t + SHIFT_ROWS, :]

    def conv_tile(seq, r0):
        acc = jnp.broadcast_to(bdw_ref[0], (CONV_ROW_TILE, D_CONV))
        for j in range(CONV_WIDTH):
            q = j + PAD_LO - CONV_PAD
            win_rows = pl.ds(pl.multiple_of(r0 + (q // SUBLANES) * SUBLANES, SUBLANES), CONV_ROW_TILE)
            win = pad_s[seq, win_rows, :] if q % SUBLANES == 0 else shf_s[q % SUBLANES - 1, seq, win_rows, :]
            acc = acc + win * wdw_ref[0, j:j + 1, :]
        mu = jnp.mean(acc, axis=-1, keepdims=True)
        cen = acc - mu
        var = jnp.mean(cen * cen, axis=-1, keepdims=True)
        uf = cen * lax.rsqrt(var + EPS) * lng_ref[0] + lnb_ref[0]
        return (uf * _sigmoid(uf)).astype(BF16)

    xa_s[...] = _dot_nt(hb, w_in(0, O_B)).astype(BF16)
    k_s[...] = (_dot_nt(hb, w_in(O_K, O_V)) * (HEAD_DIM_M ** -0.5)).astype(BF16)
    og = _dot_nt(hb, w_in(O_O, D_IN_PROJ))
    so_s[...] = _sigmoid(og[:, 0:D_MLSTM]).astype(BF16)
    qf = _dot_nt(hb, w_in(O_Q, O_K))
    for b in range(n_blk):
        qt_s[b] = qf[b * CHUNK:(b + 1) * CHUNK, :].T.astype(BF16)
    vf = _dot_nt(hb, w_in(O_V, O_O))
    ones_rows = jnp.where(lax.broadcasted_iota(jnp.int32, (AUG, CHUNK), 0) == 0, 1.0, 0.0).astype(BF16)
    for b in range(n_blk):
        vt = vf[b * CHUNK:(b + 1) * CHUNK, :].T.astype(BF16)
        for hh in range(N_HEADS_M):
            vta_s[b, hh, 0:HEAD_DIM_M, :] = vt[hh * HEAD_DIM_M:(hh + 1) * HEAD_DIM_M, :]
            vta_s[b, hh, HEAD_DIM_M:HEAD_AUG, :] = ones_rows

    r_i = lax.broadcasted_iota(jnp.int32, (CHUNK, CHUNK), 0)
    c_i = lax.broadcasted_iota(jnp.int32, (CHUNK, CHUNK), 1)
    lower = r_i >= c_i
    upper = r_i <= c_i
    tri_lo = lower.astype(BF16)
    tri_up = upper.astype(BF16)
    neg_inf = jnp.float32(-jnp.inf)
    n_gr = n_blk * GATE_ROWS

    gcol = og[:, D_MLSTM:D_MLSTM + N_GATES] + bg_ref[0]
    kind_c = (lax.broadcasted_iota(jnp.int32, gcol.shape, 1) // N_HEADS_M) % 2
    gcol = jnp.where(kind_c == 1, _log_sigmoid(gcol), gcol)
    grow = jnp.concatenate([gcol, jnp.zeros((rows, CHUNK - N_GATES), F32)], axis=1).T[0:N_GATES, :]
    for d in range(N_DIRS):
        g0 = d * 2 * N_HEADS_M
        ll = CHUNK - 1 if d == 0 else 0
        xg = jnp.concatenate([grow[g0:g0 + GATE_ROWS, b * CHUNK:(b + 1) * CHUNK] for b in range(n_blk)], axis=0)
        b_rows = pltpu.roll(_split_dot(xg, tri_up if d == 0 else tri_lo), n_gr - N_HEADS_M, axis=0)
        a_rows = xg - b_rows
        a_run = _scan_max_lanes(a_rows, d == 1)
        b_last = jnp.broadcast_to(b_rows[:, ll:ll + 1], (n_gr, CHUNK))
        a_max = jnp.broadcast_to(a_run[:, ll:ll + 1], (n_gr, CHUNK))
        rows_s[d, 0] = b_rows
        rows_s[d, 1] = a_run
        rows_s[d, 2] = b_last
        rows_s[d, 3] = a_max
        rows_s[d, 4] = jnp.exp(a_rows - a_max)
        for b in range(n_blk):
            gc = gcol[b * CHUNK:(b + 1) * CHUNK, :]
            bcol = _split_dot_left(tri_lo if d == 0 else tri_up, gc)
            acol_s[d, b] = gc[:, g0:g0 + N_HEADS_M] - bcol[:, g0 + N_HEADS_M:g0 + 2 * N_HEADS_M]

    for i in range(bb):
        xa_i = xa_s[i * t:(i + 1) * t, :]
        uu = _dot(xa_i, blk_ref[...])
        y = (_dot(dc_ref[...], uu[:, :D_FOURIER].astype(BF16))
             + _dot(ds_ref[...], uu[:, D_FOURIER:].astype(BF16)))
        mix_s[i * t:(i + 1) * t, 0:D_FOURIER] = _dot(y.astype(BF16), wfn_ref[0]).astype(BF16)

    def state_step(pair, carry):
        seq = (2 * pair) // nc
        c0 = (2 * pair) % nc

        @pl.when(c0 == 0)
        def _():
            if has_init:
                first_row = lax.broadcasted_iota(jnp.int32, (AUG, HEAD_DIM_M), 0) == 0
                for j in range(N_UNITS):
                    cta_s[j, 0:HEAD_DIM_M, :] = c0_ref[seq, 0, j].T
                    cta_s[j, HEAD_DIM_M:HEAD_AUG, :] = jnp.where(first_row, n0_ref[seq, 0, j:j + 1, :], 0.0)
                mst_s[...] = m0_ref[seq, 0]
            else:
                cta_s[...] = jnp.zeros(cta_s.shape, F32)
                mst_s[...] = jnp.zeros(mst_s.shape, F32)

        m_cur = [mst_s[d] for d in range(N_DIRS)]
        steps = []
        for sub in range(2):
            c = c0 + sub
            units = []
            for d in range(N_DIRS):
                blk = seq * nc + (c if d == 0 else nc - 1 - c)
                row0 = pl.multiple_of(blk * CHUNK, CHUNK)
                gr0 = pl.multiple_of(blk * GATE_ROWS, GATE_ROWS)
                b_last = rows_s[d, 2, pl.ds(gr0, GATE_ROWS), :]
                a_max = rows_s[d, 3, pl.ds(gr0, GATE_ROWS), :]
                w_rows = rows_s[d, 4, pl.ds(gr0, GATE_ROWS), :]
                m_prev = m_cur[d]
                mprev_s[d, blk] = m_prev
                m_new = b_last + jnp.maximum(m_prev, a_max)
                m_cur[d] = m_new
                decay = jnp.exp(b_last + m_prev - m_new)
                fac = jnp.exp(a_max + b_last - m_new)
                for hh in range(N_HEADS_M):
                    hs = slice(hh * HEAD_DIM_M, (hh + 1) * HEAD_DIM_M)
                    units.append(dict(d=d, hh=hh, j=d * N_HEADS_M + hh, blk=blk,
                                      kc=k_s[pl.ds(row0, CHUNK), hs], vta=vta_s[blk, hh], w=w_rows[hh:hh + 1, :],
                                      decay=decay[hh:hh + 1, :], fac=fac[hh:hh + 1, :]))
            steps.append(units)
        for d in range(N_DIRS):
            mst_s[d] = m_cur[d]
        for units in steps:
            for un in units:
                vw = (un["vta"].astype(F32) * un["w"]).astype(BF16)
                un["upd"] = _dot(vw, un["kc"])
        for sub in range(2):
            it = 2 * pair + sub
            cact_s[pl.ds(pl.multiple_of(it * CHUNK, CHUNK), CHUNK // 2), :] = conv_tile(seq, (c0 + sub) * CHUNK)
        cta = [cta_s[j] for j in range(N_UNITS)]
        for units in steps:
            for un in units:
                j = un["j"]
                ctab_s[un["blk"], un["hh"], un["d"] * HEAD_AUG:(un["d"] + 1) * HEAD_AUG, :] = cta[j].astype(BF16)
                cta[j] = un["decay"] * cta[j] + un["fac"] * un["upd"]
        for j in range(N_UNITS):
            cta_s[j] = cta[j]

        if emit_state:
            @pl.when(c0 == nc - 2)
            def _():
                for j in range(N_UNITS):
                    co_ref[seq, j] = cta_s[j, 0:HEAD_DIM_M, :].T
                    no_ref[seq, j:j + 1, :] = cta_s[j, HEAD_DIM_M:HEAD_DIM_M + 1, :]
                mo_ref[seq] = mst_s[...]
        return carry

    lax.fori_loop(0, n_blk // 2, state_step, 0)

    def chunk_out(pair, carry):
        heads = []
        for blk in (2 * pair, 2 * pair + 1):
            row0 = pl.multiple_of(blk * CHUNK, CHUNK)
            gr0 = pl.multiple_of(blk * GATE_ROWS, GATE_ROWS)
            dirs = []
            for d in range(N_DIRS):
                m_prev = mprev_s[d, blk]
                mm = jnp.maximum(m_prev, rows_s[d, 1, pl.ds(gr0, GATE_ROWS), :])
                floor = jnp.exp(-(rows_s[d, 0, pl.ds(gr0, GATE_ROWS), :] + mm))
                dirs.append(dict(mm=mm, inter=jnp.exp(m_prev - mm), floor=floor,
                                 acol=acol_s[d, blk], mask=upper if d == 0 else lower))
            for hh in range(N_HEADS_M):
                hs = slice(hh * HEAD_DIM_M, (hh + 1) * HEAD_DIM_M)
                hd = dict(hh=hh, hs=hs, blk=blk, row0=row0, dirs=dirs,
                          kc=k_s[pl.ds(row0, CHUNK), hs], qt=qt_s[blk, hs, :], vta=vta_s[blk, hh])
                sp = _dot(jnp.concatenate([hd["kc"], ctab_s[blk, hh]], axis=0), hd["qt"])
                hd["st"] = sp[0:CHUNK]
                hd["p1"] = [sp[CHUNK + d * HEAD_AUG:CHUNK + (d + 1) * HEAD_AUG] for d in range(N_DIRS)]
                heads.append(hd)
        for hd in heads:
            hh = hd["hh"]
            sm = []
            for dd in hd["dirs"]:
                z = dd["acol"][:, hh:hh + 1] - dd["mm"][hh:hh + 1, :]
                sm.append((hd["st"] * jnp.exp(jnp.where(dd["mask"], z, neg_inf))).astype(BF16))
            hd["p2"] = _dot(hd["vta"], jnp.concatenate(sm, axis=1))
        half = CHUNK // 2
        for blk in (2 * pair, 2 * pair + 1):
            row0 = pl.multiple_of(blk * CHUNK, CHUNK)
            cact_s[pl.ds(row0 + half, half), :] = conv_tile(blk // nc, (blk % nc) * CHUNK + half)
        for b2 in range(2):
            parts = []
            for hd in heads[b2 * N_HEADS_M:(b2 + 1) * N_HEADS_M]:
                hh = hd["hh"]
                hsum = None
                for d, dd in enumerate(hd["dirs"]):
                    numa = dd["inter"][hh:hh + 1, :] * hd["p1"][d] + hd["p2"][:, d * CHUNK:(d + 1) * CHUNK]
                    den = numa[HEAD_DIM_M:HEAD_DIM_M + 1, :]
                    ht = numa[0:HEAD_DIM_M, :] / jnp.maximum(jnp.abs(den), dd["floor"][hh:hh + 1, :])
                    hsum = ht if hsum is None else hsum + ht
                r = lax.rsqrt(jnp.mean(hsum * hsum, axis=0, keepdims=True) + EPS)
                parts.append((hsum * r * gmh_ref[0, hd["hs"], :]).T)
            row0 = heads[b2 * N_HEADS_M]["row0"]
            oc = jnp.concatenate(parts, axis=1) * so_s[pl.ds(row0, CHUNK), :].astype(F32)
            mix_s[pl.ds(row0, CHUNK), D_FOURIER + D_CONV:D_MIX] = oc.astype(BF16)
        return carry

    lax.fori_loop(0, n_blk // 2, chunk_out, 0)
    mix_s[:, D_FOURIER:D_FOURIER + D_CONV] = _dot(cact_s[...], wpw_ref[0]).astype(BF16)

    ga1 = mod_ref[0][:, 2 * D_MODEL:3 * D_MODEL]
    res = _dot(mix_s[...], wout_ref[0])
    x1_ref[...] = (load_x() + ga1 * res).reshape(bb, t, D_MODEL)


def _whole(a):
    nd = a.ndim
    return pl.BlockSpec(a.shape, lambda g: (0,) * nd, pipeline_mode=pl.Buffered(1))


def _layer_block(a, l):
    nd = a.ndim
    return pl.BlockSpec((1,) + a.shape[1:], lambda g: (l,) + (0,) * (nd - 1), pipeline_mode=pl.Buffered(1))


def _mixer_call(x, pos, mod_rows, mod_base, per_batch_mod, l, pw, state0, emit_state):
    nb, t, _ = x.shape
    bb = ROWS // t
    assert bb * t == ROWS and nb % bb == 0 and t % CHUNK == 0
    rows = ROWS
    n_blk = rows // CHUNK
    has_init = state0 is not None
    add_pos = pos is not None

    args = [x]
    in_specs = [pl.BlockSpec((bb, t, D_MODEL), lambda g: (g, 0, 0))]
    if add_pos:
        assert bb == 1
        args += list(pos)
        in_specs += [_whole(a) for a in pos]
    args.append(mod_rows)
    if per_batch_mod:
        assert bb == 1
        in_specs.append(pl.BlockSpec((1, 1, N_MOD * D_MODEL), lambda g: (mod_base + g, 0, 0)))
    else:
        in_specs.append(pl.BlockSpec((1, 1, N_MOD * D_MODEL), lambda g: (mod_base, 0, 0)))
    for name in ("g1", "win", "bg"):
        args.append(pw[name])
        in_specs.append(_layer_block(pw[name], l))
    for a in (pw["blk"], pw["dc"][t], pw["ds"][t]):
        args.append(a)
        in_specs.append(_whole(a))
    for name in ("wfn", "wdw", "bdw", "lng", "lnb", "wpw", "gmh"):
        args.append(pw[name])
        in_specs.append(_layer_block(pw[name], l))
    if has_init:
        c0, n0, m0 = state0
        args += [c0, n0, m0]
        in_specs += [
            pl.BlockSpec((bb, 1, N_UNITS, HEAD_DIM_M, HEAD_DIM_M), lambda g: (g, l, 0, 0, 0)),
            pl.BlockSpec((bb, 1, N_UNITS, HEAD_DIM_M), lambda g: (g, l, 0, 0)),
            pl.BlockSpec((bb, 1, N_DIRS, GATE_ROWS, CHUNK), lambda g: (g, l, 0, 0, 0)),
        ]
    args.append(pw["wout"])
    in_specs.append(_layer_block(pw["wout"], l))

    out_shape = [jax.ShapeDtypeStruct((nb, t, D_MODEL), F32)]
    out_specs = [pl.BlockSpec((bb, t, D_MODEL), lambda g: (g, 0, 0))]
    if emit_state:
        out_shape += [
            jax.ShapeDtypeStruct((nb, N_UNITS, HEAD_DIM_M, HEAD_DIM_M), F32),
            jax.ShapeDtypeStruct((nb, N_UNITS, HEAD_DIM_M), F32),
            jax.ShapeDtypeStruct((nb, N_DIRS, GATE_ROWS, CHUNK), F32),
        ]
        out_specs += [
            pl.BlockSpec((bb, N_UNITS, HEAD_DIM_M, HEAD_DIM_M), lambda g: (g, 0, 0, 0)),
            pl.BlockSpec((bb, N_UNITS, HEAD_DIM_M), lambda g: (g, 0, 0)),
            pl.BlockSpec((bb, N_DIRS, GATE_ROWS, CHUNK), lambda g: (g, 0, 0, 0)),
        ]
    scratch = [
        pltpu.VMEM((rows, D_FOURIER), BF16),
        pltpu.VMEM((bb, t + 2 * PAD_LO, D_CONV), F32),
        pltpu.VMEM((SUBLANES - 1, bb, t + SHIFT_ROWS, D_CONV), F32),
        pltpu.VMEM((rows, D_CONV), BF16),
        pltpu.VMEM((n_blk, D_MLSTM, CHUNK), BF16),
        pltpu.VMEM((rows, D_MLSTM), BF16),
        pltpu.VMEM((n_blk, N_HEADS_M, HEAD_AUG, CHUNK), BF16),
        pltpu.VMEM((rows, D_MLSTM), BF16),
        pltpu.VMEM((N_DIRS, 5, n_blk * GATE_ROWS, CHUNK), F32),
        pltpu.VMEM((N_DIRS, n_blk, CHUNK, N_HEADS_M), F32),
        pltpu.VMEM((N_DIRS, n_blk, GATE_ROWS, CHUNK), F32),
        pltpu.VMEM((n_blk, N_HEADS_M, N_DIRS * HEAD_AUG, HEAD_DIM_M), BF16),
        pltpu.VMEM((rows, D_MIX), BF16),
        pltpu.VMEM((N_UNITS, HEAD_AUG, HEAD_DIM_M), F32),
        pltpu.VMEM((N_DIRS, GATE_ROWS, CHUNK), F32),
    ]
    body = functools.partial(_mixer_body, bb, t, has_init, emit_state, add_pos)
    return pl.pallas_call(
        body,
        grid=(nb // bb,),
        in_specs=in_specs,
        out_specs=out_specs,
        out_shape=out_shape,
        scratch_shapes=scratch,
        compiler_params=pltpu.CompilerParams(dimension_semantics=("arbitrary",),
                                             vmem_limit_bytes=VMEM_LIMIT),
        name="mixer_t%d" % t,
    )(*args)


def _route_rows(logits_t, b_router_col):
    scores = _sigmoid(logits_t)
    sel = scores + b_router_col
    sel_r = [sel[e:e + 1, :] for e in range(N_EXPERTS)]
    sc_r = [scores[e:e + 1, :] for e in range(N_EXPERTS)]

    best = None
    best_v = None
    for g in range(N_EXPERT_GROUPS):
        a, b, c, d = sel_r[g * EXPERTS_PER_GROUP:(g + 1) * EXPERTS_PER_GROUP]
        hi1, lo1 = jnp.maximum(a, b), jnp.minimum(a, b)
        hi2, lo2 = jnp.maximum(c, d), jnp.minimum(c, d)
        gs = jnp.maximum(hi1, hi2) + jnp.maximum(jnp.minimum(hi1, hi2), jnp.maximum(lo1, lo2))
        if g == 0:
            best = jnp.zeros(gs.shape, jnp.int32)
            best_v = gs
        else:
            upd = gs > best_v
            best = jnp.where(upd, g, best)
            best_v = jnp.where(upd, gs, best_v)

    def pick(rows, j):
        out = rows[(N_EXPERT_GROUPS - 1) * EXPERTS_PER_GROUP + j]
        for g in range(N_EXPERT_GROUPS - 2, -1, -1):
            out = jnp.where(best == g, rows[g * EXPERTS_PER_GROUP + j], out)
        return out

    s = [pick(sel_r, j) for j in range(EXPERTS_PER_GROUP)]
    sc = [pick(sc_r, j) for j in range(EXPERTS_PER_GROUP)]
    rank = [jnp.zeros(best.shape, jnp.int32) for _ in range(EXPERTS_PER_GROUP)]
    for a in range(EXPERTS_PER_GROUP):
        for b in range(a + 1, EXPERTS_PER_GROUP):
            b_first = s[b] > s[a]
            rank[a] = rank[a] + b_first.astype(jnp.int32)
            rank[b] = rank[b] + (1 - b_first.astype(jnp.int32))
    w = [jnp.where(rank[j] < 2, sc[j], 0.0) for j in range(EXPERTS_PER_GROUP)]
    tot = w[0] + w[1] + w[2] + w[3]
    return best, [wj / tot for wj in w]


def _moe_body(is_last, x_ref, mod_ref, g2_ref, wr2_ref, brc_ref, utri_ref, wg_ref, wu_ref, wd_ref, fg_ref,
              o_ref, he_s, dest_s, destl_s, pt_s, ys_s, sm_s):
    gi = pl.program_id(1)
    tm = ROWS

    @pl.when(gi == 0)
    def _():
        mod = mod_ref[0]
        sh2 = mod[:, 3 * D_MODEL:4 * D_MODEL]
        sc2 = mod[:, 4 * D_MODEL:5 * D_MODEL]
        h = _rmsnorm_rows(x_ref[...], g2_ref[0]) * (1.0 + sc2) + sh2
        h_hi, h_lo = _hi_lo(h)
        he_s[:, 0:D_MODEL] = h_hi
        lg = _dot(jnp.concatenate([h_hi, h_lo], axis=1), wr2_ref[...])
        lg = lg[:, 0:N_EXPERTS] + lg[:, N_EXPERTS:2 * N_EXPERTS]
        logits_t = jnp.concatenate([lg, jnp.zeros((tm, MOE_EXT - N_EXPERTS), F32)], axis=1).T[0:N_EXPERTS, :]
        best, cw = _route_rows(logits_t, brc_ref[...])

        row_i = lax.broadcasted_iota(jnp.int32, (16, tm), 0)
        onehot = (row_i == best).astype(F32)
        before = _dot(onehot.astype(BF16), utri_ref[...])
        dest = jnp.zeros((1, tm), F32)
        off_blk = jnp.int32(0)
        for g in range(N_EXPERT_GROUPS):
            cnt = jnp.sum(onehot[g:g + 1, :]).astype(jnp.int32)
            n_blk = lax.shift_right_logical(cnt + (MOE_BLK - 1), int(math.log2(MOE_BLK)))
            sm_s[g] = off_blk
            sm_s[N_EXPERT_GROUPS + g] = n_blk
            base = (off_blk * MOE_BLK).astype(F32)
            dest = dest + onehot[g:g + 1, :] * (before[g:g + 1, :] + base)
            off_blk = off_blk + n_blk

        cw_hi = [wj.astype(BF16).astype(F32) for wj in cw]
        cw_lo = [wj - hj for wj, hj in zip(cw, cw_hi)]
        stack = jnp.concatenate([dest] + cw_hi + cw_lo + [jnp.zeros((MOE_EXT - 9, tm), F32)], axis=0)
        he_s[:, D_MODEL:D_MODEL + MOE_EXT] = stack.T.astype(BF16)
        dest_s[...] = jnp.broadcast_to(dest, (SUBLANES, tm))
        destl_s[...] = jnp.broadcast_to(dest, (MOE_BLK, tm)).T
        pt_s[:, ROWS:] = jnp.zeros((tm, MOE_CAP - ROWS), BF16)
        ys_s[ROWS:, :] = jnp.zeros((MOE_CAP - ROWS, D_MODEL), BF16)

    first_blk = sm_s[gi]
    n_blk = sm_s[N_EXPERT_GROUPS + gi]

    def expert_rows(r0, m):
        rows_f = (lax.broadcasted_iota(jnp.int32, (m, 1), 0) + r0).astype(F32)
        p_blk = jnp.where(dest_s[0:1, :] == rows_f, 1.0, 0.0).astype(BF16)
        for k in range(m // MOE_BLK):
            cols_f = (lax.broadcasted_iota(jnp.int32, (1, MOE_BLK), 1) + (r0 + k * MOE_BLK)).astype(F32)
            pt_s[:, pl.ds(pl.multiple_of(r0 + k * MOE_BLK, MOE_BLK), MOE_BLK)] = jnp.where(
                destl_s[...] == cols_f, 1.0, 0.0).astype(BF16)
        xe = _dot(p_blk, he_s[...])
        xs = xe[:, 0:D_MODEL].astype(BF16)
        acts = []
        for j in range(EXPERTS_PER_GROUP):
            gate = _dot(xs, wg_ref[0, j])
            up = _dot(xs, wu_ref[0, j])
            cwj = (xe[:, D_MODEL + 1 + j:D_MODEL + 2 + j]
                   + xe[:, D_MODEL + 1 + EXPERTS_PER_GROUP + j:D_MODEL + 2 + EXPERTS_PER_GROUP + j])
            acts.append((gate * _sigmoid(gate) * up * cwj).astype(BF16))
        ys_s[pl.ds(r0, m), :] = _dot(jnp.concatenate(acts, axis=1), wd_ref[0, 0]).astype(BF16)

    odd = lax.rem(n_blk, 2) == 1
    lead = jnp.where(odd, jnp.where(n_blk >= 3, 3, 1), 0)

    def block_pair(i, carry):
        expert_rows(pl.multiple_of((first_blk + lead + 2 * i) * MOE_BLK, MOE_BLK), 2 * MOE_BLK)
        return carry

    @pl.when(lead == 3)
    def _():
        expert_rows(pl.multiple_of(first_blk * MOE_BLK, MOE_BLK), 3 * MOE_BLK)

    @pl.when(lead == 1)
    def _():
        expert_rows(pl.multiple_of(first_blk * MOE_BLK, MOE_BLK), MOE_BLK)

    lax.fori_loop(0, lax.shift_right_logical(n_blk - lead, 1), block_pair, 0)

    @pl.when(gi == N_EXPERT_GROUPS - 1)
    def _():
        ga2 = mod_ref[0][:, 5 * D_MODEL:6 * D_MODEL]
        xo = x_ref[...] + ga2 * _dot(pt_s[...], ys_s[...])
        if is_last:
            xo = _rmsnorm_rows(xo, fg_ref[...])
        o_ref[...] = xo


def _moe_call(x2d, mod_rows, mod_base, per_tile_mod, l, pw, is_last):
    n_tok = x2d.shape[0]
    tm = ROWS
    assert n_tok % tm == 0
    if per_tile_mod:
        mod_spec = pl.BlockSpec((1, 1, N_MOD * D_MODEL), lambda i, g: (mod_base + i, 0, 0))
    else:
        mod_spec = pl.BlockSpec((1, 1, N_MOD * D_MODEL), lambda i, g: (mod_base, 0, 0))

    def const(shape):
        return pl.BlockSpec(shape, lambda i, g: (0,) * len(shape), pipeline_mode=pl.Buffered(1))

    def grp(shape):
        return pl.BlockSpec((1, EXPERTS_PER_GROUP) + shape, lambda i, g: (l, g, 0, 0))

    return pl.pallas_call(
        functools.partial(_moe_body, is_last),
        grid=(n_tok // tm, N_EXPERT_GROUPS),
        in_specs=[
            pl.BlockSpec((tm, D_MODEL), lambda i, g: (i, 0)),
            mod_spec,
            pl.BlockSpec((1, 1, D_MODEL), lambda i, g: (l, 0, 0), pipeline_mode=pl.Buffered(1)),
            const((2 * D_MODEL, 2 * N_EXPERTS)),
            const((N_EXPERTS, 1)),
            const((tm, tm)),
            grp((D_MODEL, D_EXPERT)),
            grp((D_MODEL, D_EXPERT)),
            pl.BlockSpec((1, 1, EXPERTS_PER_GROUP * D_EXPERT, D_MODEL), lambda i, g: (l, g, 0, 0)),
            const((1, D_MODEL)),
        ],
        out_specs=pl.BlockSpec((tm, D_MODEL), lambda i, g: (i, 0)),
        out_shape=jax.ShapeDtypeStruct((n_tok, D_MODEL), F32),
        scratch_shapes=[
            pltpu.VMEM((tm, D_MODEL + MOE_EXT), BF16),
            pltpu.VMEM((SUBLANES, tm), F32),
            pltpu.VMEM((tm, MOE_BLK), F32),
            pltpu.VMEM((tm, MOE_CAP), BF16),
            pltpu.VMEM((MOE_CAP, D_MODEL), BF16),
            pltpu.SMEM((2 * N_EXPERT_GROUPS,), jnp.int32),
        ],
        compiler_params=pltpu.CompilerParams(dimension_semantics=("arbitrary", "arbitrary"),
                                             vmem_limit_bytes=VMEM_LIMIT),
        name="moe",
    )(x2d, mod_rows, pw["g2"], pw["wr2"], pw["brc"], pw["utri"], pw["weg"], pw["weu"], pw["wed"], pw["fg"])


def _dft_tables(t):
    idx = np.arange(t, dtype=np.int64)
    ang = 2.0 * np.pi * ((idx[:, None] * idx[None, :]) % t).astype(np.float64) / t
    scale = 1.0 / math.sqrt(t * D_FOURIER_GROUP)
    return (np.cos(ang) * scale).astype(np.float32), (-np.sin(ang) * scale).astype(np.float32)


def _group_tables():
    idx = np.arange(D_FOURIER_GROUP, dtype=np.int64)
    ang = 2.0 * np.pi * ((idx[:, None] * idx[None, :]) % D_FOURIER_GROUP).astype(np.float64) / D_FOURIER_GROUP
    eye = np.eye(N_FOURIER_GROUPS)
    return np.concatenate([np.kron(eye, np.cos(ang)), np.kron(eye, np.sin(ang))], axis=1).astype(np.float32)


def _grid_pos_tables(n_tokens, dtype):
    quarter = D_MODEL // 4
    omega = 1.0 / (10000.0 ** (jnp.arange(quarter, dtype=jnp.float32) / quarter))

    def enc(p):
        a = p[:, None] * omega[None, :]
        return jnp.concatenate([jnp.sin(a), jnp.cos(a)], axis=-1).astype(dtype)

    return (enc(jnp.arange(n_tokens // GRID_W, dtype=jnp.float32)), enc(jnp.arange(GRID_W, dtype=jnp.float32)))


def _router_hi_lo(w_router):
    w_hi = w_router.astype(BF16)
    w_lo = (w_router - w_hi.astype(F32)).astype(BF16)
    return jnp.concatenate([jnp.concatenate([w_hi, w_lo], axis=1),
                            jnp.concatenate([w_hi, jnp.zeros_like(w_lo)], axis=1)], axis=0)


def _prepare(seq_lens, norm1_g, norm2_g, w_in, w_fnet, w_dw, b_dw, conv_ln_g, conv_ln_b, w_pw, b_gate, g_mh,
             w_out, w_router, b_router, w_exp_gate, w_exp_up, w_exp_down, final_g):
    dft = {t: _dft_tables(t) for t in sorted(set(seq_lens))}
    w_in_b = w_in.astype(BF16)
    tok = np.arange(ROWS)
    return dict(
        g1=norm1_g.reshape(DEPTH, 1, D_MODEL), g2=norm2_g.reshape(DEPTH, 1, D_MODEL),
        win=jnp.swapaxes(w_in_b, 1, 2), bg=b_gate.reshape(DEPTH, 1, N_GATES),
        blk=jnp.asarray(_group_tables()).astype(BF16),
        dc={t: jnp.asarray(v[0]).astype(BF16) for t, v in dft.items()},
        ds={t: jnp.asarray(v[1]).astype(BF16) for t, v in dft.items()},
        wfn=w_fnet.astype(BF16),
        wdw=jnp.concatenate([w_dw, jnp.zeros((DEPTH, 1, D_CONV), F32)], axis=1),
        bdw=b_dw.reshape(DEPTH, 1, D_CONV), lng=conv_ln_g.reshape(DEPTH, 1, D_CONV),
        lnb=conv_ln_b.reshape(DEPTH, 1, D_CONV), wpw=w_pw.astype(BF16),
        gmh=jnp.broadcast_to(g_mh.reshape(DEPTH, D_MLSTM, 1), (DEPTH, D_MLSTM, CHUNK)),
        wout=w_out.astype(BF16),
        wr2=_router_hi_lo(w_router), brc=b_router.reshape(N_EXPERTS, 1),
        utri=jnp.asarray((tok[:, None] < tok[None, :]).astype(np.float32)).astype(BF16),
        weg=w_exp_gate.astype(BF16), weu=w_exp_up.astype(BF16),
        wed=w_exp_down.reshape(DEPTH, N_EXPERT_GROUPS, EXPERTS_PER_GROUP * D_EXPERT, D_MODEL).astype(BF16),
        fg=final_g.reshape(1, D_MODEL),
    )


def kernel(x_prompt, x_sample, state_C, state_n, state_m, c, c_ctx, w_ada, b_ada, norm1_g, norm2_g, w_in, w_fnet, w_dw, b_dw, conv_ln_g, conv_ln_b, w_pw, b_gate, g_mh, w_out, w_router, b_router, w_exp_gate, w_exp_up, w_exp_down, final_g):
    bp, tp, _ = x_prompt.shape
    bs, ts, _ = x_sample.shape

    cv = jnp.zeros((MOD_ROWS, D_MODEL), F32).at[:bs].set(c).at[bs].set(c_ctx)
    mod_all = _ada_call(cv, w_ada, b_ada)
    mod_rows = mod_all.reshape(DEPTH * MOD_ROWS, 1, N_MOD * D_MODEL)

    pw = _prepare((tp, ts), norm1_g, norm2_g, w_in, w_fnet, w_dw, b_dw, conv_ln_g, conv_ln_b, w_pw, b_gate,
                  g_mh, w_out, w_router, b_router, w_exp_gate, w_exp_up, w_exp_down, final_g)
    pos = _grid_pos_tables(ts, x_sample.dtype)

    c0 = state_C.reshape(bs, DEPTH, N_UNITS, HEAD_DIM_M, HEAD_DIM_M)
    n0 = state_n.reshape(bs, DEPTH, N_UNITS, HEAD_DIM_M)
    m0 = jnp.pad(state_m, ((0, 0), (0, 0), (0, 0), (0, GATE_ROWS - N_HEADS_M)))
    m0 = jnp.broadcast_to(m0[..., None], (bs, DEPTH, N_DIRS, GATE_ROWS, CHUNK))

    xp, xs = x_prompt, x_sample
    cs, ns, ms = [], [], []
    for l in range(DEPTH):
        is_last = l == DEPTH - 1
        base = l * MOD_ROWS
        xp, c_l, n_l, m_l = _mixer_call(xp, None, mod_rows, base + bs, False, l, pw, None, True)
        cs.append(c_l)
        ns.append(n_l)
        ms.append(m_l[:, :, :N_HEADS_M, 0])
        xp = _moe_call(xp.reshape(bp * tp, D_MODEL), mod_rows, base + bs, False, l, pw, is_last)
        xp = xp.reshape(bp, tp, D_MODEL)

        (xs,) = _mixer_call(xs, pos if l == 0 else None, mod_rows, base, True, l, pw, (c0, n0, m0), False)
        xs = _moe_call(xs.reshape(bs * ts, D_MODEL), mod_rows, base, True, l, pw, is_last)
        xs = xs.reshape(bs, ts, D_MODEL)

    new_c = jnp.stack(cs, axis=1).reshape(bp, DEPTH, N_DIRS, N_HEADS_M, HEAD_DIM_M, HEAD_DIM_M)
    new_n = jnp.stack(ns, axis=1).reshape(bp, DEPTH, N_DIRS, N_HEADS_M, HEAD_DIM_M)
    new_m = jnp.stack(ms, axis=1).reshape(bp, DEPTH, N_DIRS, N_HEADS_M)
    return (xp, xs, new_c, new_n, new_m)
```

```python
import functools
import math

import numpy as np
import jax
import jax.numpy as jnp
from jax import lax
from jax.experimental import pallas as pl
from jax.experimental.pallas import tpu as pltpu

D_MODEL = 1024
DEPTH = 4
GRID_W = 64
EPS = 1e-6
D_FOURIER = 256
N_FOURIER_GROUPS = 4
D_FOURIER_GROUP = D_FOURIER // N_FOURIER_GROUPS
D_CONV = 256
CONV_WIDTH = 31
CONV_PAD = CONV_WIDTH // 2
D_MLSTM = 512
N_HEADS_M = 4
HEAD_DIM_M = D_MLSTM // N_HEADS_M
N_DIRS = 2
CHUNK = 128
N_GATES = 2 * N_DIRS * N_HEADS_M
N_UNITS = N_DIRS * N_HEADS_M
D_MIX = D_FOURIER + D_CONV + D_MLSTM
D_IN_PROJ = D_FOURIER + 2 * D_CONV + 4 * D_MLSTM + N_GATES
N_EXPERTS = 16
N_EXPERT_GROUPS = 4
EXPERTS_PER_GROUP = N_EXPERTS // N_EXPERT_GROUPS
D_EXPERT = 256
N_MOD = 6

O_B = D_FOURIER
O_Q = O_B + 2 * D_CONV
O_K = O_Q + D_MLSTM
O_V = O_K + D_MLSTM
O_O = O_V + D_MLSTM
O_G = O_O + D_MLSTM

ROWS = 1024
CONV_ROW_TILE = 64
PAD_LO = 16
MOD_ROWS = 16
SUBLANES = 8
SHIFT_ROWS = 24
AUG = 16
HEAD_AUG = HEAD_DIM_M + AUG
GATE_ROWS = 8
MOE_BLK = 128
MOE_CAP = ROWS + N_EXPERT_GROUPS * MOE_BLK
MOE_EXT = 128
VMEM_LIMIT = 60 * 1024 * 1024

F32 = jnp.float32
BF16 = jnp.bfloat16


def _dot(a, b):
    return jnp.dot(a, b, preferred_element_type=F32)


def _dot_nt(a, b):
    return lax.dot_general(a, b, (((1,), (1,)), ((), ())), preferred_element_type=F32)


def _sigmoid(x):
    return 1.0 / (1.0 + jnp.exp(-x))


def _log_sigmoid(x):
    return jnp.minimum(x, 0.0) - jnp.log(1.0 + jnp.exp(-jnp.abs(x)))


def _hi_lo(a_f32):
    hi = a_f32.astype(BF16)
    return hi, (a_f32 - hi.astype(F32)).astype(BF16)


def _split_dot(a_f32, b_bf16):
    hi, lo = _hi_lo(a_f32)
    return _dot(hi, b_bf16) + _dot(lo, b_bf16)


def _split_dot_left(b_bf16, a_f32):
    hi, lo = _hi_lo(a_f32)
    return _dot(b_bf16, hi) + _dot(b_bf16, lo)


def _rmsnorm_rows(x, g):
    ms = jnp.mean(x * x, axis=-1, keepdims=True)
    return x * lax.rsqrt(ms + EPS) * g


def _ada_body(c_ref, w_ref, b_ref, o_ref):
    cv = c_ref[...]
    s_hi, s_lo = _hi_lo(cv * _sigmoid(cv))
    w_hi, w_lo = _hi_lo(w_ref[0])
    both = _dot(jnp.concatenate([s_hi, s_lo], axis=0), w_hi)
    o_ref[0] = both[0:MOD_ROWS] + both[MOD_ROWS:2 * MOD_ROWS] + _dot(s_hi, w_lo) + b_ref[0]


def _ada_call(cv, w_ada, b_ada):
    n_col = N_MOD * D_MODEL
    tn = D_MODEL
    return pl.pallas_call(
        _ada_body,
        grid=(DEPTH, n_col // tn),
        in_specs=[
            pl.BlockSpec((MOD_ROWS, D_MODEL), lambda l, j: (0, 0)),
            pl.BlockSpec((1, D_MODEL, tn), lambda l, j: (l, 0, j)),
            pl.BlockSpec((1, 1, tn), lambda l, j: (l, 0, j)),
        ],
        out_specs=pl.BlockSpec((1, MOD_ROWS, tn), lambda l, j: (l, 0, j)),
        out_shape=jax.ShapeDtypeStruct((DEPTH, MOD_ROWS, n_col), F32),
        compiler_params=pltpu.CompilerParams(dimension_semantics=("arbitrary", "arbitrary")),
        name="ada",
    )(cv, w_ada, b_ada.reshape(DEPTH, 1, n_col))


def _scan_max_lanes(x, reverse):
    lane = lax.broadcasted_iota(jnp.int32, x.shape, 1)
    neg_inf = jnp.float32(-jnp.inf)
    sh = 1
    while sh < CHUNK:
        if reverse:
            y = pltpu.roll(x, CHUNK - sh, axis=1)
            x = jnp.maximum(x, jnp.where(lane < CHUNK - sh, y, neg_inf))
        else:
            y = pltpu.roll(x, sh, axis=1)
            x = jnp.maximum(x, jnp.where(lane >= sh, y, neg_inf))
        sh *= 2
    return x


def _mixer_body(bb, t, has_init, emit_state, has_prev, add_pos, *refs):
    rows = bb * t
    nc = t // CHUNK
    n_blk = rows // CHUNK
    refs = list(refs)
    x_ref = refs.pop(0)
    if add_pos:
        posr_ref, posc_ref = refs[:2]
        refs = refs[2:]
    (mod_ref, g1_ref, win_ref, bg_ref, blk_ref, dc_ref, ds_ref, wfn_ref,
     wdw_ref, bdw_ref, lng_ref, lnb_ref, wpw_ref, gmh_ref) = refs[:14]
    refs = refs[14:]
    if has_init:
        c0_ref, n0_ref, m0_ref = refs[:3]
        refs = refs[3:]
    wout_ref = refs.pop(0)
    if has_prev:
        refs = refs[3:]
    x1_ref = refs.pop(0)
    if emit_state:
        co_ref, no_ref, mo_ref = refs[:3]
        refs = refs[3:]
    (xa_s, pad_s, shf_s, cact_s, qt_s, k_s, vta_s, so_s, rows_s, acol_s, mprev_s, ctab_s,
     mix_s, cta_s, mst_s) = refs

    if add_pos:
        half_d = D_MODEL // 2
        for g in range(t // GRID_W):
            rs = slice(g * GRID_W, (g + 1) * GRID_W)
            emb = jnp.concatenate([jnp.broadcast_to(posr_ref[g:g + 1, :], (GRID_W, half_d)), posc_ref[...]], axis=1)
            x1_ref[0, rs, :] = x_ref[0, rs, :] + emb

    def load_x():
        src = x1_ref if add_pos else x_ref
        return src[...].reshape(rows, D_MODEL)

    def w_in(lo, hi):
        return win_ref[0, lo:hi, :]

    x = load_x()
    mod = mod_ref[0]
    sh1 = mod[:, 0:D_MODEL]
    sc1 = mod[:, D_MODEL:2 * D_MODEL]
    h = _rmsnorm_rows(x, g1_ref[0]) * (1.0 + sc1) + sh1
    hb = h.astype(BF16)

    pb = _dot_nt(hb, w_in(O_B, O_Q))
    u = pb[:, :D_CONV] * _sigmoid(pb[:, D_CONV:])
    zpad = jnp.zeros((PAD_LO, D_CONV), F32)
    for i in range(bb):
        pad_s[i, 0:PAD_LO, :] = zpad
        pad_s[i, PAD_LO:PAD_LO + t, :] = u[i * t:(i + 1) * t]
        pad_s[i, PAD_LO + t:2 * PAD_LO + t, :] = zpad

    for i in range(bb):
        for s8 in range(1, SUBLANES):
            shf_s[s8 - 1, i] = pad_s[i, s8:s8 + t + SHIFT_ROWS, :]

    def conv_tile(seq, r0):
        acc = jnp.broadcast_to(bdw_ref[0], (CONV_ROW_TILE, D_CONV))
        for j in range(CONV_WIDTH):
            q = j + PAD_LO - CONV_PAD
            win_rows = pl.ds(pl.multiple_of(r0 + (q // SUBLANES) * SUBLANES, SUBLANES), CONV_ROW_TILE)
            win = pad_s[seq, win_rows, :] if q % SUBLANES == 0 else shf_s[q % SUBLANES - 1, seq, win_rows, :]
            acc = acc + win * wdw_ref[0, j:j + 1, :]
        mu = jnp.mean(acc, axis=-1, keepdims=True)
        cen = acc - mu
        var = jnp.mean(cen * cen, axis=-1, keepdims=True)
        uf = cen * lax.rsqrt(var + EPS) * lng_ref[0] + lnb_ref[0]
        return (uf * _sigmoid(uf)).astype(BF16)

    xa_s[...] = _dot_nt(hb, w_in(0, O_B)).astype(BF16)
    k_s[...] = (_dot_nt(hb, w_in(O_K, O_V)) * (HEAD_DIM_M ** -0.5)).astype(BF16)
    og = _dot_nt(hb, w_in(O_O, D_IN_PROJ))
    so_s[...] = _sigmoid(og[:, 0:D_MLSTM]).astype(BF16)
    qf = _dot_nt(hb, w_in(O_Q, O_K))
    for b in range(n_blk):
        qt_s[b] = qf[b * CHUNK:(b + 1) * CHUNK, :].T.astype(BF16)
    vf = _dot_nt(hb, w_in(O_V, O_O))
    ones_rows = jnp.where(lax.broadcasted_iota(jnp.int32, (AUG, CHUNK), 0) == 0, 1.0, 0.0).astype(BF16)
    for b in range(n_blk):
        vt = vf[b * CHUNK:(b + 1) * CHUNK, :].T.astype(BF16)
        for hh in range(N_HEADS_M):
            vta_s[b, hh, 0:HEAD_DIM_M, :] = vt[hh * HEAD_DIM_M:(hh + 1) * HEAD_DIM_M, :]
            vta_s[b, hh, HEAD_DIM_M:HEAD_AUG, :] = ones_rows

    r_i = lax.broadcasted_iota(jnp.int32, (CHUNK, CHUNK), 0)
    c_i = lax.broadcasted_iota(jnp.int32, (CHUNK, CHUNK), 1)
    lower = r_i >= c_i
    upper = r_i <= c_i
    tri_lo = lower.astype(BF16)
    tri_up = upper.astype(BF16)
    neg_inf = jnp.float32(-jnp.inf)
    n_gr = n_blk * GATE_ROWS

    gcol = og[:, D_MLSTM:D_MLSTM + N_GATES] + bg_ref[0]
    kind_c = (lax.broadcasted_iota(jnp.int32, gcol.shape, 1) // N_HEADS_M) % 2
    gcol = jnp.where(kind_c == 1, _log_sigmoid(gcol), gcol)
    grow = jnp.concatenate([gcol, jnp.zeros((rows, CHUNK - N_GATES), F32)], axis=1).T[0:N_GATES, :]
    for d in range(N_DIRS):
        g0 = d * 2 * N_HEADS_M
        ll = CHUNK - 1 if d == 0 else 0
        xg = jnp.concatenate([grow[g0:g0 + GATE_ROWS, b * CHUNK:(b + 1) * CHUNK] for b in range(n_blk)], axis=0)
        b_rows = pltpu.roll(_split_dot(xg, tri_up if d == 0 else tri_lo), n_gr - N_HEADS_M, axis=0)
        a_rows = xg - b_rows
        a_run = _scan_max_lanes(a_rows, d == 1)
        b_last = jnp.broadcast_to(b_rows[:, ll:ll + 1], (n_gr, CHUNK))
        a_max = jnp.broadcast_to(a_run[:, ll:ll + 1], (n_gr, CHUNK))
        rows_s[d, 0] = b_rows
        rows_s[d, 1] = a_run
        rows_s[d, 2] = b_last
        rows_s[d, 3] = a_max
        rows_s[d, 4] = jnp.exp(a_rows - a_max)
        for b in range(n_blk):
            gc = gcol[b * CHUNK:(b + 1) * CHUNK, :]
            bcol = _split_dot_left(tri_lo if d == 0 else tri_up, gc)
            acol_s[d, b] = gc[:, g0:g0 + N_HEADS_M] - bcol[:, g0 + N_HEADS_M:g0 + 2 * N_HEADS_M]

    for i in range(bb):
        xa_i = xa_s[i * t:(i + 1) * t, :]
        uu = _dot(xa_i, blk_ref[...])
        y = (_dot(dc_ref[...], uu[:, :D_FOURIER].astype(BF16))
             + _dot(ds_ref[...], uu[:, D_FOURIER:].astype(BF16)))
        mix_s[i * t:(i + 1) * t, 0:D_FOURIER] = _dot(y.astype(BF16), wfn_ref[0]).astype(BF16)

    def state_step(pair, carry):
        seq = (2 * pair) // nc
        c0 = (2 * pair) % nc

        @pl.when(c0 == 0)
        def _():
            if has_init:
                first_row = lax.broadcasted_iota(jnp.int32, (AUG, HEAD_DIM_M), 0) == 0
                for j in range(N_UNITS):
                    cta_s[j, 0:HEAD_DIM_M, :] = c0_ref[seq, 0, j].T
                    cta_s[j, HEAD_DIM_M:HEAD_AUG, :] = jnp.where(first_row, n0_ref[seq, 0, j:j + 1, :], 0.0)
                mst_s[...] = m0_ref[seq, 0]
            else:
                cta_s[...] = jnp.zeros(cta_s.shape, F32)
                mst_s[...] = jnp.zeros(mst_s.shape, F32)

        m_cur = [mst_s[d] for d in range(N_DIRS)]
        steps = []
        for sub in range(2):
            c = c0 + sub
            units = []
            for d in range(N_DIRS):
                blk = seq * nc + (c if d == 0 else nc - 1 - c)
                row0 = pl.multiple_of(blk * CHUNK, CHUNK)
                gr0 = pl.multiple_of(blk * GATE_ROWS, GATE_ROWS)
                b_last = rows_s[d, 2, pl.ds(gr0, GATE_ROWS), :]
                a_max = rows_s[d, 3, pl.ds(gr0, GATE_ROWS), :]
                w_rows = rows_s[d, 4, pl.ds(gr0, GATE_ROWS), :]
                m_prev = m_cur[d]
                mprev_s[d, blk] = m_prev
                m_new = b_last + jnp.maximum(m_prev, a_max)
                m_cur[d] = m_new
                decay = jnp.exp(b_last + m_prev - m_new)
                fac = jnp.exp(a_max + b_last - m_new)
                for hh in range(N_HEADS_M):
                    hs = slice(hh * HEAD_DIM_M, (hh + 1) * HEAD_DIM_M)
                    units.append(dict(d=d, hh=hh, j=d * N_HEADS_M + hh, blk=blk,
                                      kc=k_s[pl.ds(row0, CHUNK), hs], vta=vta_s[blk, hh], w=w_rows[hh:hh + 1, :],
                                      decay=decay[hh:hh + 1, :], fac=fac[hh:hh + 1, :]))
            steps.append(units)
        for d in range(N_DIRS):
            mst_s[d] = m_cur[d]
        for units in steps:
            for un in units:
                vw = (un["vta"].astype(F32) * un["w"]).astype(BF16)
                un["upd"] = _dot(vw, un["kc"])
        for sub in range(2):
            it = 2 * pair + sub
            cact_s[pl.ds(pl.multiple_of(it * CHUNK, CHUNK), CHUNK // 2), :] = conv_tile(seq, (c0 + sub) * CHUNK)
        cta = [cta_s[j] for j in range(N_UNITS)]
        for units in steps:
            for un in units:
                j = un["j"]
                ctab_s[un["blk"], un["hh"], un["d"] * HEAD_AUG:(un["d"] + 1) * HEAD_AUG, :] = cta[j].astype(BF16)
                cta[j] = un["decay"] * cta[j] + un["fac"] * un["upd"]
        for j in range(N_UNITS):
            cta_s[j] = cta[j]

        if emit_state:
            @pl.when(c0 == nc - 2)
            def _():
                for j in range(N_UNITS):
                    co_ref[seq, 0, j] = cta_s[j, 0:HEAD_DIM_M, :].T
                    no_ref[seq, 0, j:j + 1, :] = cta_s[j, HEAD_DIM_M:HEAD_DIM_M + 1, :]
                mo_ref[seq, 0] = mst_s[...]
        return carry

    lax.fori_loop(0, n_blk // 2, state_step, 0)

    def chunk_out(pair, carry):
        heads = []
        for blk in (2 * pair, 2 * pair + 1):
            row0 = pl.multiple_of(blk * CHUNK, CHUNK)
            gr0 = pl.multiple_of(blk * GATE_ROWS, GATE_ROWS)
            dirs = []
            for d in range(N_DIRS):
                m_prev = mprev_s[d, blk]
                mm = jnp.maximum(m_prev, rows_s[d, 1, pl.ds(gr0, GATE_ROWS), :])
                floor = jnp.exp(-(rows_s[d, 0, pl.ds(gr0, GATE_ROWS), :] + mm))
                dirs.append(dict(mm=mm, inter=jnp.exp(m_prev - mm), floor=floor,
                                 acol=acol_s[d, blk], mask=upper if d == 0 else lower))
            for hh in range(N_HEADS_M):
                hs = slice(hh * HEAD_DIM_M, (hh + 1) * HEAD_DIM_M)
                hd = dict(hh=hh, hs=hs, blk=blk, row0=row0, dirs=dirs,
                          kc=k_s[pl.ds(row0, CHUNK), hs], qt=qt_s[blk, hs, :], vta=vta_s[blk, hh])
                sp = _dot(jnp.concatenate([hd["kc"], ctab_s[blk, hh]], axis=0), hd["qt"])
                hd["st"] = sp[0:CHUNK]
                hd["p1"] = [sp[CHUNK + d * HEAD_AUG:CHUNK + (d + 1) * HEAD_AUG] for d in range(N_DIRS)]
                heads.append(hd)
        for hd in heads:
            hh = hd["hh"]
            sm = []
            for dd in hd["dirs"]:
                z = dd["acol"][:, hh:hh + 1] - dd["mm"][hh:hh + 1, :]
                sm.append((hd["st"] * jnp.exp(jnp.where(dd["mask"], z, neg_inf))).astype(BF16))
            hd["p2"] = _dot(hd["vta"], jnp.concatenate(sm, axis=1))
        half = CHUNK // 2
        for blk in (2 * pair, 2 * pair + 1):
            row0 = pl.multiple_of(blk * CHUNK, CHUNK)
            cact_s[pl.ds(row0 + half, half), :] = conv_tile(blk // nc, (blk % nc) * CHUNK + half)
        for b2 in range(2):
            parts = []
            for hd in heads[b2 * N_HEADS_M:(b2 + 1) * N_HEADS_M]:
                hh = hd["hh"]
                hsum = None
                for d, dd in enumerate(hd["dirs"]):
                    numa = dd["inter"][hh:hh + 1, :] * hd["p1"][d] + hd["p2"][:, d * CHUNK:(d + 1) * CHUNK]
                    den = numa[HEAD_DIM_M:HEAD_DIM_M + 1, :]
                    ht = numa[0:HEAD_DIM_M, :] / jnp.maximum(jnp.abs(den), dd["floor"][hh:hh + 1, :])
                    hsum = ht if hsum is None else hsum + ht
                r = lax.rsqrt(jnp.mean(hsum * hsum, axis=0, keepdims=True) + EPS)
                parts.append((hsum * r * gmh_ref[0, hd["hs"], :]).T)
            row0 = heads[b2 * N_HEADS_M]["row0"]
            oc = jnp.concatenate(parts, axis=1) * so_s[pl.ds(row0, CHUNK), :].astype(F32)
            mix_s[pl.ds(row0, CHUNK), D_FOURIER + D_CONV:D_MIX] = oc.astype(BF16)
        return carry

    lax.fori_loop(0, n_blk // 2, chunk_out, 0)
    mix_s[:, D_FOURIER:D_FOURIER + D_CONV] = _dot(cact_s[...], wpw_ref[0]).astype(BF16)

    ga1 = mod_ref[0][:, 2 * D_MODEL:3 * D_MODEL]
    res = _dot(mix_s[...], wout_ref[0])
    x1_ref[...] = (load_x() + ga1 * res).reshape(bb, t, D_MODEL)


def _whole(a):
    nd = a.ndim
    return pl.BlockSpec(a.shape, lambda g: (0,) * nd, pipeline_mode=pl.Buffered(1))


def _layer_block(a, l):
    nd = a.ndim
    return pl.BlockSpec((1,) + a.shape[1:], lambda g: (l,) + (0,) * (nd - 1), pipeline_mode=pl.Buffered(1))


def _mixer_call(x, pos, mod_rows, mod_base, per_batch_mod, l, pw, state0, prev_states):
    emit_state = prev_states is not None
    has_prev = bool(prev_states)
    nb, t, _ = x.shape
    bb = ROWS // t
    assert bb * t == ROWS and nb % bb == 0 and t % CHUNK == 0
    rows = ROWS
    n_blk = rows // CHUNK
    has_init = state0 is not None
    add_pos = pos is not None

    args = [x]
    in_specs = [pl.BlockSpec((bb, t, D_MODEL), lambda g: (g, 0, 0))]
    if add_pos:
        assert bb == 1
        args += list(pos)
        in_specs += [_whole(a) for a in pos]
    args.append(mod_rows)
    if per_batch_mod:
        assert bb == 1
        in_specs.append(pl.BlockSpec((1, 1, N_MOD * D_MODEL), lambda g: (mod_base + g, 0, 0)))
    else:
        in_specs.append(pl.BlockSpec((1, 1, N_MOD * D_MODEL), lambda g: (mod_base, 0, 0)))
    for name in ("g1", "win", "bg"):
        args.append(pw[name])
        in_specs.append(_layer_block(pw[name], l))
    for a in (pw["blk"], pw["dc"][t], pw["ds"][t]):
        args.append(a)
        in_specs.append(_whole(a))
    for name in ("wfn", "wdw", "bdw", "lng", "lnb", "wpw", "gmh"):
        args.append(pw[name])
        in_specs.append(_layer_block(pw[name], l))
    if has_init:
        c0, n0, m0 = state0
        args += [c0, n0, m0]
        in_specs += [
            pl.BlockSpec((bb, 1, N_UNITS, HEAD_DIM_M, HEAD_DIM_M), lambda g: (g, l, 0, 0, 0)),
            pl.BlockSpec((bb, 1, N_UNITS, HEAD_DIM_M), lambda g: (g, l, 0, 0)),
            pl.BlockSpec((bb, 1, N_DIRS, GATE_ROWS, CHUNK), lambda g: (g, l, 0, 0, 0)),
        ]
    args.append(pw["wout"])
    in_specs.append(_layer_block(pw["wout"], l))
    aliases = {}
    if has_prev:
        aliases = {len(args) + i: 1 + i for i in range(3)}
        args += list(prev_states)
        in_specs += [pl.BlockSpec(memory_space=pl.ANY)] * 3

    out_shape = [jax.ShapeDtypeStruct((nb, t, D_MODEL), F32)]
    out_specs = [pl.BlockSpec((bb, t, D_MODEL), lambda g: (g, 0, 0))]
    if emit_state:
        out_shape += [
            jax.ShapeDtypeStruct((nb, DEPTH, N_UNITS, HEAD_DIM_M, HEAD_DIM_M), F32),
            jax.ShapeDtypeStruct((nb, DEPTH, N_UNITS, HEAD_DIM_M), F32),
            jax.ShapeDtypeStruct((nb, DEPTH, N_DIRS, GATE_ROWS, CHUNK), F32),
        ]
        out_specs += [
            pl.BlockSpec((bb, 1, N_UNITS, HEAD_DIM_M, HEAD_DIM_M), lambda g: (g, l, 0, 0, 0)),
            pl.BlockSpec((bb, 1, N_UNITS, HEAD_DIM_M), lambda g: (g, l, 0, 0)),
            pl.BlockSpec((bb, 1, N_DIRS, GATE_ROWS, CHUNK), lambda g: (g, l, 0, 0, 0)),
        ]
    scratch = [
        pltpu.VMEM((rows, D_FOURIER), BF16),
        pltpu.VMEM((bb, t + 2 * PAD_LO, D_CONV), F32),
        pltpu.VMEM((SUBLANES - 1, bb, t + SHIFT_ROWS, D_CONV), F32),
        pltpu.VMEM((rows, D_CONV), BF16),
        pltpu.VMEM((n_blk, D_MLSTM, CHUNK), BF16),
        pltpu.VMEM((rows, D_MLSTM), BF16),
        pltpu.VMEM((n_blk, N_HEADS_M, HEAD_AUG, CHUNK), BF16),
        pltpu.VMEM((rows, D_MLSTM), BF16),
        pltpu.VMEM((N_DIRS, 5, n_blk * GATE_ROWS, CHUNK), F32),
        pltpu.VMEM((N_DIRS, n_blk, CHUNK, N_HEADS_M), F32),
        pltpu.VMEM((N_DIRS, n_blk, GATE_ROWS, CHUNK), F32),
        pltpu.VMEM((n_blk, N_HEADS_M, N_DIRS * HEAD_AUG, HEAD_DIM_M), BF16),
        pltpu.VMEM((rows, D_MIX), BF16),
        pltpu.VMEM((N_UNITS, HEAD_AUG, HEAD_DIM_M), F32),
        pltpu.VMEM((N_DIRS, GATE_ROWS, CHUNK), F32),
    ]
    body = functools.partial(_mixer_body, bb, t, has_init, emit_state, has_prev, add_pos)
    return pl.pallas_call(
        body,
        grid=(nb // bb,),
        in_specs=in_specs,
        out_specs=out_specs,
        out_shape=out_shape,
        scratch_shapes=scratch,
        input_output_aliases=aliases,
        compiler_params=pltpu.CompilerParams(dimension_semantics=("arbitrary",),
                                             vmem_limit_bytes=VMEM_LIMIT),
        name="mixer_t%d" % t,
    )(*args)


def _route_rows(logits_t, b_router_col):
    scores = _sigmoid(logits_t)
    sel = scores + b_router_col
    sel_r = [sel[e:e + 1, :] for e in range(N_EXPERTS)]
    sc_r = [scores[e:e + 1, :] for e in range(N_EXPERTS)]

    best = None
    best_v = None
    for g in range(N_EXPERT_GROUPS):
        a, b, c, d = sel_r[g * EXPERTS_PER_GROUP:(g + 1) * EXPERTS_PER_GROUP]
        hi1, lo1 = jnp.maximum(a, b), jnp.minimum(a, b)
        hi2, lo2 = jnp.maximum(c, d), jnp.minimum(c, d)
        gs = jnp.maximum(hi1, hi2) + jnp.maximum(jnp.minimum(hi1, hi2), jnp.maximum(lo1, lo2))
        if g == 0:
            best = jnp.zeros(gs.shape, jnp.int32)
            best_v = gs
        else:
            upd = gs > best_v
            best = jnp.where(upd, g, best)
            best_v = jnp.where(upd, gs, best_v)

    def pick(rows, j):
        out = rows[(N_EXPERT_GROUPS - 1) * EXPERTS_PER_GROUP + j]
        for g in range(N_EXPERT_GROUPS - 2, -1, -1):
            out = jnp.where(best == g, rows[g * EXPERTS_PER_GROUP + j], out)
        return out

    s = [pick(sel_r, j) for j in range(EXPERTS_PER_GROUP)]
    sc = [pick(sc_r, j) for j in range(EXPERTS_PER_GROUP)]
    rank = [jnp.zeros(best.shape, jnp.int32) for _ in range(EXPERTS_PER_GROUP)]
    for a in range(EXPERTS_PER_GROUP):
        for b in range(a + 1, EXPERTS_PER_GROUP):
            b_first = s[b] > s[a]
            rank[a] = rank[a] + b_first.astype(jnp.int32)
            rank[b] = rank[b] + (1 - b_first.astype(jnp.int32))
    w = [jnp.where(rank[j] < 2, sc[j], 0.0) for j in range(EXPERTS_PER_GROUP)]
    tot = w[0] + w[1] + w[2] + w[3]
    return best, [wj / tot for wj in w]


def _moe_body(is_last, x_ref, mod_ref, g2_ref, wr2_ref, brc_ref, utri_ref, wg_ref, wu_ref, wd_ref, fg_ref,
              o_ref, he_s, dest_s, destl_s, pt_s, ys_s, sm_s):
    gi = pl.program_id(1)
    tm = ROWS

    @pl.when(gi == 0)
    def _():
        mod = mod_ref[0]
        sh2 = mod[:, 3 * D_MODEL:4 * D_MODEL]
        sc2 = mod[:, 4 * D_MODEL:5 * D_MODEL]
        h = _rmsnorm_rows(x_ref[...], g2_ref[0]) * (1.0 + sc2) + sh2
        h_hi, h_lo = _hi_lo(h)
        he_s[:, 0:D_MODEL] = h_hi
        lg = _dot(jnp.concatenate([h_hi, h_lo], axis=1), wr2_ref[...])
        lg = lg[:, 0:N_EXPERTS] + lg[:, N_EXPERTS:2 * N_EXPERTS]
        logits_t = jnp.concatenate([lg, jnp.zeros((tm, MOE_EXT - N_EXPERTS), F32)], axis=1).T[0:N_EXPERTS, :]
        best, cw = _route_rows(logits_t, brc_ref[...])

        row_i = lax.broadcasted_iota(jnp.int32, (16, tm), 0)
        onehot = (row_i == best).astype(F32)
        before = _dot(onehot.astype(BF16), utri_ref[...])
        dest = jnp.zeros((1, tm), F32)
        off_blk = jnp.int32(0)
        for g in range(N_EXPERT_GROUPS):
            cnt = jnp.sum(onehot[g:g + 1, :]).astype(jnp.int32)
            n_blk = lax.shift_right_logical(cnt + (MOE_BLK - 1), int(math.log2(MOE_BLK)))
            sm_s[g] = off_blk
            sm_s[N_EXPERT_GROUPS + g] = n_blk
            base = (off_blk * MOE_BLK).astype(F32)
            dest = dest + onehot[g:g + 1, :] * (before[g:g + 1, :] + base)
            off_blk = off_blk + n_blk

        cw_hi = [wj.astype(BF16).astype(F32) for wj in cw]
        cw_lo = [wj - hj for wj, hj in zip(cw, cw_hi)]
        stack = jnp.concatenate([dest] + cw_hi + cw_lo + [jnp.zeros((MOE_EXT - 9, tm), F32)], axis=0)
        he_s[:, D_MODEL:D_MODEL + MOE_EXT] = stack.T.astype(BF16)
        dest_s[...] = jnp.broadcast_to(dest, (SUBLANES, tm))
        destl_s[...] = jnp.broadcast_to(dest, (MOE_BLK, tm)).T
        pt_s[:, ROWS:] = jnp.zeros((tm, MOE_CAP - ROWS), BF16)
        ys_s[ROWS:, :] = jnp.zeros((MOE_CAP - ROWS, D_MODEL), BF16)

    first_blk = sm_s[gi]
    n_blk = sm_s[N_EXPERT_GROUPS + gi]

    def expert_rows(r0, m):
        rows_f = (lax.broadcasted_iota(jnp.int32, (m, 1), 0) + r0).astype(F32)
        p_blk = jnp.where(dest_s[0:1, :] == rows_f, 1.0, 0.0).astype(BF16)
        for k in range(m // MOE_BLK):
            cols_f = (lax.broadcasted_iota(jnp.int32, (1, MOE_BLK), 1) + (r0 + k * MOE_BLK)).astype(F32)
            pt_s[:, pl.ds(pl.multiple_of(r0 + k * MOE_BLK, MOE_BLK), MOE_BLK)] = jnp.where(
                destl_s[...] == cols_f, 1.0, 0.0).astype(BF16)
        xe = _dot(p_blk, he_s[...])
        xs = xe[:, 0:D_MODEL].astype(BF16)
        acts = []
        for j in range(EXPERTS_PER_GROUP):
            gate = _dot(xs, wg_ref[0, j])
            up = _dot(xs, wu_ref[0, j])
            cwj = (xe[:, D_MODEL + 1 + j:D_MODEL + 2 + j]
                   + xe[:, D_MODEL + 1 + EXPERTS_PER_GROUP + j:D_MODEL + 2 + EXPERTS_PER_GROUP + j])
            acts.append((gate * _sigmoid(gate) * up * cwj).astype(BF16))
        ys_s[pl.ds(r0, m), :] = _dot(jnp.concatenate(acts, axis=1), wd_ref[0, 0]).astype(BF16)

    odd = lax.rem(n_blk, 2) == 1
    lead = jnp.where(odd, jnp.where(n_blk >= 3, 3, 1), 0)

    def block_pair(i, carry):
        expert_rows(pl.multiple_of((first_blk + lead + 2 * i) * MOE_BLK, MOE_BLK), 2 * MOE_BLK)
        return carry

    @pl.when(lead == 3)
    def _():
        expert_rows(pl.multiple_of(first_blk * MOE_BLK, MOE_BLK), 3 * MOE_BLK)

    @pl.when(lead == 1)
    def _():
        expert_rows(pl.multiple_of(first_blk * MOE_BLK, MOE_BLK), MOE_BLK)

    lax.fori_loop(0, lax.shift_right_logical(n_blk - lead, 1), block_pair, 0)

    @pl.when(gi == N_EXPERT_GROUPS - 1)
    def _():
        ga2 = mod_ref[0][:, 5 * D_MODEL:6 * D_MODEL]
        xo = x_ref[...] + ga2 * _dot(pt_s[...], ys_s[...])
        if is_last:
            xo = _rmsnorm_rows(xo, fg_ref[...])
        o_ref[...] = xo


def _moe_call(x2d, mod_rows, mod_base, per_tile_mod, l, pw, is_last):
    n_tok = x2d.shape[0]
    tm = ROWS
    assert n_tok % tm == 0
    if per_tile_mod:
        mod_spec = pl.BlockSpec((1, 1, N_MOD * D_MODEL), lambda i, g: (mod_base + i, 0, 0))
    else:
        mod_spec = pl.BlockSpec((1, 1, N_MOD * D_MODEL), lambda i, g: (mod_base, 0, 0))

    def const(shape):
        return pl.BlockSpec(shape, lambda i, g: (0,) * len(shape), pipeline_mode=pl.Buffered(1))

    def grp(shape):
        return pl.BlockSpec((1, EXPERTS_PER_GROUP) + shape, lambda i, g: (l, g, 0, 0))

    return pl.pallas_call(
        functools.partial(_moe_body, is_last),
        grid=(n_tok // tm, N_EXPERT_GROUPS),
        in_specs=[
            pl.BlockSpec((tm, D_MODEL), lambda i, g: (i, 0)),
            mod_spec,
            pl.BlockSpec((1, 1, D_MODEL), lambda i, g: (l, 0, 0), pipeline_mode=pl.Buffered(1)),
            const((2 * D_MODEL, 2 * N_EXPERTS)),
            const((N_EXPERTS, 1)),
            const((tm, tm)),
            grp((D_MODEL, D_EXPERT)),
            grp((D_MODEL, D_EXPERT)),
            pl.BlockSpec((1, 1, EXPERTS_PER_GROUP * D_EXPERT, D_MODEL), lambda i, g: (l, g, 0, 0)),
            const((1, D_MODEL)),
        ],
        out_specs=pl.BlockSpec((tm, D_MODEL), lambda i, g: (i, 0)),
        out_shape=jax.ShapeDtypeStruct((n_tok, D_MODEL), F32),
        scratch_shapes=[
            pltpu.VMEM((tm, D_MODEL + MOE_EXT), BF16),
            pltpu.VMEM((SUBLANES, tm), F32),
            pltpu.VMEM((tm, MOE_BLK), F32),
            pltpu.VMEM((tm, MOE_CAP), BF16),
            pltpu.VMEM((MOE_CAP, D_MODEL), BF16),
            pltpu.SMEM((2 * N_EXPERT_GROUPS,), jnp.int32),
        ],
        compiler_params=pltpu.CompilerParams(dimension_semantics=("arbitrary", "arbitrary"),
                                             vmem_limit_bytes=VMEM_LIMIT),
        name="moe",
    )(x2d, mod_rows, pw["g2"], pw["wr2"], pw["brc"], pw["utri"], pw["weg"], pw["weu"], pw["wed"], pw["fg"])


def _dft_tables(t):
    idx = np.arange(t, dtype=np.int64)
    ang = 2.0 * np.pi * ((idx[:, None] * idx[None, :]) % t).astype(np.float64) / t
    scale = 1.0 / math.sqrt(t * D_FOURIER_GROUP)
    return (np.cos(ang) * scale).astype(np.float32), (-np.sin(ang) * scale).astype(np.float32)


def _group_tables():
    idx = np.arange(D_FOURIER_GROUP, dtype=np.int64)
    ang = 2.0 * np.pi * ((idx[:, None] * idx[None, :]) % D_FOURIER_GROUP).astype(np.float64) / D_FOURIER_GROUP
    eye = np.eye(N_FOURIER_GROUPS)
    return np.concatenate([np.kron(eye, np.cos(ang)), np.kron(eye, np.sin(ang))], axis=1).astype(np.float32)


def _grid_pos_tables(n_tokens, dtype):
    quarter = D_MODEL // 4
    omega = 1.0 / (10000.0 ** (jnp.arange(quarter, dtype=jnp.float32) / quarter))

    def enc(p):
        a = p[:, None] * omega[None, :]
        return jnp.concatenate([jnp.sin(a), jnp.cos(a)], axis=-1).astype(dtype)

    return (enc(jnp.arange(n_tokens // GRID_W, dtype=jnp.float32)), enc(jnp.arange(GRID_W, dtype=jnp.float32)))


def _router_hi_lo(w_router):
    w_hi = w_router.astype(BF16)
    w_lo = (w_router - w_hi.astype(F32)).astype(BF16)
    return jnp.concatenate([jnp.concatenate([w_hi, w_lo], axis=1),
                            jnp.concatenate([w_hi, jnp.zeros_like(w_lo)], axis=1)], axis=0)


def _prepare(seq_lens, norm1_g, norm2_g, w_in, w_fnet, w_dw, b_dw, conv_ln_g, conv_ln_b, w_pw, b_gate, g_mh,
             w_out, w_router, b_router, w_exp_gate, w_exp_up, w_exp_down, final_g):
    dft = {t: _dft_tables(t) for t in sorted(set(seq_lens))}
    w_in_b = w_in.astype(BF16)
    tok = np.arange(ROWS)
    return dict(
        g1=norm1_g.reshape(DEPTH, 1, D_MODEL), g2=norm2_g.reshape(DEPTH, 1, D_MODEL),
        win=jnp.swapaxes(w_in_b, 1, 2), bg=b_gate.reshape(DEPTH, 1, N_GATES),
        blk=jnp.asarray(_group_tables()).astype(BF16),
        dc={t: jnp.asarray(v[0]).astype(BF16) for t, v in dft.items()},
        ds={t: jnp.asarray(v[1]).astype(BF16) for t, v in dft.items()},
        wfn=w_fnet.astype(BF16),
        wdw=jnp.concatenate([w_dw, jnp.zeros((DEPTH, 1, D_CONV), F32)], axis=1),
        bdw=b_dw.reshape(DEPTH, 1, D_CONV), lng=conv_ln_g.reshape(DEPTH, 1, D_CONV),
        lnb=conv_ln_b.reshape(DEPTH, 1, D_CONV), wpw=w_pw.astype(BF16),
        gmh=jnp.broadcast_to(g_mh.reshape(DEPTH, D_MLSTM, 1), (DEPTH, D_MLSTM, CHUNK)),
        wout=w_out.astype(BF16),
        wr2=_router_hi_lo(w_router), brc=b_router.reshape(N_EXPERTS, 1),
        utri=jnp.asarray((tok[:, None] < tok[None, :]).astype(np.float32)).astype(BF16),
        weg=w_exp_gate.astype(BF16), weu=w_exp_up.astype(BF16),
        wed=w_exp_down.reshape(DEPTH, N_EXPERT_GROUPS, EXPERTS_PER_GROUP * D_EXPERT, D_MODEL).astype(BF16),
        fg=final_g.reshape(1, D_MODEL),
    )


def kernel(x_prompt, x_sample, state_C, state_n, state_m, c, c_ctx, w_ada, b_ada, norm1_g, norm2_g, w_in, w_fnet, w_dw, b_dw, conv_ln_g, conv_ln_b, w_pw, b_gate, g_mh, w_out, w_router, b_router, w_exp_gate, w_exp_up, w_exp_down, final_g):
    bp, tp, _ = x_prompt.shape
    bs, ts, _ = x_sample.shape

    cv = jnp.zeros((MOD_ROWS, D_MODEL), F32).at[:bs].set(c).at[bs].set(c_ctx)
    mod_all = _ada_call(cv, w_ada, b_ada)
    mod_rows = mod_all.reshape(DEPTH * MOD_ROWS, 1, N_MOD * D_MODEL)

    pw = _prepare((tp, ts), norm1_g, norm2_g, w_in, w_fnet, w_dw, b_dw, conv_ln_g, conv_ln_b, w_pw, b_gate,
                  g_mh, w_out, w_router, b_router, w_exp_gate, w_exp_up, w_exp_down, final_g)
    pos = _grid_pos_tables(ts, x_sample.dtype)

    c0 = state_C.reshape(bs, DEPTH, N_UNITS, HEAD_DIM_M, HEAD_DIM_M)
    n0 = state_n.reshape(bs, DEPTH, N_UNITS, HEAD_DIM_M)
    m0 = jnp.pad(state_m, ((0, 0), (0, 0), (0, 0), (0, GATE_ROWS - N_HEADS_M)))
    m0 = jnp.broadcast_to(m0[..., None], (bs, DEPTH, N_DIRS, GATE_ROWS, CHUNK))

    xp, xs = x_prompt, x_sample
    states = ()
    for l in range(DEPTH):
        is_last = l == DEPTH - 1
        base = l * MOD_ROWS
        xp, *states = _mixer_call(xp, None, mod_rows, base + bs, False, l, pw, None, tuple(states))
        xp = _moe_call(xp.reshape(bp * tp, D_MODEL), mod_rows, base + bs, False, l, pw, is_last)
        xp = xp.reshape(bp, tp, D_MODEL)

        (xs,) = _mixer_call(xs, pos if l == 0 else None, mod_rows, base, True, l, pw, (c0, n0, m0), None)
        xs = _moe_call(xs.reshape(bs * ts, D_MODEL), mod_rows, base, True, l, pw, is_last)
        xs = xs.reshape(bs, ts, D_MODEL)

    c_all, n_all, m_all = states
    new_c = c_all.reshape(bp, DEPTH, N_DIRS, N_HEADS_M, HEAD_DIM_M, HEAD_DIM_M)
    new_n = n_all.reshape(bp, DEPTH, N_DIRS, N_HEADS_M, HEAD_DIM_M)
    new_m = m_all[:, :, :, :N_HEADS_M, 0]
    return (xp, xs, new_c, new_n, new_m)
```

```python
import functools
import math

import numpy as np
import jax
import jax.numpy as jnp
from jax import lax
from jax.experimental import pallas as pl
from jax.experimental.pallas import tpu as pltpu

D_MODEL = 1024
DEPTH = 4
GRID_W = 64
EPS = 1e-6
D_FOURIER = 256
N_FOURIER_GROUPS = 4
D_FOURIER_GROUP = D_FOURIER // N_FOURIER_GROUPS
D_CONV = 256
CONV_WIDTH = 31
CONV_PAD = CONV_WIDTH // 2
D_MLSTM = 512
N_HEADS_M = 4
HEAD_DIM_M = D_MLSTM // N_HEADS_M
N_DIRS = 2
CHUNK = 128
N_GATES = 2 * N_DIRS * N_HEADS_M
N_UNITS = N_DIRS * N_HEADS_M
D_MIX = D_FOURIER + D_CONV + D_MLSTM
D_IN_PROJ = D_FOURIER + 2 * D_CONV + 4 * D_MLSTM + N_GATES
N_EXPERTS = 16
N_EXPERT_GROUPS = 4
EXPERTS_PER_GROUP = N_EXPERTS // N_EXPERT_GROUPS
D_EXPERT = 256
N_MOD = 6

O_B = D_FOURIER
O_Q = O_B + 2 * D_CONV
O_K = O_Q + D_MLSTM
O_V = O_K + D_MLSTM
O_O = O_V + D_MLSTM
O_G = O_O + D_MLSTM

ROWS = 1024
CONV_ROW_TILE = 64
PAD_LO = 16
MOD_ROWS = 16
SUBLANES = 8
SHIFT_ROWS = 24
AUG = 16
HEAD_AUG = HEAD_DIM_M + AUG
GATE_ROWS = 8
MOE_BLK = 128
MOE_CAP = ROWS + N_EXPERT_GROUPS * MOE_BLK
MOE_EXT = 128
VMEM_LIMIT = 60 * 1024 * 1024

F32 = jnp.float32
BF16 = jnp.bfloat16


def _dot(a, b):
    return jnp.dot(a, b, preferred_element_type=F32)


def _dot_nt(a, b):
    return lax.dot_general(a, b, (((1,), (1,)), ((), ())), preferred_element_type=F32)


def _mo(x, m):
    return x if isinstance(x, int) else pl.multiple_of(x, m)


def _sigmoid(x):
    return 1.0 / (1.0 + jnp.exp(-x))


def _log_sigmoid(x):
    return jnp.minimum(x, 0.0) - jnp.log(1.0 + jnp.exp(-jnp.abs(x)))


def _hi_lo(a_f32):
    hi = a_f32.astype(BF16)
    return hi, (a_f32 - hi.astype(F32)).astype(BF16)


def _split_dot(a_f32, b_bf16):
    hi, lo = _hi_lo(a_f32)
    return _dot(hi, b_bf16) + _dot(lo, b_bf16)


def _split_dot_left(b_bf16, a_f32):
    hi, lo = _hi_lo(a_f32)
    return _dot(b_bf16, hi) + _dot(b_bf16, lo)


def _rmsnorm_rows(x, g):
    ms = jnp.mean(x * x, axis=-1, keepdims=True)
    return x * lax.rsqrt(ms + EPS) * g


def _ada_body(c_ref, w_ref, b_ref, o_ref):
    cv = c_ref[...]
    s_hi, s_lo = _hi_lo(cv * _sigmoid(cv))
    w_hi, w_lo = _hi_lo(w_ref[0])
    both = _dot(jnp.concatenate([s_hi, s_lo], axis=0), w_hi)
    o_ref[0] = both[0:MOD_ROWS] + both[MOD_ROWS:2 * MOD_ROWS] + _dot(s_hi, w_lo) + b_ref[0]


def _ada_call(cv, w_ada, b_ada):
    n_col = N_MOD * D_MODEL
    tn = D_MODEL
    return pl.pallas_call(
        _ada_body,
        grid=(DEPTH, n_col // tn),
        in_specs=[
            pl.BlockSpec((MOD_ROWS, D_MODEL), lambda l, j: (0, 0)),
            pl.BlockSpec((1, D_MODEL, tn), lambda l, j: (l, 0, j)),
            pl.BlockSpec((1, 1, tn), lambda l, j: (l, 0, j)),
        ],
        out_specs=pl.BlockSpec((1, MOD_ROWS, tn), lambda l, j: (l, 0, j)),
        out_shape=jax.ShapeDtypeStruct((DEPTH, MOD_ROWS, n_col), F32),
        compiler_params=pltpu.CompilerParams(dimension_semantics=("arbitrary", "arbitrary")),
        name="ada",
    )(cv, w_ada, b_ada.reshape(DEPTH, 1, n_col))


def _scan_max_lanes(x, reverse):
    lane = lax.broadcasted_iota(jnp.int32, x.shape, 1)
    neg_inf = jnp.float32(-jnp.inf)
    sh = 1
    while sh < CHUNK:
        if reverse:
            y = pltpu.roll(x, CHUNK - sh, axis=1)
            x = jnp.maximum(x, jnp.where(lane < CHUNK - sh, y, neg_inf))
        else:
            y = pltpu.roll(x, sh, axis=1)
            x = jnp.maximum(x, jnp.where(lane >= sh, y, neg_inf))
        sh *= 2
    return x


def _mixer_body(bb, t, has_init, emit_state, has_prev, add_pos, *refs):
    rows = bb * t
    nc = t // CHUNK
    n_blk = rows // CHUNK
    refs = list(refs)
    x_ref = refs.pop(0)
    if add_pos:
        posr_ref, posc_ref = refs[:2]
        refs = refs[2:]
    (mod_ref, g1_ref, win_ref, bg_ref, blk_ref, dc_ref, ds_ref, wfn_ref,
     wdw_ref, bdw_ref, lng_ref, lnb_ref, wpw_ref, gmh_ref) = refs[:14]
    refs = refs[14:]
    if has_init:
        c0_ref, n0_ref, m0_ref = refs[:3]
        refs = refs[3:]
    wout_ref = refs.pop(0)
    if has_prev:
        refs = refs[3:]
    x1_ref = refs.pop(0)
    if emit_state:
        co_ref, no_ref, mo_ref = refs[:3]
        refs = refs[3:]
    (hb_s, xa_s, ucs_s, y_s, pad_s, shf_s, cact_s, qt_s, k_s, vta_s, so_s, rows_s, acol_s, mprev_s, ctab_s,
     mix_s, cta_s, mst_s) = refs

    if add_pos:
        half_d = D_MODEL // 2
        for g in range(t // GRID_W):
            rs = slice(g * GRID_W, (g + 1) * GRID_W)
            emb = jnp.concatenate([jnp.broadcast_to(posr_ref[g:g + 1, :], (GRID_W, half_d)), posc_ref[...]], axis=1)
            x1_ref[0, rs, :] = x_ref[0, rs, :] + emb

    def load_x():
        src = x1_ref if add_pos else x_ref
        return src[...].reshape(rows, D_MODEL)

    def w_in(lo, hi):
        return win_ref[0, lo:hi, :]

    x = load_x()
    mod = mod_ref[0]
    sh1 = mod[:, 0:D_MODEL]
    sc1 = mod[:, D_MODEL:2 * D_MODEL]
    h = _rmsnorm_rows(x, g1_ref[0]) * (1.0 + sc1) + sh1
    hb = h.astype(BF16)

    pb = _dot_nt(hb, w_in(O_B, O_Q))
    u = pb[:, :D_CONV] * _sigmoid(pb[:, D_CONV:])
    zpad = jnp.zeros((PAD_LO, D_CONV), F32)
    for i in range(bb):
        pad_s[i, 0:PAD_LO, :] = zpad
        pad_s[i, PAD_LO:PAD_LO + t, :] = u[i * t:(i + 1) * t]
        pad_s[i, PAD_LO + t:2 * PAD_LO + t, :] = zpad

    for i in range(bb):
        for s8 in range(1, SUBLANES):
            shf_s[s8 - 1, i] = pad_s[i, s8:s8 + t + SHIFT_ROWS, :]

    def conv_tile(seq, r0):
        acc = jnp.broadcast_to(bdw_ref[0], (CONV_ROW_TILE, D_CONV))
        for j in range(CONV_WIDTH):
            q = j + PAD_LO - CONV_PAD
            win_rows = pl.ds(_mo(r0 + (q // SUBLANES) * SUBLANES, SUBLANES), CONV_ROW_TILE)
            win = pad_s[seq, win_rows, :] if q % SUBLANES == 0 else shf_s[q % SUBLANES - 1, seq, win_rows, :]
            acc = acc + win * wdw_ref[0, j:j + 1, :]
        mu = jnp.mean(acc, axis=-1, keepdims=True)
        cen = acc - mu
        var = jnp.mean(cen * cen, axis=-1, keepdims=True)
        uf = cen * lax.rsqrt(var + EPS) * lng_ref[0] + lnb_ref[0]
        return (uf * _sigmoid(uf)).astype(BF16)

    hb_s[...] = hb
    k_s[...] = (_dot_nt(hb, w_in(O_K, O_V)) * (HEAD_DIM_M ** -0.5)).astype(BF16)
    og = _dot_nt(hb, w_in(O_O, D_IN_PROJ))
    so_s[...] = _sigmoid(og[:, 0:D_MLSTM]).astype(BF16)
    qf = _dot_nt(hb, w_in(O_Q, O_K))
    for b in range(n_blk):
        qt_s[b] = qf[b * CHUNK:(b + 1) * CHUNK, :].T.astype(BF16)
    vf = _dot_nt(hb, w_in(O_V, O_O))
    ones_rows = jnp.where(lax.broadcasted_iota(jnp.int32, (AUG, CHUNK), 0) == 0, 1.0, 0.0).astype(BF16)
    for b in range(n_blk):
        vt = vf[b * CHUNK:(b + 1) * CHUNK, :].T.astype(BF16)
        for hh in range(N_HEADS_M):
            vta_s[b, hh, 0:HEAD_DIM_M, :] = vt[hh * HEAD_DIM_M:(hh + 1) * HEAD_DIM_M, :]
            vta_s[b, hh, HEAD_DIM_M:HEAD_AUG, :] = ones_rows

    r_i = lax.broadcasted_iota(jnp.int32, (CHUNK, CHUNK), 0)
    c_i = lax.broadcasted_iota(jnp.int32, (CHUNK, CHUNK), 1)
    lower = r_i >= c_i
    upper = r_i <= c_i
    tri_lo = lower.astype(BF16)
    tri_up = upper.astype(BF16)
    neg_inf = jnp.float32(-jnp.inf)
    n_gr = n_blk * GATE_ROWS

    gcol = og[:, D_MLSTM:D_MLSTM + N_GATES] + bg_ref[0]
    kind_c = (lax.broadcasted_iota(jnp.int32, gcol.shape, 1) // N_HEADS_M) % 2
    gcol = jnp.where(kind_c == 1, _log_sigmoid(gcol), gcol)
    grow = jnp.concatenate([gcol, jnp.zeros((rows, CHUNK - N_GATES), F32)], axis=1).T[0:N_GATES, :]
    for d in range(N_DIRS):
        g0 = d * 2 * N_HEADS_M
        ll = CHUNK - 1 if d == 0 else 0
        xg = jnp.concatenate([grow[g0:g0 + GATE_ROWS, b * CHUNK:(b + 1) * CHUNK] for b in range(n_blk)], axis=0)
        b_rows = pltpu.roll(_split_dot(xg, tri_up if d == 0 else tri_lo), n_gr - N_HEADS_M, axis=0)
        a_rows = xg - b_rows
        a_run = _scan_max_lanes(a_rows, d == 1)
        b_last = jnp.broadcast_to(b_rows[:, ll:ll + 1], (n_gr, CHUNK))
        a_max = jnp.broadcast_to(a_run[:, ll:ll + 1], (n_gr, CHUNK))
        rows_s[d, 0] = b_rows
        rows_s[d, 1] = a_run
        rows_s[d, 2] = b_last
        rows_s[d, 3] = a_max
        rows_s[d, 4] = jnp.exp(a_rows - a_max)
        for b in range(n_blk):
            gc = gcol[b * CHUNK:(b + 1) * CHUNK, :]
            bcol = _split_dot_left(tri_lo if d == 0 else tri_up, gc)
            acol_s[d, b] = gc[:, g0:g0 + N_HEADS_M] - bcol[:, g0 + N_HEADS_M:g0 + 2 * N_HEADS_M]

    def fnet_proj():
        xa_s[...] = _dot_nt(hb_s[...], w_in(0, O_B)).astype(BF16)

    def fnet_channels():
        ucs_s[...] = _dot(xa_s[...], blk_ref[...]).astype(BF16)

    def fnet_positions(seqs, part):
        for i in seqs:
            rs = slice(i * t, (i + 1) * t)
            if part == 0:
                y_s[rs, :] = _dot(dc_ref[...], ucs_s[rs, 0:D_FOURIER])
            else:
                y_s[rs, :] = y_s[rs, :] + _dot(ds_ref[...], ucs_s[rs, D_FOURIER:2 * D_FOURIER])

    def fnet_linear():
        mix_s[:, 0:D_FOURIER] = _dot(y_s[...].astype(BF16), wfn_ref[0]).astype(BF16)

    all_seqs = tuple(range(bb))
    side_work = [fnet_proj, fnet_channels,
                 functools.partial(fnet_positions, all_seqs, 0), functools.partial(fnet_positions, all_seqs, 1),
                 fnet_linear]

    def state_step(pair, side):
        seq = (2 * pair) // nc
        c0 = (2 * pair) % nc

        if c0 == 0:
            if has_init:
                first_row = lax.broadcasted_iota(jnp.int32, (AUG, HEAD_DIM_M), 0) == 0
                for j in range(N_UNITS):
                    cta_s[j, 0:HEAD_DIM_M, :] = c0_ref[seq, 0, j].T
                    cta_s[j, HEAD_DIM_M:HEAD_AUG, :] = jnp.where(first_row, n0_ref[seq, 0, j:j + 1, :], 0.0)
                mst_s[...] = m0_ref[seq, 0]
            else:
                cta_s[...] = jnp.zeros(cta_s.shape, F32)
                mst_s[...] = jnp.zeros(mst_s.shape, F32)

        m_cur = [mst_s[d] for d in range(N_DIRS)]
        steps = []
        for sub in range(2):
            c = c0 + sub
            units = []
            for d in range(N_DIRS):
                blk = seq * nc + (c if d == 0 else nc - 1 - c)
                row0 = _mo(blk * CHUNK, CHUNK)
                gr0 = _mo(blk * GATE_ROWS, GATE_ROWS)
                b_last = rows_s[d, 2, pl.ds(gr0, GATE_ROWS), :]
                a_max = rows_s[d, 3, pl.ds(gr0, GATE_ROWS), :]
                w_rows = rows_s[d, 4, pl.ds(gr0, GATE_ROWS), :]
                m_prev = m_cur[d]
                mprev_s[d, blk] = m_prev
                m_new = b_last + jnp.maximum(m_prev, a_max)
                m_cur[d] = m_new
                decay = jnp.exp(b_last + m_prev - m_new)
                fac = jnp.exp(a_max + b_last - m_new)
                for hh in range(N_HEADS_M):
                    hs = slice(hh * HEAD_DIM_M, (hh + 1) * HEAD_DIM_M)
                    units.append(dict(d=d, hh=hh, j=d * N_HEADS_M + hh, blk=blk,
                                      kc=k_s[pl.ds(row0, CHUNK), hs], vta=vta_s[blk, hh], w=w_rows[hh:hh + 1, :],
                                      decay=decay[hh:hh + 1, :], fac=fac[hh:hh + 1, :]))
            steps.append(units)
        for d in range(N_DIRS):
            mst_s[d] = m_cur[d]
        for units in steps:
            for un in units:
                vw = (un["vta"].astype(F32) * un["w"]).astype(BF16)
                un["upd"] = _dot(vw, un["kc"])
        if side is not None:
            side()
        for sub in range(2):
            it = 2 * pair + sub
            cact_s[pl.ds(_mo(it * CHUNK, CHUNK), CHUNK // 2), :] = conv_tile(seq, (c0 + sub) * CHUNK)
        cta = [cta_s[j] for j in range(N_UNITS)]
        for units in steps:
            for un in units:
                j = un["j"]
                ctab_s[un["blk"], un["hh"], un["d"] * HEAD_AUG:(un["d"] + 1) * HEAD_AUG, :] = cta[j].astype(BF16)
                cta[j] = un["decay"] * cta[j] + un["fac"] * un["upd"]
        for j in range(N_UNITS):
            cta_s[j] = cta[j]

        if emit_state and c0 == nc - 2:
            for j in range(N_UNITS):
                co_ref[seq, 0, j] = cta_s[j, 0:HEAD_DIM_M, :].T
                no_ref[seq, 0, j:j + 1, :] = cta_s[j, HEAD_DIM_M:HEAD_DIM_M + 1, :]
            mo_ref[seq, 0] = mst_s[...]

    n_pairs = n_blk // 2
    for pair in range(n_pairs):
        state_step(pair, side_work.pop(0) if side_work else None)

    def chunk_out(pair, side):
        heads = []
        for blk in (2 * pair, 2 * pair + 1):
            row0 = _mo(blk * CHUNK, CHUNK)
            gr0 = _mo(blk * GATE_ROWS, GATE_ROWS)
            dirs = []
            for d in range(N_DIRS):
                m_prev = mprev_s[d, blk]
                mm = jnp.maximum(m_prev, rows_s[d, 1, pl.ds(gr0, GATE_ROWS), :])
                floor = jnp.exp(-(rows_s[d, 0, pl.ds(gr0, GATE_ROWS), :] + mm))
                dirs.append(dict(mm=mm, inter=jnp.exp(m_prev - mm), floor=floor,
                                 acol=acol_s[d, blk], mask=upper if d == 0 else lower))
            for hh in range(N_HEADS_M):
                hs = slice(hh * HEAD_DIM_M, (hh + 1) * HEAD_DIM_M)
                hd = dict(hh=hh, hs=hs, blk=blk, row0=row0, dirs=dirs,
                          kc=k_s[pl.ds(row0, CHUNK), hs], qt=qt_s[blk, hs, :], vta=vta_s[blk, hh])
                sp = _dot(jnp.concatenate([hd["kc"], ctab_s[blk, hh]], axis=0), hd["qt"])
                hd["st"] = sp[0:CHUNK]
                hd["p1"] = [sp[CHUNK + d * HEAD_AUG:CHUNK + (d + 1) * HEAD_AUG] for d in range(N_DIRS)]
                heads.append(hd)
        for hd in heads:
            hh = hd["hh"]
            sm = []
            for dd in hd["dirs"]:
                z = dd["acol"][:, hh:hh + 1] - dd["mm"][hh:hh + 1, :]
                sm.append((hd["st"] * jnp.exp(jnp.where(dd["mask"], z, neg_inf))).astype(BF16))
            hd["p2"] = _dot(hd["vta"], jnp.concatenate(sm, axis=1))
        if side is not None:
            side()
        half = CHUNK // 2
        for blk in (2 * pair, 2 * pair + 1):
            row0 = _mo(blk * CHUNK, CHUNK)
            cact_s[pl.ds(row0 + half, half), :] = conv_tile(blk // nc, (blk % nc) * CHUNK + half)
        for b2 in range(2):
            parts = []
            for hd in heads[b2 * N_HEADS_M:(b2 + 1) * N_HEADS_M]:
                hh = hd["hh"]
                hsum = None
                for d, dd in enumerate(hd["dirs"]):
                    numa = dd["inter"][hh:hh + 1, :] * hd["p1"][d] + hd["p2"][:, d * CHUNK:(d + 1) * CHUNK]
                    den = numa[HEAD_DIM_M:HEAD_DIM_M + 1, :]
                    ht = numa[0:HEAD_DIM_M, :] / jnp.maximum(jnp.abs(den), dd["floor"][hh:hh + 1, :])
                    hsum = ht if hsum is None else hsum + ht
                r = lax.rsqrt(jnp.mean(hsum * hsum, axis=0, keepdims=True) + EPS)
                parts.append((hsum * r * gmh_ref[0, hd["hs"], :]).T)
            row0 = heads[b2 * N_HEADS_M]["row0"]
            oc = jnp.concatenate(parts, axis=1) * so_s[pl.ds(row0, CHUNK), :].astype(F32)
            mix_s[pl.ds(row0, CHUNK), D_FOURIER + D_CONV:D_MIX] = oc.astype(BF16)

    for pair in range(n_pairs):
        chunk_out(pair, side_work.pop(0) if side_work else None)
    assert not side_work
    mix_s[:, D_FOURIER:D_FOURIER + D_CONV] = _dot(cact_s[...], wpw_ref[0]).astype(BF16)

    ga1 = mod_ref[0][:, 2 * D_MODEL:3 * D_MODEL]
    res = _dot(mix_s[...], wout_ref[0])
    x1_ref[...] = (load_x() + ga1 * res).reshape(bb, t, D_MODEL)


def _whole(a):
    nd = a.ndim
    return pl.BlockSpec(a.shape, lambda g: (0,) * nd, pipeline_mode=pl.Buffered(1))


def _layer_block(a, l):
    nd = a.ndim
    return pl.BlockSpec((1,) + a.shape[1:], lambda g: (l,) + (0,) * (nd - 1), pipeline_mode=pl.Buffered(1))


def _mixer_call(x, pos, mod_rows, mod_base, per_batch_mod, l, pw, state0, prev_states):
    emit_state = prev_states is not None
    has_prev = bool(prev_states)
    nb, t, _ = x.shape
    bb = ROWS // t
    assert bb * t == ROWS and nb % bb == 0 and t % CHUNK == 0
    rows = ROWS
    n_blk = rows // CHUNK
    has_init = state0 is not None
    add_pos = pos is not None

    args = [x]
    in_specs = [pl.BlockSpec((bb, t, D_MODEL), lambda g: (g, 0, 0))]
    if add_pos:
        assert bb == 1
        args += list(pos)
        in_specs += [_whole(a) for a in pos]
    args.append(mod_rows)
    if per_batch_mod:
        assert bb == 1
        in_specs.append(pl.BlockSpec((1, 1, N_MOD * D_MODEL), lambda g: (mod_base + g, 0, 0)))
    else:
        in_specs.append(pl.BlockSpec((1, 1, N_MOD * D_MODEL), lambda g: (mod_base, 0, 0)))
    for name in ("g1", "win", "bg"):
        args.append(pw[name])
        in_specs.append(_layer_block(pw[name], l))
    for a in (pw["blk"], pw["dc"][t], pw["ds"][t]):
        args.append(a)
        in_specs.append(_whole(a))
    for name in ("wfn", "wdw", "bdw", "lng", "lnb", "wpw", "gmh"):
        args.append(pw[name])
        in_specs.append(_layer_block(pw[name], l))
    if has_init:
        c0, n0, m0 = state0
        args += [c0, n0, m0]
        in_specs += [
            pl.BlockSpec((bb, 1, N_UNITS, HEAD_DIM_M, HEAD_DIM_M), lambda g: (g, l, 0, 0, 0)),
            pl.BlockSpec((bb, 1, N_UNITS, HEAD_DIM_M), lambda g: (g, l, 0, 0)),
            pl.BlockSpec((bb, 1, N_DIRS, GATE_ROWS, CHUNK), lambda g: (g, l, 0, 0, 0)),
        ]
    args.append(pw["wout"])
    in_specs.append(_layer_block(pw["wout"], l))
    aliases = {}
    if has_prev:
        aliases = {len(args) + i: 1 + i for i in range(3)}
        args += list(prev_states)
        in_specs += [pl.BlockSpec(memory_space=pl.ANY)] * 3

    out_shape = [jax.ShapeDtypeStruct((nb, t, D_MODEL), F32)]
    out_specs = [pl.BlockSpec((bb, t, D_MODEL), lambda g: (g, 0, 0))]
    if emit_state:
        out_shape += [
            jax.ShapeDtypeStruct((nb, DEPTH, N_UNITS, HEAD_DIM_M, HEAD_DIM_M), F32),
            jax.ShapeDtypeStruct((nb, DEPTH, N_UNITS, HEAD_DIM_M), F32),
            jax.ShapeDtypeStruct((nb, DEPTH, N_DIRS, GATE_ROWS, CHUNK), F32),
        ]
        out_specs += [
            pl.BlockSpec((bb, 1, N_UNITS, HEAD_DIM_M, HEAD_DIM_M), lambda g: (g, l, 0, 0, 0)),
            pl.BlockSpec((bb, 1, N_UNITS, HEAD_DIM_M), lambda g: (g, l, 0, 0)),
            pl.BlockSpec((bb, 1, N_DIRS, GATE_ROWS, CHUNK), lambda g: (g, l, 0, 0, 0)),
        ]
    scratch = [
        pltpu.VMEM((rows, D_MODEL), BF16),
        pltpu.VMEM((rows, D_FOURIER), BF16),
        pltpu.VMEM((rows, 2 * D_FOURIER), BF16),
        pltpu.VMEM((rows, D_FOURIER), F32),
        pltpu.VMEM((bb, t + 2 * PAD_LO, D_CONV), F32),
        pltpu.VMEM((SUBLANES - 1, bb, t + SHIFT_ROWS, D_CONV), F32),
        pltpu.VMEM((rows, D_CONV), BF16),
        pltpu.VMEM((n_blk, D_MLSTM, CHUNK), BF16),
        pltpu.VMEM((rows, D_MLSTM), BF16),
        pltpu.VMEM((n_blk, N_HEADS_M, HEAD_AUG, CHUNK), BF16),
        pltpu.VMEM((rows, D_MLSTM), BF16),
        pltpu.VMEM((N_DIRS, 5, n_blk * GATE_ROWS, CHUNK), F32),
        pltpu.VMEM((N_DIRS, n_blk, CHUNK, N_HEADS_M), F32),
        pltpu.VMEM((N_DIRS, n_blk, GATE_ROWS, CHUNK), F32),
        pltpu.VMEM((n_blk, N_HEADS_M, N_DIRS * HEAD_AUG, HEAD_DIM_M), BF16),
        pltpu.VMEM((rows, D_MIX), BF16),
        pltpu.VMEM((N_UNITS, HEAD_AUG, HEAD_DIM_M), F32),
        pltpu.VMEM((N_DIRS, GATE_ROWS, CHUNK), F32),
    ]
    body = functools.partial(_mixer_body, bb, t, has_init, emit_state, has_prev, add_pos)
    return pl.pallas_call(
        body,
        grid=(nb // bb,),
        in_specs=in_specs,
        out_specs=out_specs,
        out_shape=out_shape,
        scratch_shapes=scratch,
        input_output_aliases=aliases,
        compiler_params=pltpu.CompilerParams(dimension_semantics=("arbitrary",),
                                             vmem_limit_bytes=VMEM_LIMIT),
        name="mixer_t%d" % t,
    )(*args)


def _route_rows(logits_t, b_router_col):
    scores = _sigmoid(logits_t)
    sel = scores + b_router_col
    sel_r = [sel[e:e + 1, :] for e in range(N_EXPERTS)]
    sc_r = [scores[e:e + 1, :] for e in range(N_EXPERTS)]

    best = None
    best_v = None
    for g in range(N_EXPERT_GROUPS):
        a, b, c, d = sel_r[g * EXPERTS_PER_GROUP:(g + 1) * EXPERTS_PER_GROUP]
        hi1, lo1 = jnp.maximum(a, b), jnp.minimum(a, b)
        hi2, lo2 = jnp.maximum(c, d), jnp.minimum(c, d)
        gs = jnp.maximum(hi1, hi2) + jnp.maximum(jnp.minimum(hi1, hi2), jnp.maximum(lo1, lo2))
        if g == 0:
            best = jnp.zeros(gs.shape, jnp.int32)
            best_v = gs
        else:
            upd = gs > best_v
            best = jnp.where(upd, g, best)
            best_v = jnp.where(upd, gs, best_v)

    def pick(rows, j):
        out = rows[(N_EXPERT_GROUPS - 1) * EXPERTS_PER_GROUP + j]
        for g in range(N_EXPERT_GROUPS - 2, -1, -1):
            out = jnp.where(best == g, rows[g * EXPERTS_PER_GROUP + j], out)
        return out

    s = [pick(sel_r, j) for j in range(EXPERTS_PER_GROUP)]
    sc = [pick(sc_r, j) for j in range(EXPERTS_PER_GROUP)]
    rank = [jnp.zeros(best.shape, jnp.int32) for _ in range(EXPERTS_PER_GROUP)]
    for a in range(EXPERTS_PER_GROUP):
        for b in range(a + 1, EXPERTS_PER_GROUP):
            b_first = s[b] > s[a]
            rank[a] = rank[a] + b_first.astype(jnp.int32)
            rank[b] = rank[b] + (1 - b_first.astype(jnp.int32))
    w = [jnp.where(rank[j] < 2, sc[j], 0.0) for j in range(EXPERTS_PER_GROUP)]
    tot = w[0] + w[1] + w[2] + w[3]
    return best, [wj / tot for wj in w]


def _moe_body(is_last, x_ref, mod_ref, g2_ref, wr2_ref, brc_ref, utri_ref, wg_ref, wu_ref, wd_ref, fg_ref,
              o_ref, he_s, dest_s, destl_s, pt_s, ys_s, sm_s):
    gi = pl.program_id(1)
    tm = ROWS

    @pl.when(gi == 0)
    def _():
        mod = mod_ref[0]
        sh2 = mod[:, 3 * D_MODEL:4 * D_MODEL]
        sc2 = mod[:, 4 * D_MODEL:5 * D_MODEL]
        h = _rmsnorm_rows(x_ref[...], g2_ref[0]) * (1.0 + sc2) + sh2
        h_hi, h_lo = _hi_lo(h)
        he_s[:, 0:D_MODEL] = h_hi
        lg = _dot(jnp.concatenate([h_hi, h_lo], axis=1), wr2_ref[...])
        lg = lg[:, 0:N_EXPERTS] + lg[:, N_EXPERTS:2 * N_EXPERTS]
        logits_t = jnp.concatenate([lg, jnp.zeros((tm, MOE_EXT - N_EXPERTS), F32)], axis=1).T[0:N_EXPERTS, :]
        best, cw = _route_rows(logits_t, brc_ref[...])

        row_i = lax.broadcasted_iota(jnp.int32, (16, tm), 0)
        onehot = (row_i == best).astype(F32)
        before = _dot(onehot.astype(BF16), utri_ref[...])
        dest = jnp.zeros((1, tm), F32)
        off_blk = jnp.int32(0)
        for g in range(N_EXPERT_GROUPS):
            cnt = jnp.sum(onehot[g:g + 1, :]).astype(jnp.int32)
            n_blk = lax.shift_right_logical(cnt + (MOE_BLK - 1), int(math.log2(MOE_BLK)))
            sm_s[g] = off_blk
            sm_s[N_EXPERT_GROUPS + g] = n_blk
            base = (off_blk * MOE_BLK).astype(F32)
            dest = dest + onehot[g:g + 1, :] * (before[g:g + 1, :] + base)
            off_blk = off_blk + n_blk

        cw_hi = [wj.astype(BF16).astype(F32) for wj in cw]
        cw_lo = [wj - hj for wj, hj in zip(cw, cw_hi)]
        stack = jnp.concatenate([dest] + cw_hi + cw_lo + [jnp.zeros((MOE_EXT - 9, tm), F32)], axis=0)
        he_s[:, D_MODEL:D_MODEL + MOE_EXT] = stack.T.astype(BF16)
        dest_s[...] = jnp.broadcast_to(dest, (SUBLANES, tm))
        destl_s[...] = jnp.broadcast_to(dest, (MOE_BLK, tm)).T
        pt_s[:, ROWS:] = jnp.zeros((tm, MOE_CAP - ROWS), BF16)
        ys_s[ROWS:, :] = jnp.zeros((MOE_CAP - ROWS, D_MODEL), BF16)

    first_blk = sm_s[gi]
    n_blk = sm_s[N_EXPERT_GROUPS + gi]

    def expert_rows(r0, m):
        rows_f = (lax.broadcasted_iota(jnp.int32, (m, 1), 0) + r0).astype(F32)
        p_blk = jnp.where(dest_s[0:1, :] == rows_f, 1.0, 0.0).astype(BF16)
        for k in range(m // MOE_BLK):
            cols_f = (lax.broadcasted_iota(jnp.int32, (1, MOE_BLK), 1) + (r0 + k * MOE_BLK)).astype(F32)
            pt_s[:, pl.ds(_mo(r0 + k * MOE_BLK, MOE_BLK), MOE_BLK)] = jnp.where(
                destl_s[...] == cols_f, 1.0, 0.0).astype(BF16)
        xe = _dot(p_blk, he_s[...])
        xs = xe[:, 0:D_MODEL].astype(BF16)
        acts = []
        for j in range(EXPERTS_PER_GROUP):
            gate = _dot(xs, wg_ref[0, j])
            up = _dot(xs, wu_ref[0, j])
            cwj = (xe[:, D_MODEL + 1 + j:D_MODEL + 2 + j]
                   + xe[:, D_MODEL + 1 + EXPERTS_PER_GROUP + j:D_MODEL + 2 + EXPERTS_PER_GROUP + j])
            acts.append((gate * _sigmoid(gate) * up * cwj).astype(BF16))
        ys_s[pl.ds(r0, m), :] = _dot(jnp.concatenate(acts, axis=1), wd_ref[0, 0]).astype(BF16)

    odd = lax.rem(n_blk, 2) == 1
    lead = jnp.where(odd, jnp.where(n_blk >= 3, 3, 1), 0)

    def block_pair(i, carry):
        expert_rows(_mo((first_blk + lead + 2 * i) * MOE_BLK, MOE_BLK), 2 * MOE_BLK)
        return carry

    @pl.when(lead == 3)
    def _():
        expert_rows(_mo(first_blk * MOE_BLK, MOE_BLK), 3 * MOE_BLK)

    @pl.when(lead == 1)
    def _():
        expert_rows(_mo(first_blk * MOE_BLK, MOE_BLK), MOE_BLK)

    lax.fori_loop(0, lax.shift_right_logical(n_blk - lead, 1), block_pair, 0)

    @pl.when(gi == N_EXPERT_GROUPS - 1)
    def _():
        ga2 = mod_ref[0][:, 5 * D_MODEL:6 * D_MODEL]
        xo = x_ref[...] + ga2 * _dot(pt_s[...], ys_s[...])
        if is_last:
            xo = _rmsnorm_rows(xo, fg_ref[...])
        o_ref[...] = xo


def _moe_call(x2d, mod_rows, mod_base, per_tile_mod, l, pw, is_last):
    n_tok = x2d.shape[0]
    tm = ROWS
    assert n_tok % tm == 0
    if per_tile_mod:
        mod_spec = pl.BlockSpec((1, 1, N_MOD * D_MODEL), lambda i, g: (mod_base + i, 0, 0))
    else:
        mod_spec = pl.BlockSpec((1, 1, N_MOD * D_MODEL), lambda i, g: (mod_base, 0, 0))

    def const(shape):
        return pl.BlockSpec(shape, lambda i, g: (0,) * len(shape), pipeline_mode=pl.Buffered(1))

    def grp(shape):
        return pl.BlockSpec((1, EXPERTS_PER_GROUP) + shape, lambda i, g: (l, g, 0, 0))

    return pl.pallas_call(
        functools.partial(_moe_body, is_last),
        grid=(n_tok // tm, N_EXPERT_GROUPS),
        in_specs=[
            pl.BlockSpec((tm, D_MODEL), lambda i, g: (i, 0)),
            mod_spec,
            pl.BlockSpec((1, 1, D_MODEL), lambda i, g: (l, 0, 0), pipeline_mode=pl.Buffered(1)),
            const((2 * D_MODEL, 2 * N_EXPERTS)),
            const((N_EXPERTS, 1)),
            const((tm, tm)),
            grp((D_MODEL, D_EXPERT)),
            grp((D_MODEL, D_EXPERT)),
            pl.BlockSpec((1, 1, EXPERTS_PER_GROUP * D_EXPERT, D_MODEL), lambda i, g: (l, g, 0, 0)),
            const((1, D_MODEL)),
        ],
        out_specs=pl.BlockSpec((tm, D_MODEL), lambda i, g: (i, 0)),
        out_shape=jax.ShapeDtypeStruct((n_tok, D_MODEL), F32),
        scratch_shapes=[
            pltpu.VMEM((tm, D_MODEL + MOE_EXT), BF16),
            pltpu.VMEM((SUBLANES, tm), F32),
            pltpu.VMEM((tm, MOE_BLK), F32),
            pltpu.VMEM((tm, MOE_CAP), BF16),
            pltpu.VMEM((MOE_CAP, D_MODEL), BF16),
            pltpu.SMEM((2 * N_EXPERT_GROUPS,), jnp.int32),
        ],
        compiler_params=pltpu.CompilerParams(dimension_semantics=("arbitrary", "arbitrary"),
                                             vmem_limit_bytes=VMEM_LIMIT),
        name="moe",
    )(x2d, mod_rows, pw["g2"], pw["wr2"], pw["brc"], pw["utri"], pw["weg"], pw["weu"], pw["wed"], pw["fg"])


def _dft_tables(t):
    idx = np.arange(t, dtype=np.int64)
    ang = 2.0 * np.pi * ((idx[:, None] * idx[None, :]) % t).astype(np.float64) / t
    scale = 1.0 / math.sqrt(t * D_FOURIER_GROUP)
    return (np.cos(ang) * scale).astype(np.float32), (-np.sin(ang) * scale).astype(np.float32)


def _group_tables():
    idx = np.arange(D_FOURIER_GROUP, dtype=np.int64)
    ang = 2.0 * np.pi * ((idx[:, None] * idx[None, :]) % D_FOURIER_GROUP).astype(np.float64) / D_FOURIER_GROUP
    eye = np.eye(N_FOURIER_GROUPS)
    return np.concatenate([np.kron(eye, np.cos(ang)), np.kron(eye, np.sin(ang))], axis=1).astype(np.float32)


def _grid_pos_tables(n_tokens, dtype):
    quarter = D_MODEL // 4
    omega = 1.0 / (10000.0 ** (jnp.arange(quarter, dtype=jnp.float32) / quarter))

    def enc(p):
        a = p[:, None] * omega[None, :]
        return jnp.concatenate([jnp.sin(a), jnp.cos(a)], axis=-1).astype(dtype)

    return (enc(jnp.arange(n_tokens // GRID_W, dtype=jnp.float32)), enc(jnp.arange(GRID_W, dtype=jnp.float32)))


def _router_hi_lo(w_router):
    w_hi = w_router.astype(BF16)
    w_lo = (w_router - w_hi.astype(F32)).astype(BF16)
    return jnp.concatenate([jnp.concatenate([w_hi, w_lo], axis=1),
                            jnp.concatenate([w_hi, jnp.zeros_like(w_lo)], axis=1)], axis=0)


def _prepare(seq_lens, norm1_g, norm2_g, w_in, w_fnet, w_dw, b_dw, conv_ln_g, conv_ln_b, w_pw, b_gate, g_mh,
             w_out, w_router, b_router, w_exp_gate, w_exp_up, w_exp_down, final_g):
    dft = {t: _dft_tables(t) for t in sorted(set(seq_lens))}
    w_in_b = w_in.astype(BF16)
    tok = np.arange(ROWS)
    return dict(
        g1=norm1_g.reshape(DEPTH, 1, D_MODEL), g2=norm2_g.reshape(DEPTH, 1, D_MODEL),
        win=jnp.swapaxes(w_in_b, 1, 2), bg=b_gate.reshape(DEPTH, 1, N_GATES),
        blk=jnp.asarray(_group_tables()).astype(BF16),
        dc={t: jnp.asarray(v[0]).astype(BF16) for t, v in dft.items()},
        ds={t: jnp.asarray(v[1]).astype(BF16) for t, v in dft.items()},
        wfn=w_fnet.astype(BF16),
        wdw=jnp.concatenate([w_dw, jnp.zeros((DEPTH, 1, D_CONV), F32)], axis=1),
        bdw=b_dw.reshape(DEPTH, 1, D_CONV), lng=conv_ln_g.reshape(DEPTH, 1, D_CONV),
        lnb=conv_ln_b.reshape(DEPTH, 1, D_CONV), wpw=w_pw.astype(BF16),
        gmh=jnp.broadcast_to(g_mh.reshape(DEPTH, D_MLSTM, 1), (DEPTH, D_MLSTM, CHUNK)),
        wout=w_out.astype(BF16),
        wr2=_router_hi_lo(w_router), brc=b_router.reshape(N_EXPERTS, 1),
        utri=jnp.asarray((tok[:, None] < tok[None, :]).astype(np.float32)).astype(BF16),
        weg=w_exp_gate.astype(BF16), weu=w_exp_up.astype(BF16),
        wed=w_exp_down.reshape(DEPTH, N_EXPERT_GROUPS, EXPERTS_PER_GROUP * D_EXPERT, D_MODEL).astype(BF16),
        fg=final_g.reshape(1, D_MODEL),
    )


def kernel(x_prompt, x_sample, state_C, state_n, state_m, c, c_ctx, w_ada, b_ada, norm1_g, norm2_g, w_in, w_fnet, w_dw, b_dw, conv_ln_g, conv_ln_b, w_pw, b_gate, g_mh, w_out, w_router, b_router, w_exp_gate, w_exp_up, w_exp_down, final_g):
    bp, tp, _ = x_prompt.shape
    bs, ts, _ = x_sample.shape

    cv = jnp.zeros((MOD_ROWS, D_MODEL), F32).at[:bs].set(c).at[bs].set(c_ctx)
    mod_all = _ada_call(cv, w_ada, b_ada)
    mod_rows = mod_all.reshape(DEPTH * MOD_ROWS, 1, N_MOD * D_MODEL)

    pw = _prepare((tp, ts), norm1_g, norm2_g, w_in, w_fnet, w_dw, b_dw, conv_ln_g, conv_ln_b, w_pw, b_gate,
                  g_mh, w_out, w_router, b_router, w_exp_gate, w_exp_up, w_exp_down, final_g)
    pos = _grid_pos_tables(ts, x_sample.dtype)

    c0 = state_C.reshape(bs, DEPTH, N_UNITS, HEAD_DIM_M, HEAD_DIM_M)
    n0 = state_n.reshape(bs, DEPTH, N_UNITS, HEAD_DIM_M)
    m0 = jnp.pad(state_m, ((0, 0), (0, 0), (0, 0), (0, GATE_ROWS - N_HEADS_M)))
    m0 = jnp.broadcast_to(m0[..., None], (bs, DEPTH, N_DIRS, GATE_ROWS, CHUNK))

    xp, xs = x_prompt, x_sample
    states = ()
    for l in range(DEPTH):
        is_last = l == DEPTH - 1
        base = l * MOD_ROWS
        xp, *states = _mixer_call(xp, None, mod_rows, base + bs, False, l, pw, None, tuple(states))
        xp = _moe_call(xp.reshape(bp * tp, D_MODEL), mod_rows, base + bs, False, l, pw, is_last)
        xp = xp.reshape(bp, tp, D_MODEL)

        (xs,) = _mixer_call(xs, pos if l == 0 else None, mod_rows, base, True, l, pw, (c0, n0, m0), None)
        xs = _moe_call(xs.reshape(bs * ts, D_MODEL), mod_rows, base, True, l, pw, is_last)
        xs = xs.reshape(bs, ts, D_MODEL)

    c_all, n_all, m_all = states
    new_c = c_all.reshape(bp, DEPTH, N_DIRS, N_HEADS_M, HEAD_DIM_M, HEAD_DIM_M)
    new_n = n_all.reshape(bp, DEPTH, N_DIRS, N_HEADS_M, HEAD_DIM_M)
    new_m = m_all[:, :, :, :N_HEADS_M, 0]
    return (xp, xs, new_c, new_n, new_m)
```

```python
import functools
import math

import numpy as np
import jax
import jax.numpy as jnp
from jax import lax
from jax.experimental import pallas as pl
from jax.experimental.pallas import tpu as pltpu

D_MODEL = 1024
DEPTH = 4
GRID_W = 64
EPS = 1e-6
D_FOURIER = 256
N_FOURIER_GROUPS = 4
D_FOURIER_GROUP = D_FOURIER // N_FOURIER_GROUPS
D_CONV = 256
CONV_WIDTH = 31
CONV_PAD = CONV_WIDTH // 2
D_MLSTM = 512
N_HEADS_M = 4
HEAD_DIM_M = D_MLSTM // N_HEADS_M
N_DIRS = 2
CHUNK = 128
N_GATES = 2 * N_DIRS * N_HEADS_M
N_UNITS = N_DIRS * N_HEADS_M
D_MIX = D_FOURIER + D_CONV + D_MLSTM
D_IN_PROJ = D_FOURIER + 2 * D_CONV + 4 * D_MLSTM + N_GATES
N_EXPERTS = 16
N_EXPERT_GROUPS = 4
EXPERTS_PER_GROUP = N_EXPERTS // N_EXPERT_GROUPS
D_EXPERT = 256
N_MOD = 6

O_B = D_FOURIER
O_Q = O_B + 2 * D_CONV
O_K = O_Q + D_MLSTM
O_V = O_K + D_MLSTM
O_O = O_V + D_MLSTM
O_G = O_O + D_MLSTM

ROWS = 1024
CONV_ROW_TILE = 64
PAD_LO = 16
MOD_ROWS = 16
SUBLANES = 8
SHIFT_ROWS = 24
AUG = 16
HEAD_AUG = HEAD_DIM_M + AUG
GATE_ROWS = 8
MOE_BLK = 128
MOE_CAP = ROWS + N_EXPERT_GROUPS * MOE_BLK
MOE_EXT = 128
VMEM_LIMIT = 60 * 1024 * 1024

F32 = jnp.float32
BF16 = jnp.bfloat16


def _dot(a, b):
    return jnp.dot(a, b, preferred_element_type=F32)


def _dot_nt(a, b):
    return lax.dot_general(a, b, (((1,), (1,)), ((), ())), preferred_element_type=F32)


def _mo(x, m):
    return x if isinstance(x, int) else pl.multiple_of(x, m)


def _sigmoid(x):
    return 1.0 / (1.0 + jnp.exp(-x))


def _log_sigmoid(x):
    return jnp.minimum(x, 0.0) - jnp.log(1.0 + jnp.exp(-jnp.abs(x)))


def _hi_lo(a_f32):
    hi = a_f32.astype(BF16)
    return hi, (a_f32 - hi.astype(F32)).astype(BF16)


def _split_dot(a_f32, b_bf16):
    hi, lo = _hi_lo(a_f32)
    return _dot(hi, b_bf16) + _dot(lo, b_bf16)


def _split_dot_left(b_bf16, a_f32):
    hi, lo = _hi_lo(a_f32)
    return _dot(b_bf16, hi) + _dot(b_bf16, lo)


def _rmsnorm_rows(x, g):
    ms = jnp.mean(x * x, axis=-1, keepdims=True)
    return x * lax.rsqrt(ms + EPS) * g


def _ada_body(c_ref, w_ref, b_ref, o_ref):
    cv = c_ref[...]
    s_hi, s_lo = _hi_lo(cv * _sigmoid(cv))
    w_hi, w_lo = _hi_lo(w_ref[0])
    both = _dot(jnp.concatenate([s_hi, s_lo], axis=0), w_hi)
    o_ref[0] = both[0:MOD_ROWS] + both[MOD_ROWS:2 * MOD_ROWS] + _dot(s_hi, w_lo) + b_ref[0]


def _ada_call(cv, w_ada, b_ada):
    n_col = N_MOD * D_MODEL
    tn = D_MODEL
    return pl.pallas_call(
        _ada_body,
        grid=(DEPTH, n_col // tn),
        in_specs=[
            pl.BlockSpec((MOD_ROWS, D_MODEL), lambda l, j: (0, 0)),
            pl.BlockSpec((1, D_MODEL, tn), lambda l, j: (l, 0, j)),
            pl.BlockSpec((1, 1, tn), lambda l, j: (l, 0, j)),
        ],
        out_specs=pl.BlockSpec((1, MOD_ROWS, tn), lambda l, j: (l, 0, j)),
        out_shape=jax.ShapeDtypeStruct((DEPTH, MOD_ROWS, n_col), F32),
        compiler_params=pltpu.CompilerParams(dimension_semantics=("arbitrary", "arbitrary")),
        name="ada",
    )(cv, w_ada, b_ada.reshape(DEPTH, 1, n_col))


def _scan_max_lanes(x, reverse):
    lane = lax.broadcasted_iota(jnp.int32, x.shape, 1)
    neg_inf = jnp.float32(-jnp.inf)
    sh = 1
    while sh < CHUNK:
        if reverse:
            y = pltpu.roll(x, CHUNK - sh, axis=1)
            x = jnp.maximum(x, jnp.where(lane < CHUNK - sh, y, neg_inf))
        else:
            y = pltpu.roll(x, sh, axis=1)
            x = jnp.maximum(x, jnp.where(lane >= sh, y, neg_inf))
        sh *= 2
    return x


def _mixer_body(bb, t, has_init, emit_state, has_prev, add_pos, *refs):
    rows = bb * t
    nc = t // CHUNK
    n_blk = rows // CHUNK
    refs = list(refs)
    x_ref = refs.pop(0)
    if add_pos:
        posr_ref, posc_ref = refs[:2]
        refs = refs[2:]
    (mod_ref, g1_ref, win_ref, bg_ref, blk_ref, dc_ref, ds_ref, wfn_ref,
     wdw_ref, bdw_ref, lng_ref, lnb_ref, wpw_ref, gmh_ref) = refs[:14]
    refs = refs[14:]
    if has_init:
        c0_ref, n0_ref, m0_ref = refs[:3]
        refs = refs[3:]
    wout_ref = refs.pop(0)
    if has_prev:
        refs = refs[3:]
    x1_ref = refs.pop(0)
    if emit_state:
        co_ref, no_ref, mo_ref = refs[:3]
        refs = refs[3:]
    (hb_s, xa_s, ucs_s, y_s, pad_s, shf_s, cact_s, qt_s, k_s, vta_s, so_s, rows_s, acol_s, mprev_s, ctab_s,
     mix_s, cta_s, mst_s) = refs

    if add_pos:
        half_d = D_MODEL // 2
        for g in range(t // GRID_W):
            rs = slice(g * GRID_W, (g + 1) * GRID_W)
            emb = jnp.concatenate([jnp.broadcast_to(posr_ref[g:g + 1, :], (GRID_W, half_d)), posc_ref[...]], axis=1)
            x1_ref[0, rs, :] = x_ref[0, rs, :] + emb

    def load_x():
        src = x1_ref if add_pos else x_ref
        return src[...].reshape(rows, D_MODEL)

    def w_in(lo, hi):
        return win_ref[0, lo:hi, :]

    x = load_x()
    mod = mod_ref[0]
    sh1 = mod[:, 0:D_MODEL]
    sc1 = mod[:, D_MODEL:2 * D_MODEL]
    h = _rmsnorm_rows(x, g1_ref[0]) * (1.0 + sc1) + sh1
    hb = h.astype(BF16)

    pb = _dot_nt(hb, w_in(O_B, O_Q))
    u = pb[:, :D_CONV] * _sigmoid(pb[:, D_CONV:])
    zpad = jnp.zeros((PAD_LO, D_CONV), F32)
    for i in range(bb):
        pad_s[i, 0:PAD_LO, :] = zpad
        pad_s[i, PAD_LO:PAD_LO + t, :] = u[i * t:(i + 1) * t]
        pad_s[i, PAD_LO + t:2 * PAD_LO + t, :] = zpad

    for i in range(bb):
        for s8 in range(1, SUBLANES):
            shf_s[s8 - 1, i] = pad_s[i, s8:s8 + t + SHIFT_ROWS, :]

    def conv_tile(seq, r0):
        acc = jnp.broadcast_to(bdw_ref[0], (CONV_ROW_TILE, D_CONV))
        for j in range(CONV_WIDTH):
            q = j + PAD_LO - CONV_PAD
            win_rows = pl.ds(_mo(r0 + (q // SUBLANES) * SUBLANES, SUBLANES), CONV_ROW_TILE)
            win = pad_s[seq, win_rows, :] if q % SUBLANES == 0 else shf_s[q % SUBLANES - 1, seq, win_rows, :]
            acc = acc + win * wdw_ref[0, j:j + 1, :]
        mu = jnp.mean(acc, axis=-1, keepdims=True)
        cen = acc - mu
        var = jnp.mean(cen * cen, axis=-1, keepdims=True)
        uf = cen * lax.rsqrt(var + EPS) * lng_ref[0] + lnb_ref[0]
        return (uf * _sigmoid(uf)).astype(BF16)

    hb_s[...] = hb
    k_s[...] = (_dot_nt(hb, w_in(O_K, O_V)) * (HEAD_DIM_M ** -0.5)).astype(BF16)
    og = _dot_nt(hb, w_in(O_O, D_IN_PROJ))
    so_s[...] = _sigmoid(og[:, 0:D_MLSTM]).astype(BF16)
    qf = _dot_nt(hb, w_in(O_Q, O_K))
    for b in range(n_blk):
        qt_s[b] = qf[b * CHUNK:(b + 1) * CHUNK, :].T.astype(BF16)
    vf = _dot_nt(hb, w_in(O_V, O_O))
    ones_rows = jnp.where(lax.broadcasted_iota(jnp.int32, (AUG, CHUNK), 0) == 0, 1.0, 0.0).astype(BF16)
    for b in range(n_blk):
        vt = vf[b * CHUNK:(b + 1) * CHUNK, :].T.astype(BF16)
        for hh in range(N_HEADS_M):
            vta_s[b, hh, 0:HEAD_DIM_M, :] = vt[hh * HEAD_DIM_M:(hh + 1) * HEAD_DIM_M, :]
            vta_s[b, hh, HEAD_DIM_M:HEAD_AUG, :] = ones_rows

    r_i = lax.broadcasted_iota(jnp.int32, (CHUNK, CHUNK), 0)
    c_i = lax.broadcasted_iota(jnp.int32, (CHUNK, CHUNK), 1)
    lower = r_i >= c_i
    upper = r_i <= c_i
    tri_lo = lower.astype(BF16)
    tri_up = upper.astype(BF16)
    neg_inf = jnp.float32(-jnp.inf)
    n_gr = n_blk * GATE_ROWS

    gcol = og[:, D_MLSTM:D_MLSTM + N_GATES] + bg_ref[0]
    kind_c = (lax.broadcasted_iota(jnp.int32, gcol.shape, 1) // N_HEADS_M) % 2
    gcol = jnp.where(kind_c == 1, _log_sigmoid(gcol), gcol)
    grow = jnp.concatenate([gcol, jnp.zeros((rows, CHUNK - N_GATES), F32)], axis=1).T[0:N_GATES, :]
    for d in range(N_DIRS):
        g0 = d * 2 * N_HEADS_M
        ll = CHUNK - 1 if d == 0 else 0
        xg = jnp.concatenate([grow[g0:g0 + GATE_ROWS, b * CHUNK:(b + 1) * CHUNK] for b in range(n_blk)], axis=0)
        b_rows = pltpu.roll(_split_dot(xg, tri_up if d == 0 else tri_lo), n_gr - N_HEADS_M, axis=0)
        a_rows = xg - b_rows
        a_run = _scan_max_lanes(a_rows, d == 1)
        b_last = jnp.broadcast_to(b_rows[:, ll:ll + 1], (n_gr, CHUNK))
        a_max = jnp.broadcast_to(a_run[:, ll:ll + 1], (n_gr, CHUNK))
        rows_s[d, 0] = b_rows
        rows_s[d, 1] = a_run
        rows_s[d, 2] = b_last
        rows_s[d, 3] = a_max
        rows_s[d, 4] = jnp.exp(a_rows - a_max)
        for b in range(n_blk):
            gc = gcol[b * CHUNK:(b + 1) * CHUNK, :]
            bcol = _split_dot_left(tri_lo if d == 0 else tri_up, gc)
            acol_s[d, b] = gc[:, g0:g0 + N_HEADS_M] - bcol[:, g0 + N_HEADS_M:g0 + 2 * N_HEADS_M]

    def fnet_proj():
        xa_s[...] = _dot_nt(hb_s[...], w_in(0, O_B)).astype(BF16)

    def fnet_channels():
        ucs_s[...] = _dot(xa_s[...], blk_ref[...]).astype(BF16)

    def fnet_positions(seqs, part):
        for i in seqs:
            rs = slice(i * t, (i + 1) * t)
            if part == 0:
                y_s[rs, :] = _dot(dc_ref[...], ucs_s[rs, 0:D_FOURIER])
            else:
                y_s[rs, :] = y_s[rs, :] + _dot(ds_ref[...], ucs_s[rs, D_FOURIER:2 * D_FOURIER])

    def fnet_linear():
        mix_s[:, 0:D_FOURIER] = _dot(y_s[...].astype(BF16), wfn_ref[0]).astype(BF16)

    all_seqs = tuple(range(bb))
    side_work = [fnet_proj, fnet_channels,
                 functools.partial(fnet_positions, all_seqs, 0), functools.partial(fnet_positions, all_seqs, 1),
                 fnet_linear]

    def state_step(pair, side):
        seq = (2 * pair) // nc
        c0 = (2 * pair) % nc

        if c0 == 0:
            if has_init:
                first_row = lax.broadcasted_iota(jnp.int32, (AUG, HEAD_DIM_M), 0) == 0
                for j in range(N_UNITS):
                    cta_s[j, 0:HEAD_DIM_M, :] = c0_ref[seq, 0, j].T
                    cta_s[j, HEAD_DIM_M:HEAD_AUG, :] = jnp.where(first_row, n0_ref[seq, 0, j:j + 1, :], 0.0)
                mst_s[...] = m0_ref[seq, 0]
            else:
                cta_s[...] = jnp.zeros(cta_s.shape, F32)
                mst_s[...] = jnp.zeros(mst_s.shape, F32)

        m_cur = [mst_s[d] for d in range(N_DIRS)]
        steps = []
        for sub in range(2):
            c = c0 + sub
            units = []
            for d in range(N_DIRS):
                blk = seq * nc + (c if d == 0 else nc - 1 - c)
                row0 = _mo(blk * CHUNK, CHUNK)
                gr0 = _mo(blk * GATE_ROWS, GATE_ROWS)
                b_last = rows_s[d, 2, pl.ds(gr0, GATE_ROWS), :]
                a_max = rows_s[d, 3, pl.ds(gr0, GATE_ROWS), :]
                w_rows = rows_s[d, 4, pl.ds(gr0, GATE_ROWS), :]
                m_prev = m_cur[d]
                mprev_s[d, blk] = m_prev
                m_new = b_last + jnp.maximum(m_prev, a_max)
                m_cur[d] = m_new
                decay = jnp.exp(b_last + m_prev - m_new)
                fac = jnp.exp(a_max + b_last - m_new)
                for hh in range(N_HEADS_M):
                    hs = slice(hh * HEAD_DIM_M, (hh + 1) * HEAD_DIM_M)
                    units.append(dict(d=d, hh=hh, j=d * N_HEADS_M + hh, blk=blk,
                                      kc=k_s[pl.ds(row0, CHUNK), hs], vta=vta_s[blk, hh], w=w_rows[hh:hh + 1, :],
                                      decay=decay[hh:hh + 1, :], fac=fac[hh:hh + 1, :]))
            steps.append(units)
        for d in range(N_DIRS):
            mst_s[d] = m_cur[d]
        for units in steps:
            for un in units:
                vw = (un["vta"].astype(F32) * un["w"]).astype(BF16)
                un["upd"] = _dot(vw, un["kc"])
        if side is not None:
            side()
        for sub in range(2):
            it = 2 * pair + sub
            cact_s[pl.ds(_mo(it * CHUNK, CHUNK), CHUNK // 2), :] = conv_tile(seq, (c0 + sub) * CHUNK)
        cta = [cta_s[j] for j in range(N_UNITS)]
        for units in steps:
            for un in units:
                j = un["j"]
                ctab_s[un["blk"], un["hh"], un["d"] * HEAD_AUG:(un["d"] + 1) * HEAD_AUG, :] = cta[j].astype(BF16)
                cta[j] = un["decay"] * cta[j] + un["fac"] * un["upd"]
        for j in range(N_UNITS):
            cta_s[j] = cta[j]

        if emit_state and c0 == nc - 2:
            for j in range(N_UNITS):
                co_ref[seq, 0, j] = cta_s[j, 0:HEAD_DIM_M, :].T
                no_ref[seq, 0, j:j + 1, :] = cta_s[j, HEAD_DIM_M:HEAD_DIM_M + 1, :]
            mo_ref[seq, 0] = mst_s[...]

    n_pairs = n_blk // 2
    for pair in range(n_pairs):
        state_step(pair, side_work.pop(0) if side_work else None)

    def chunk_out(pair, side):
        heads = []
        for blk in (2 * pair, 2 * pair + 1):
            row0 = _mo(blk * CHUNK, CHUNK)
            gr0 = _mo(blk * GATE_ROWS, GATE_ROWS)
            dirs = []
            for d in range(N_DIRS):
                m_prev = mprev_s[d, blk]
                mm = jnp.maximum(m_prev, rows_s[d, 1, pl.ds(gr0, GATE_ROWS), :])
                floor = jnp.exp(-(rows_s[d, 0, pl.ds(gr0, GATE_ROWS), :] + mm))
                dirs.append(dict(mm=mm, inter=jnp.exp(m_prev - mm), floor=floor,
                                 acol=acol_s[d, blk], mask=upper if d == 0 else lower))
            for hh in range(N_HEADS_M):
                hs = slice(hh * HEAD_DIM_M, (hh + 1) * HEAD_DIM_M)
                hd = dict(hh=hh, hs=hs, blk=blk, row0=row0, dirs=dirs,
                          kc=k_s[pl.ds(row0, CHUNK), hs], qt=qt_s[blk, hs, :], vta=vta_s[blk, hh])
                sp = _dot(jnp.concatenate([hd["kc"], ctab_s[blk, hh]], axis=0), hd["qt"])
                hd["st"] = sp[0:CHUNK]
                hd["p1"] = [sp[CHUNK + d * HEAD_AUG:CHUNK + (d + 1) * HEAD_AUG] for d in range(N_DIRS)]
                heads.append(hd)
        for hd in heads:
            hh = hd["hh"]
            sm = []
            for dd in hd["dirs"]:
                z = dd["acol"][:, hh:hh + 1] - dd["mm"][hh:hh + 1, :]
                sm.append((hd["st"] * jnp.exp(jnp.where(dd["mask"], z, neg_inf))).astype(BF16))
            hd["p2"] = _dot(hd["vta"], jnp.concatenate(sm, axis=1))
        if side is not None:
            side()
        half = CHUNK // 2
        for blk in (2 * pair, 2 * pair + 1):
            row0 = _mo(blk * CHUNK, CHUNK)
            cact_s[pl.ds(row0 + half, half), :] = conv_tile(blk // nc, (blk % nc) * CHUNK + half)
        for b2 in range(2):
            parts = []
            for hd in heads[b2 * N_HEADS_M:(b2 + 1) * N_HEADS_M]:
                hh = hd["hh"]
                hsum = None
                for d, dd in enumerate(hd["dirs"]):
                    numa = dd["inter"][hh:hh + 1, :] * hd["p1"][d] + hd["p2"][:, d * CHUNK:(d + 1) * CHUNK]
                    den = numa[HEAD_DIM_M:HEAD_DIM_M + 1, :]
                    ht = numa[0:HEAD_DIM_M, :] / jnp.maximum(jnp.abs(den), dd["floor"][hh:hh + 1, :])
                    hsum = ht if hsum is None else hsum + ht
                r = lax.rsqrt(jnp.mean(hsum * hsum, axis=0, keepdims=True) + EPS)
                parts.append((hsum * r * gmh_ref[0, hd["hs"], :]).T)
            row0 = heads[b2 * N_HEADS_M]["row0"]
            oc = jnp.concatenate(parts, axis=1) * so_s[pl.ds(row0, CHUNK), :].astype(F32)
            mix_s[pl.ds(row0, CHUNK), D_FOURIER + D_CONV:D_MIX] = oc.astype(BF16)

    for pair in range(n_pairs):
        chunk_out(pair, side_work.pop(0) if side_work else None)
    assert not side_work
    mix_s[:, D_FOURIER:D_FOURIER + D_CONV] = _dot(cact_s[...], wpw_ref[0]).astype(BF16)

    ga1 = mod_ref[0][:, 2 * D_MODEL:3 * D_MODEL]
    res = _dot(mix_s[...], wout_ref[0])
    x1_ref[...] = (load_x() + ga1 * res).reshape(bb, t, D_MODEL)


def _whole(a):
    nd = a.ndim
    return pl.BlockSpec(a.shape, lambda g: (0,) * nd, pipeline_mode=pl.Buffered(1))


def _layer_block(a, l):
    nd = a.ndim
    return pl.BlockSpec((1,) + a.shape[1:], lambda g: (l,) + (0,) * (nd - 1), pipeline_mode=pl.Buffered(1))


def _mixer_call(x, pos, mod_rows, mod_base, per_batch_mod, l, pw, state0, prev_states):
    emit_state = prev_states is not None
    has_prev = bool(prev_states)
    nb, t, _ = x.shape
    bb = ROWS // t
    assert bb * t == ROWS and nb % bb == 0 and t % CHUNK == 0
    rows = ROWS
    n_blk = rows // CHUNK
    has_init = state0 is not None
    add_pos = pos is not None

    args = [x]
    in_specs = [pl.BlockSpec((bb, t, D_MODEL), lambda g: (g, 0, 0))]
    if add_pos:
        assert bb == 1
        args += list(pos)
        in_specs += [_whole(a) for a in pos]
    args.append(mod_rows)
    if per_batch_mod:
        assert bb == 1
        in_specs.append(pl.BlockSpec((1, 1, N_MOD * D_MODEL), lambda g: (mod_base + g, 0, 0)))
    else:
        in_specs.append(pl.BlockSpec((1, 1, N_MOD * D_MODEL), lambda g: (mod_base, 0, 0)))
    for name in ("g1", "win", "bg"):
        args.append(pw[name])
        in_specs.append(_layer_block(pw[name], l))
    for a in (pw["blk"], pw["dc"][t], pw["ds"][t]):
        args.append(a)
        in_specs.append(_whole(a))
    for name in ("wfn", "wdw", "bdw", "lng", "lnb", "wpw", "gmh"):
        args.append(pw[name])
        in_specs.append(_layer_block(pw[name], l))
    if has_init:
        c0, n0, m0 = state0
        args += [c0, n0, m0]
        in_specs += [
            pl.BlockSpec((bb, 1, N_UNITS, HEAD_DIM_M, HEAD_DIM_M), lambda g: (g, l, 0, 0, 0)),
            pl.BlockSpec((bb, 1, N_UNITS, HEAD_DIM_M), lambda g: (g, l, 0, 0)),
            pl.BlockSpec((bb, 1, N_DIRS, GATE_ROWS, CHUNK), lambda g: (g, l, 0, 0, 0)),
        ]
    args.append(pw["wout"])
    in_specs.append(_layer_block(pw["wout"], l))
    aliases = {}
    if has_prev:
        aliases = {len(args) + i: 1 + i for i in range(3)}
        args += list(prev_states)
        in_specs += [pl.BlockSpec(memory_space=pl.ANY)] * 3

    out_shape = [jax.ShapeDtypeStruct((nb, t, D_MODEL), F32)]
    out_specs = [pl.BlockSpec((bb, t, D_MODEL), lambda g: (g, 0, 0))]
    if emit_state:
        out_shape += [
            jax.ShapeDtypeStruct((nb, DEPTH, N_UNITS, HEAD_DIM_M, HEAD_DIM_M), F32),
            jax.ShapeDtypeStruct((nb, DEPTH, N_UNITS, HEAD_DIM_M), F32),
            jax.ShapeDtypeStruct((nb, DEPTH, N_DIRS, GATE_ROWS, CHUNK), F32),
        ]
        out_specs += [
            pl.BlockSpec((bb, 1, N_UNITS, HEAD_DIM_M, HEAD_DIM_M), lambda g: (g, l, 0, 0, 0)),
            pl.BlockSpec((bb, 1, N_UNITS, HEAD_DIM_M), lambda g: (g, l, 0, 0)),
            pl.BlockSpec((bb, 1, N_DIRS, GATE_ROWS, CHUNK), lambda g: (g, l, 0, 0, 0)),
        ]
    scratch = [
        pltpu.VMEM((rows, D_MODEL), BF16),
        pltpu.VMEM((rows, D_FOURIER), BF16),
        pltpu.VMEM((rows, 2 * D_FOURIER), BF16),
        pltpu.VMEM((rows, D_FOURIER), F32),
        pltpu.VMEM((bb, t + 2 * PAD_LO, D_CONV), F32),
        pltpu.VMEM((SUBLANES - 1, bb, t + SHIFT_ROWS, D_CONV), F32),
        pltpu.VMEM((rows, D_CONV), BF16),
        pltpu.VMEM((n_blk, D_MLSTM, CHUNK), BF16),
        pltpu.VMEM((rows, D_MLSTM), BF16),
        pltpu.VMEM((n_blk, N_HEADS_M, HEAD_AUG, CHUNK), BF16),
        pltpu.VMEM((rows, D_MLSTM), BF16),
        pltpu.VMEM((N_DIRS, 5, n_blk * GATE_ROWS, CHUNK), F32),
        pltpu.VMEM((N_DIRS, n_blk, CHUNK, N_HEADS_M), F32),
        pltpu.VMEM((N_DIRS, n_blk, GATE_ROWS, CHUNK), F32),
        pltpu.VMEM((n_blk, N_HEADS_M, N_DIRS * HEAD_AUG, HEAD_DIM_M), BF16),
        pltpu.VMEM((rows, D_MIX), BF16),
        pltpu.VMEM((N_UNITS, HEAD_AUG, HEAD_DIM_M), F32),
        pltpu.VMEM((N_DIRS, GATE_ROWS, CHUNK), F32),
    ]
    body = functools.partial(_mixer_body, bb, t, has_init, emit_state, has_prev, add_pos)
    return pl.pallas_call(
        body,
        grid=(nb // bb,),
        in_specs=in_specs,
        out_specs=out_specs,
        out_shape=out_shape,
        scratch_shapes=scratch,
        input_output_aliases=aliases,
        compiler_params=pltpu.CompilerParams(dimension_semantics=("arbitrary",),
                                             vmem_limit_bytes=VMEM_LIMIT),
        name="mixer_t%d" % t,
    )(*args)


def _route_rows(logits_t, b_router_col):
    scores = _sigmoid(logits_t)
    sel = scores + b_router_col
    sel_r = [sel[e:e + 1, :] for e in range(N_EXPERTS)]
    sc_r = [scores[e:e + 1, :] for e in range(N_EXPERTS)]

    best = None
    best_v = None
    for g in range(N_EXPERT_GROUPS):
        a, b, c, d = sel_r[g * EXPERTS_PER_GROUP:(g + 1) * EXPERTS_PER_GROUP]
        hi1, lo1 = jnp.maximum(a, b), jnp.minimum(a, b)
        hi2, lo2 = jnp.maximum(c, d), jnp.minimum(c, d)
        gs = jnp.maximum(hi1, hi2) + jnp.maximum(jnp.minimum(hi1, hi2), jnp.maximum(lo1, lo2))
        if g == 0:
            best = jnp.zeros(gs.shape, jnp.int32)
            best_v = gs
        else:
            upd = gs > best_v
            best = jnp.where(upd, g, best)
            best_v = jnp.where(upd, gs, best_v)

    def pick(rows, j):
        out = rows[(N_EXPERT_GROUPS - 1) * EXPERTS_PER_GROUP + j]
        for g in range(N_EXPERT_GROUPS - 2, -1, -1):
            out = jnp.where(best == g, rows[g * EXPERTS_PER_GROUP + j], out)
        return out

    s = [pick(sel_r, j) for j in range(EXPERTS_PER_GROUP)]
    sc = [pick(sc_r, j) for j in range(EXPERTS_PER_GROUP)]
    rank = [jnp.zeros(best.shape, jnp.int32) for _ in range(EXPERTS_PER_GROUP)]
    for a in range(EXPERTS_PER_GROUP):
        for b in range(a + 1, EXPERTS_PER_GROUP):
            b_first = s[b] > s[a]
            rank[a] = rank[a] + b_first.astype(jnp.int32)
            rank[b] = rank[b] + (1 - b_first.astype(jnp.int32))
    w = [jnp.where(rank[j] < 2, sc[j], 0.0) for j in range(EXPERTS_PER_GROUP)]
    tot = w[0] + w[1] + w[2] + w[3]
    return best, [wj / tot for wj in w]


def _moe_body(is_last, x_ref, mod_ref, g2_ref, wr2_ref, brc_ref, utri_ref, wg_ref, wu_ref, wd_ref, fg_ref,
              o_ref, he_s, dest_s, destl_s, pt_s, ys_s, sm_s):
    gi = pl.program_id(1)
    tm = ROWS

    @pl.when(gi == 0)
    def _():
        mod = mod_ref[0]
        sh2 = mod[:, 3 * D_MODEL:4 * D_MODEL]
        sc2 = mod[:, 4 * D_MODEL:5 * D_MODEL]
        h = _rmsnorm_rows(x_ref[...], g2_ref[0]) * (1.0 + sc2) + sh2
        h_hi, h_lo = _hi_lo(h)
        he_s[:, 0:D_MODEL] = h_hi
        lg = _dot(jnp.concatenate([h_hi, h_lo], axis=1), wr2_ref[...])
        lg = lg[:, 0:N_EXPERTS] + lg[:, N_EXPERTS:2 * N_EXPERTS]
        logits_t = jnp.concatenate([lg, jnp.zeros((tm, MOE_EXT - N_EXPERTS), F32)], axis=1).T[0:N_EXPERTS, :]
        best, cw = _route_rows(logits_t, brc_ref[...])

        row_i = lax.broadcasted_iota(jnp.int32, (16, tm), 0)
        onehot = (row_i == best).astype(F32)
        before = _dot(onehot.astype(BF16), utri_ref[...])
        dest = jnp.zeros((1, tm), F32)
        off_blk = jnp.int32(0)
        for g in range(N_EXPERT_GROUPS):
            cnt = jnp.sum(onehot[g:g + 1, :]).astype(jnp.int32)
            n_blk = lax.shift_right_logical(cnt + (MOE_BLK - 1), int(math.log2(MOE_BLK)))
            sm_s[g] = off_blk
            sm_s[N_EXPERT_GROUPS + g] = n_blk
            base = (off_blk * MOE_BLK).astype(F32)
            dest = dest + onehot[g:g + 1, :] * (before[g:g + 1, :] + base)
            off_blk = off_blk + n_blk

        cw_hi = [wj.astype(BF16).astype(F32) for wj in cw]
        cw_lo = [wj - hj for wj, hj in zip(cw, cw_hi)]
        stack = jnp.concatenate([dest] + cw_hi + cw_lo + [jnp.zeros((MOE_EXT - 9, tm), F32)], axis=0)
        he_s[:, D_MODEL:D_MODEL + MOE_EXT] = stack.T.astype(BF16)
        dest_s[...] = jnp.broadcast_to(dest, (SUBLANES, tm))
        destl_s[...] = jnp.broadcast_to(dest, (MOE_BLK, tm)).T
        pt_s[:, ROWS:] = jnp.zeros((tm, MOE_CAP - ROWS), BF16)
        ys_s[ROWS:, :] = jnp.zeros((MOE_CAP - ROWS, D_MODEL), BF16)

    first_blk = sm_s[gi]
    n_blk = sm_s[N_EXPERT_GROUPS + gi]

    def expert_rows(r0, m):
        rows_f = (lax.broadcasted_iota(jnp.int32, (m, 1), 0) + r0).astype(F32)
        p_blk = jnp.where(dest_s[0:1, :] == rows_f, 1.0, 0.0).astype(BF16)
        for k in range(m // MOE_BLK):
            cols_f = (lax.broadcasted_iota(jnp.int32, (1, MOE_BLK), 1) + (r0 + k * MOE_BLK)).astype(F32)
            pt_s[:, pl.ds(_mo(r0 + k * MOE_BLK, MOE_BLK), MOE_BLK)] = jnp.where(
                destl_s[...] == cols_f, 1.0, 0.0).astype(BF16)
        xe = _dot(p_blk, he_s[...])
        xs = xe[:, 0:D_MODEL].astype(BF16)
        acts = []
        for j in range(EXPERTS_PER_GROUP):
            gate = _dot(xs, wg_ref[0, j])
            up = _dot(xs, wu_ref[0, j])
            cwj = (xe[:, D_MODEL + 1 + j:D_MODEL + 2 + j]
                   + xe[:, D_MODEL + 1 + EXPERTS_PER_GROUP + j:D_MODEL + 2 + EXPERTS_PER_GROUP + j])
            acts.append((gate * _sigmoid(gate) * up * cwj).astype(BF16))
        ys_s[pl.ds(r0, m), :] = _dot(jnp.concatenate(acts, axis=1), wd_ref[0, 0]).astype(BF16)

    odd = lax.rem(n_blk, 2) == 1
    lead = jnp.where(odd, jnp.where(n_blk >= 3, 3, 1), 0)

    def block_pair(i, carry):
        expert_rows(_mo((first_blk + lead + 2 * i) * MOE_BLK, MOE_BLK), 2 * MOE_BLK)
        return carry

    @pl.when(lead == 3)
    def _():
        expert_rows(_mo(first_blk * MOE_BLK, MOE_BLK), 3 * MOE_BLK)

    @pl.when(lead == 1)
    def _():
        expert_rows(_mo(first_blk * MOE_BLK, MOE_BLK), MOE_BLK)

    lax.fori_loop(0, lax.shift_right_logical(n_blk - lead, 1), block_pair, 0)

    @pl.when(gi == N_EXPERT_GROUPS - 1)
    def _():
        ga2 = mod_ref[0][:, 5 * D_MODEL:6 * D_MODEL]
        xo = x_ref[...] + ga2 * _dot(pt_s[...], ys_s[...])
        if is_last:
            xo = _rmsnorm_rows(xo, fg_ref[...])
        o_ref[...] = xo


def _moe_call(x2d, mod_rows, mod_base, per_tile_mod, l, pw, is_last):
    n_tok = x2d.shape[0]
    tm = ROWS
    assert n_tok % tm == 0
    if per_tile_mod:
        mod_spec = pl.BlockSpec((1, 1, N_MOD * D_MODEL), lambda i, g: (mod_base + i, 0, 0))
    else:
        mod_spec = pl.BlockSpec((1, 1, N_MOD * D_MODEL), lambda i, g: (mod_base, 0, 0))

    def const(shape):
        return pl.BlockSpec(shape, lambda i, g: (0,) * len(shape), pipeline_mode=pl.Buffered(1))

    def grp(shape):
        return pl.BlockSpec((1, EXPERTS_PER_GROUP) + shape, lambda i, g: (l, g, 0, 0))

    return pl.pallas_call(
        functools.partial(_moe_body, is_last),
        grid=(n_tok // tm, N_EXPERT_GROUPS),
        in_specs=[
            pl.BlockSpec((tm, D_MODEL), lambda i, g: (i, 0)),
            mod_spec,
            pl.BlockSpec((1, 1, D_MODEL), lambda i, g: (l, 0, 0), pipeline_mode=pl.Buffered(1)),
            const((2 * D_MODEL, 2 * N_EXPERTS)),
            const((N_EXPERTS, 1)),
            const((tm, tm)),
            grp((D_MODEL, D_EXPERT)),
            grp((D_MODEL, D_EXPERT)),
            pl.BlockSpec((1, 1, EXPERTS_PER_GROUP * D_EXPERT, D_MODEL), lambda i, g: (l, g, 0, 0)),
            const((1, D_MODEL)),
        ],
        out_specs=pl.BlockSpec((tm, D_MODEL), lambda i, g: (i, 0)),
        out_shape=jax.ShapeDtypeStruct((n_tok, D_MODEL), F32),
        scratch_shapes=[
            pltpu.VMEM((tm, D_MODEL + MOE_EXT), BF16),
            pltpu.VMEM((SUBLANES, tm), F32),
            pltpu.VMEM((tm, MOE_BLK), F32),
            pltpu.VMEM((tm, MOE_CAP), BF16),
            pltpu.VMEM((MOE_CAP, D_MODEL), BF16),
            pltpu.SMEM((2 * N_EXPERT_GROUPS,), jnp.int32),
        ],
        compiler_params=pltpu.CompilerParams(dimension_semantics=("arbitrary", "arbitrary"),
                                             vmem_limit_bytes=VMEM_LIMIT),
        name="moe",
    )(x2d, mod_rows, pw["g2"], pw["wr2"], pw["brc"], pw["utri"], pw["weg"], pw["weu"], pw["wed"], pw["fg"])


def _dft_tables(t):
    idx = np.arange(t, dtype=np.int64)
    ang = 2.0 * np.pi * ((idx[:, None] * idx[None, :]) % t).astype(np.float64) / t
    scale = 1.0 / math.sqrt(t * D_FOURIER_GROUP)
    return (np.cos(ang) * scale).astype(np.float32), (-np.sin(ang) * scale).astype(np.float32)


def _group_tables():
    idx = np.arange(D_FOURIER_GROUP, dtype=np.int64)
    ang = 2.0 * np.pi * ((idx[:, None] * idx[None, :]) % D_FOURIER_GROUP).astype(np.float64) / D_FOURIER_GROUP
    eye = np.eye(N_FOURIER_GROUPS)
    return np.concatenate([np.kron(eye, np.cos(ang)), np.kron(eye, np.sin(ang))], axis=1).astype(np.float32)


def _grid_pos_tables(n_tokens, dtype):
    quarter = D_MODEL // 4
    omega = 1.0 / (10000.0 ** (jnp.arange(quarter, dtype=jnp.float32) / quarter))

    def enc(p):
        a = p[:, None] * omega[None, :]
        return jnp.concatenate([jnp.sin(a), jnp.cos(a)], axis=-1).astype(dtype)

    return (enc(jnp.arange(n_tokens // GRID_W, dtype=jnp.float32)), enc(jnp.arange(GRID_W, dtype=jnp.float32)))


def _router_hi_lo(w_router):
    w_hi = w_router.astype(BF16)
    w_lo = (w_router - w_hi.astype(F32)).astype(BF16)
    return jnp.concatenate([jnp.concatenate([w_hi, w_lo], axis=1),
                            jnp.concatenate([w_hi, jnp.zeros_like(w_lo)], axis=1)], axis=0)


def _prepare(seq_lens, norm1_g, norm2_g, w_in, w_fnet, w_dw, b_dw, conv_ln_g, conv_ln_b, w_pw, b_gate, g_mh,
             w_out, w_router, b_router, w_exp_gate, w_exp_up, w_exp_down, final_g):
    dft = {t: _dft_tables(t) for t in sorted(set(seq_lens))}
    w_in_b = w_in.astype(BF16)
    tok = np.arange(ROWS)
    return dict(
        g1=norm1_g.reshape(DEPTH, 1, D_MODEL), g2=norm2_g.reshape(DEPTH, 1, D_MODEL),
        win=jnp.swapaxes(w_in_b, 1, 2), bg=b_gate.reshape(DEPTH, 1, N_GATES),
        blk=jnp.asarray(_group_tables()).astype(BF16),
        dc={t: jnp.asarray(v[0]).astype(BF16) for t, v in dft.items()},
        ds={t: jnp.asarray(v[1]).astype(BF16) for t, v in dft.items()},
        wfn=w_fnet.astype(BF16),
        wdw=jnp.concatenate([w_dw, jnp.zeros((DEPTH, 1, D_CONV), F32)], axis=1),
        bdw=b_dw.reshape(DEPTH, 1, D_CONV), lng=conv_ln_g.reshape(DEPTH, 1, D_CONV),
        lnb=conv_ln_b.reshape(DEPTH, 1, D_CONV), wpw=w_pw.astype(BF16),
        gmh=jnp.broadcast_to(g_mh.reshape(DEPTH, D_MLSTM, 1), (DEPTH, D_MLSTM, CHUNK)),
        wout=w_out.astype(BF16),
        wr2=_router_hi_lo(w_router), brc=b_router.reshape(N_EXPERTS, 1),
        utri=jnp.asarray((tok[:, None] < tok[None, :]).astype(np.float32)).astype(BF16),
        weg=w_exp_gate.astype(BF16), weu=w_exp_up.astype(BF16),
        wed=w_exp_down.reshape(DEPTH, N_EXPERT_GROUPS, EXPERTS_PER_GROUP * D_EXPERT, D_MODEL).astype(BF16),
        fg=final_g.reshape(1, D_MODEL),
    )


def kernel(x_prompt, x_sample, state_C, state_n, state_m, c, c_ctx, w_ada, b_ada, norm1_g, norm2_g, w_in, w_fnet, w_dw, b_dw, conv_ln_g, conv_ln_b, w_pw, b_gate, g_mh, w_out, w_router, b_router, w_exp_gate, w_exp_up, w_exp_down, final_g):
    bp, tp, _ = x_prompt.shape
    bs, ts, _ = x_sample.shape

    cv = jnp.zeros((MOD_ROWS, D_MODEL), F32).at[:bs].set(c).at[bs].set(c_ctx)
    mod_all = _ada_call(cv, w_ada, b_ada)
    mod_rows = mod_all.reshape(DEPTH * MOD_ROWS, 1, N_MOD * D_MODEL)

    pw = _prepare((tp, ts), norm1_g, norm2_g, w_in, w_fnet, w_dw, b_dw, conv_ln_g, conv_ln_b, w_pw, b_gate,
                  g_mh, w_out, w_router, b_router, w_exp_gate, w_exp_up, w_exp_down, final_g)
    pos = _grid_pos_tables(ts, x_sample.dtype)

    c0 = state_C.reshape(bs, DEPTH, N_UNITS, HEAD_DIM_M, HEAD_DIM_M)
    n0 = state_n.reshape(bs, DEPTH, N_UNITS, HEAD_DIM_M)
    m0 = jnp.pad(state_m, ((0, 0), (0, 0), (0, 0), (0, GATE_ROWS - N_HEADS_M)))
    m0 = jnp.broadcast_to(m0[..., None], (bs, DEPTH, N_DIRS, GATE_ROWS, CHUNK))

    xp, xs = x_prompt, x_sample
    states = (jnp.zeros((bp, DEPTH, N_UNITS, HEAD_DIM_M, HEAD_DIM_M), F32),
              jnp.zeros((bp, DEPTH, N_UNITS, HEAD_DIM_M), F32),
              jnp.zeros((bp, DEPTH, N_DIRS, GATE_ROWS, CHUNK), F32))
    for l in range(DEPTH):
        is_last = l == DEPTH - 1
        base = l * MOD_ROWS
        xp, *states = _mixer_call(xp, None, mod_rows, base + bs, False, l, pw, None, tuple(states))
        xp = _moe_call(xp.reshape(bp * tp, D_MODEL), mod_rows, base + bs, False, l, pw, is_last)
        xp = xp.reshape(bp, tp, D_MODEL)

        (xs,) = _mixer_call(xs, pos if l == 0 else None, mod_rows, base, True, l, pw, (c0, n0, m0), None)
        xs = _moe_call(xs.reshape(bs * ts, D_MODEL), mod_rows, base, True, l, pw, is_last)
        xs = xs.reshape(bs, ts, D_MODEL)

    c_all, n_all, m_all = states
    new_c = c_all.reshape(bp, DEPTH, N_DIRS, N_HEADS_M, HEAD_DIM_M, HEAD_DIM_M)
    new_n = n_all.reshape(bp, DEPTH, N_DIRS, N_HEADS_M, HEAD_DIM_M)
    new_m = m_all[:, :, :, :N_HEADS_M, 0]
    return (xp, xs, new_c, new_n, new_m)
```
